```python
import math
import jax
import jax.numpy as jnp
from jax import lax
import numpy as np

D_MODEL = 2048
BATCH = 1
SEQ = 8192
DEPTH = 4

GRID_W = 64
CTX_LEN = 256
N_MIXERS = 4
MIX_WIDTH = D_MODEL
GROUP_WIDTH = MIX_WIDTH // N_MIXERS
D_FF = ((8 * D_MODEL // 3 + 255) // 256) * 256
N_MOD = 9
EPS = 1e-6

S5_CH = 16
S5_GROUPS = GROUP_WIDTH // S5_CH
S5_STATE = 64
S5_DT_MIN = 1e-3
S5_DT_MAX = 1e-1

DIFF_HEADS = 4
DIFF_HEAD_DIM = GROUP_WIDTH // (2 * DIFF_HEADS)
ROPE_BASE = 10000.0
Q_BLOCK = 128

GDN_HEADS = 4
GDN_HEAD_DIM = GROUP_WIDTH // GDN_HEADS
GDN_CONV = 5
GDN_CHUNK = 64

NA_HEADS = 8
NA_HEAD_DIM = GROUP_WIDTH // NA_HEADS
WIN_H = 8
WIN_W = 16
NA_KEY_COLS = 2 * WIN_W

S5_IN = GROUP_WIDTH
DIFF_IN = 3 * GROUP_WIDTH
GDN_IN = 4 * GROUP_WIDTH + 4 * GDN_HEADS
NA_IN = 3 * GROUP_WIDTH
N_IN = S5_IN + DIFF_IN + GDN_IN + NA_IN
IN_SPLITS = (S5_IN, S5_IN + DIFF_IN, S5_IN + DIFF_IN + GDN_IN)

kernel_name = 'hybrid_parallel_head_flow_block'


def rms_norm(x, g):
    xf = x.astype(jnp.float32)
    y = xf * lax.rsqrt(jnp.mean(xf * xf, axis=-1, keepdims=True) + EPS)
    return (y * g.astype(jnp.float32)).astype(x.dtype)


def adaln(h, g, shift, scale):
    return rms_norm(h, g) * (1 + scale) + shift


def swiglu(h, w_in, w_out):
    gate, up = jnp.split(h @ w_in, 2, axis=-1)
    return (jax.nn.silu(gate) * up) @ w_out


def l2norm(t):
    return t * lax.rsqrt(jnp.sum(t * t, axis=-1, keepdims=True) + EPS)


def axial_rope_tables(n_tok, head_dim):
    pos = jnp.arange(n_tok)
    row = (pos // GRID_W).astype(jnp.float32)
    col = (pos % GRID_W).astype(jnp.float32)
    n_freq = head_dim // 4
    inv = ROPE_BASE ** (-jnp.arange(n_freq, dtype=jnp.float32) / n_freq)
    ang_r = row[:, None] * inv
    ang_c = col[:, None] * inv
    ang = jnp.concatenate([ang_r, ang_r, ang_c, ang_c], axis=-1)
    return jnp.cos(ang), jnp.sin(ang)


def apply_axial_rope(x, cos, sin):
    extra = x.ndim - 3
    shp = cos.shape[:1] + (1,) * extra + cos.shape[1:]
    cos = cos.reshape(shp)
    sin = sin.reshape(shp)
    xf = x.astype(jnp.float32)
    a, b, c2, d = jnp.split(xf, 4, axis=-1)
    rot = jnp.concatenate([-b, a, -d, c2], axis=-1)
    return (xf * cos + rot * sin).astype(x.dtype)


def plain_attention(q, k, v):
    s = jnp.einsum('bqhd,bkhd->bhqk', q, k).astype(jnp.float32) * (q.shape[-1] ** -0.5)
    p = jax.nn.softmax(s, axis=-1)
    return jnp.einsum('bhqk,bkhd->bqhd', p.astype(v.dtype), v)


def _complex_affine_combine(e1, e2):
    a1r, a1i, b1r, b1i = e1
    a2r, a2i, b2r, b2i = e2
    return (a2r * a1r - a2i * a1i,
            a2r * a1i + a2i * a1r,
            a2r * b1r - a2i * b1i + b2r,
            a2r * b1i + a2i * b1r + b2i)


def s5_discretize(lam_re, lam_im, log_step, b_re, b_im):
    lr = lam_re.astype(jnp.float32)
    li = lam_im.astype(jnp.float32)
    dt = jnp.exp(log_step.astype(jnp.float32))[:, None]
    mag = jnp.exp(lr * dt)
    ar = mag * jnp.cos(li * dt)
    ai = mag * jnp.sin(li * dt)
    den = lr * lr + li * li
    fr = ((ar - 1.0) * lr + ai * li) / den
    fi = (ai * lr - (ar - 1.0) * li) / den
    br = b_re.astype(jnp.float32)
    bi = b_im.astype(jnp.float32)
    bbr = fr[..., None] * br - fi[..., None] * bi
    bbi = fr[..., None] * bi + fi[..., None] * br
    return ar, ai, bbr, bbi


def s5_scan(u, ar, ai, bbr, bbi, h0):
    br = jnp.einsum('blgn,gpn->blgp', u, bbr)
    bi = jnp.einsum('blgn,gpn->blgp', u, bbi)
    if h0 is not None:
        h0r, h0i = h0
        br = br.at[:, 0].add(ar * h0r - ai * h0i)
        bi = bi.at[:, 0].add(ar * h0i + ai * h0r)
    a_r = jnp.broadcast_to(ar, br.shape)
    a_i = jnp.broadcast_to(ai, bi.shape)
    _, _, hr, hi = lax.associative_scan(_complex_affine_combine, (a_r, a_i, br, bi), axis=1)
    return hr, hi


def s5_readout(hr, hi, c_re, c_im):
    return (jnp.einsum('blgp,gnp->blgn', hr, c_re.astype(jnp.float32))
            - jnp.einsum('blgp,gnp->blgn', hi, c_im.astype(jnp.float32)))


def s5_mixer(u_c, u_l, lam_re, lam_im, log_step, b_re, b_im, c_re, c_im,
             d_skip, w_glu, b_glu, need_ctx):
    def grp(u):
        return u.astype(jnp.float32).reshape(u.shape[0], u.shape[1], S5_GROUPS, S5_CH)

    gc_in, gl_in = grp(u_c), grp(u_l)
    y_c, y_l = 0.0, 0.0
    for d in (0, 1):
        fl = (lambda t: jnp.flip(t, 1)) if d else (lambda t: t)
        ar, ai, bbr, bbi = s5_discretize(lam_re[d], lam_im[d], log_step[d], b_re[d], b_im[d])
        hcr, hci = s5_scan(fl(gc_in), ar, ai, bbr, bbi, None)
        hlr, hli = s5_scan(fl(gl_in), ar, ai, bbr, bbi, (hcr[:, -1], hci[:, -1]))
        y_l = y_l + fl(s5_readout(hlr, hli, c_re[d], c_im[d]))
        if need_ctx:
            y_c = y_c + fl(s5_readout(hcr, hci, c_re[d], c_im[d]))

    def finish(y, u):
        uf = u.astype(jnp.float32)
        y = y.reshape(u.shape) + d_skip.astype(jnp.float32) * uf
        h = jax.nn.gelu(y) @ w_glu.astype(jnp.float32) + b_glu.astype(jnp.float32)
        a, b = jnp.split(h, 2, axis=-1)
        return (a * jax.nn.sigmoid(b)).astype(u.dtype)

    out_c = finish(y_c, u_c) if need_ctx else None
    return out_c, finish(y_l, u_l)


def diff_attention(p_c, p_l, cos, sin, q_norm, k_norm, lq1, lk1, lq2, lk2, subln,
                   lam_init, need_ctx):
    def heads(p):
        bsz, n = p.shape[:2]
        q, k, v = jnp.split(p, 3, axis=-1)
        q = rms_norm(q.reshape(bsz, n, DIFF_HEADS, 2, DIFF_HEAD_DIM), q_norm)
        k = rms_norm(k.reshape(bsz, n, DIFF_HEADS, 2, DIFF_HEAD_DIM), k_norm)
        v = v.reshape(bsz, n, DIFF_HEADS, 2 * DIFF_HEAD_DIM)
        return q, k, v

    qc, kc, vc = heads(p_c)
    ql, kl, vl = heads(p_l)
    ql = apply_axial_rope(ql, cos, sin)
    kl = apply_axial_rope(kl, cos, sin)
    f = jnp.float32
    lam = (jnp.exp(jnp.sum(lq1.astype(f) * lk1.astype(f)))
           - jnp.exp(jnp.sum(lq2.astype(f) * lk2.astype(f))) + lam_init)
    scale = DIFF_HEAD_DIM ** -0.5

    def attend(q, k, v):
        s = jnp.einsum('bqhcd,bkhcd->bhcqk', q, k).astype(jnp.float32) * scale
        p = jax.nn.softmax(s, axis=-1)
        a = p[:, :, 0] - lam * p[:, :, 1]
        return jnp.einsum('bhqk,bkhe->bqhe', a.astype(v.dtype), v)

    k_all = jnp.concatenate([kc, kl], axis=1)
    v_all = jnp.concatenate([vc, vl], axis=1)
    bsz, n_tok = ql.shape[:2]
    nb = n_tok // Q_BLOCK
    qb = jnp.moveaxis(ql.reshape(bsz, nb, Q_BLOCK, DIFF_HEADS, 2, DIFF_HEAD_DIM), 1, 0)
    ob = lax.map(lambda q: attend(q, k_all, v_all), qb)
    o_l = jnp.moveaxis(ob, 0, 1).reshape(bsz, n_tok, DIFF_HEADS, 2 * DIFF_HEAD_DIM)

    def finish(o):
        return (rms_norm(o, subln) * (1.0 - lam_init)).reshape(o.shape[0], o.shape[1], GROUP_WIDTH)

    out_c = finish(attend(qc, kc, vc)) if need_ctx else None
    return out_c, finish(o_l)


def short_conv(x, w):
    width = w.shape[0]
    return lax.conv_general_dilated(
        x, w[:, None, :].astype(x.dtype), window_strides=(1,),
        padding=[(width // 2, width // 2)],
        dimension_numbers=('NWC', 'WIO', 'NWC'),
        feature_group_count=x.shape[-1])


def gated_delta_chunked(q, k, v, g, beta, s0, with_output):
    bsz, n_tok, nh, dk = k.shape
    dv = v.shape[-1]
    n = n_tok // GDN_CHUNK

    def chunks(t):
        t = t.reshape((bsz, n, GDN_CHUNK, nh) + t.shape[3:])
        return jnp.moveaxis(t, (1, 3), (0, 2))

    k_c, v_c, g_c, b_c = chunks(k), chunks(v), chunks(g), chunks(beta)
    g_cum = jnp.cumsum(g_c, axis=-1)
    idx = jnp.arange(GDN_CHUNK)
    incl = idx[:, None] >= idx[None, :]
    strict = idx[:, None] > idx[None, :]
    diff = g_cum[..., :, None] - g_cum[..., None, :]
    decay = jnp.where(incl, jnp.exp(jnp.where(incl, diff, 0.0)), 0.0)
    k_beta = k_c * b_c[..., None]
    a_mat = jnp.where(strict, jnp.einsum('nbhid,nbhjd->nbhij', k_beta, k_c) * decay, 0.0)
    eye = jnp.eye(GDN_CHUNK, dtype=jnp.float32)
    t_mat = lax.linalg.triangular_solve(eye + a_mat, jnp.broadcast_to(eye, a_mat.shape),
                                        left_side=True, lower=True)
    w = t_mat @ (k_beta * jnp.exp(g_cum)[..., None])
    u = t_mat @ (v_c * b_c[..., None])
    g_last = g_cum[..., -1]
    k_carry = k_c * jnp.exp(g_last[..., None] - g_cum)[..., None]

    def advance(S, k_i, u_i, w_i, gl):
        v_new = u_i - w_i @ S
        S_new = S * jnp.exp(gl)[..., None, None] + jnp.einsum('bhcd,bhce->bhde', k_i, v_new)
        return v_new, S_new

    if with_output:
        q_c = chunks(q)
        attn = jnp.where(incl, jnp.einsum('nbhid,nbhjd->nbhij', q_c, k_c) * decay, 0.0)
        q_dec = q_c * jnp.exp(g_cum)[..., None]

        def step(S, inp):
            k_i, u_i, w_i, gl, qd_i, at_i = inp
            v_new, S_new = advance(S, k_i, u_i, w_i, gl)
            o = qd_i @ S + at_i @ v_new
            return S_new, o

        S, o = lax.scan(step, s0, (k_carry, u, w, g_last, q_dec, attn))
        o = jnp.moveaxis(o, (0, 2), (1, 3)).reshape(bsz, n_tok, nh, dv)
        return o, S

    def step_state(S, inp):
        k_i, u_i, w_i, gl = inp
        _, S_new = advance(S, k_i, u_i, w_i, gl)
        return S_new, None

    S, _ = lax.scan(step_state, s0, (k_carry, u, w, g_last))
    return None, S


def gdn_mixer(p_c, p_l, conv_w, a_log, dt_bias, norm_w, need_ctx):
    W, H, dh = GROUP_WIDTH, GDN_HEADS, GDN_HEAD_DIM

    def prep(p):
        bsz, n = p.shape[:2]
        qkv, z, a, b = jnp.split(p, [3 * W, 4 * W, 4 * W + 2 * H], axis=-1)
        qkv = jax.nn.silu(short_conv(qkv, conv_w)).astype(jnp.float32)
        q, k, v = [t.reshape(bsz, n, H, dh) for t in jnp.split(qkv, 3, axis=-1)]
        q = l2norm(q) * (dh ** -0.5)
        k = l2norm(k)
        a = a.astype(jnp.float32).reshape(bsz, n, 2, H)
        b = b.astype(jnp.float32).reshape(bsz, n, 2, H)
        g = -jnp.exp(a_log.astype(jnp.float32)) * jax.nn.softplus(a + dt_bias.astype(jnp.float32))
        return q, k, v, g, jax.nn.sigmoid(b), z

    qc, kc, vc, gc, bc, zc = prep(p_c)
    ql, kl, vl, gl, bl, zl = prep(p_l)
    bsz = p_l.shape[0]
    s0 = jnp.zeros((bsz, H, dh, dh), jnp.float32)
    o_c, o_l = 0.0, 0.0
    for d in (0, 1):
        fl = (lambda t: jnp.flip(t, 1)) if d else (lambda t: t)
        oc_d, s_ctx = gated_delta_chunked(fl(qc), fl(kc), fl(vc), fl(gc[:, :, d]), fl(bc[:, :, d]),
                                          s0, need_ctx)
        ol_d, _ = gated_delta_chunked(fl(ql), fl(kl), fl(vl), fl(gl[:, :, d]), fl(bl[:, :, d]),
                                      s_ctx, True)
        o_l = o_l + fl(ol_d)
        if need_ctx:
            o_c = o_c + fl(oc_d)

    def finish(o, z):
        bz, n = z.shape[:2]
        y = rms_norm(o, norm_w) * jax.nn.silu(z.astype(jnp.float32).reshape(bz, n, H, dh))
        return y.reshape(bz, n, W).astype(z.dtype)

    out_c = finish(o_c, zc) if need_ctx else None
    return out_c, finish(o_l, zl)


def neighborhood_attention(p_c, p_l, q_norm, k_norm, rpb, need_ctx):
    def heads(p):
        bsz, n = p.shape[:2]
        q, k, v = [t.reshape(bsz, n, NA_HEADS, NA_HEAD_DIM) for t in jnp.split(p, 3, axis=-1)]
        return rms_norm(q, q_norm), rms_norm(k, k_norm), v

    qc, kc, vc = heads(p_c)
    ql, kl, vl = heads(p_l)
    bsz, n_tok = p_l.shape[:2]
    rows = n_tok // GRID_W
    kh = min(WIN_H, rows)
    n_cb = GRID_W // WIN_W
    scale = NA_HEAD_DIM ** -0.5
    r = jnp.arange(rows)
    row0 = jnp.clip(r - kh // 2, 0, rows - kh)
    key_rows = row0[:, None] + jnp.arange(kh)
    q_cols = (jnp.arange(n_cb) * WIN_W)[:, None] + jnp.arange(WIN_W)
    col0 = jnp.clip(q_cols - WIN_W // 2, 0, GRID_W - WIN_W)
    kcol0 = jnp.clip(jnp.arange(n_cb) * WIN_W - WIN_W // 2, 0, GRID_W - NA_KEY_COLS)
    key_cols = kcol0[:, None] + jnp.arange(NA_KEY_COLS)
    nk = kh * NA_KEY_COLS
    key_idx = (key_rows[:, None, :, None] * GRID_W
               + key_cols[None, :, None, :]).reshape(rows, n_cb, nk)
    kg = kl[:, key_idx]
    vg = vl[:, key_idx]
    qb = ql.reshape(bsz, rows, n_cb, WIN_W, NA_HEADS, NA_HEAD_DIM)
    s_loc = jnp.einsum('brjqhd,brjkhd->bhrjqk', qb, kg).astype(jnp.float32) * scale
    dcol = key_cols[:, None, :] - q_cols[:, :, None]
    col_ok = (key_cols[:, None, :] >= col0[:, :, None]) & (key_cols[:, None, :] < col0[:, :, None] + WIN_W)
    drow = key_rows - r[:, None]
    ri = (drow + WIN_H - 1)[:, None, None, :, None]
    ci = jnp.clip(dcol + WIN_W - 1, 0, 2 * WIN_W - 2)[None, :, :, None, :]
    bias = rpb[:, ri, ci].reshape(NA_HEADS, rows, n_cb, WIN_W, nk).astype(jnp.float32)
    mask = jnp.broadcast_to(col_ok[:, :, None, :], (n_cb, WIN_W, kh, NA_KEY_COLS)).reshape(n_cb, WIN_W, nk)
    s_loc = jnp.where(mask, s_loc + bias[None], -jnp.inf)
    s_ctx = jnp.einsum('brjqhd,bkhd->bhrjqk', qb, kc).astype(jnp.float32) * scale
    p = jax.nn.softmax(jnp.concatenate([s_loc, s_ctx], axis=-1), axis=-1)
    o = (jnp.einsum('bhrjqk,brjkhd->brjqhd', p[..., :nk].astype(vl.dtype), vg)
         + jnp.einsum('bhrjqk,bkhd->brjqhd', p[..., nk:].astype(vc.dtype), vc))
    o_l = o.reshape(bsz, n_tok, GROUP_WIDTH)
    out_c = plain_attention(qc, kc, vc).reshape(p_c.shape[0], p_c.shape[1], GROUP_WIDTH) if need_ctx else None
    return out_c, o_l


def setup_inputs(seed: int = 0) -> dict:
    key = jax.random.key(seed)
    keys = iter(jax.random.split(key, 48))

    def nrm(shape, std):
        return std * jax.random.normal(next(keys), shape, jnp.float32)

    def unif(shape, lo, hi):
        return jax.random.uniform(next(keys), shape, jnp.float32, lo, hi)

    L, D, F, W = DEPTH, D_MODEL, D_FF, GROUP_WIDTH
    G, P, N = S5_GROUPS, S5_STATE, S5_CH
    x = nrm((BATCH, SEQ, D), 1.0)
    c = nrm((BATCH, D), 1.0)
    ctx = nrm((BATCH, CTX_LEN, D), 1.0)
    c_ctx = nrm((D,), 1.0)
    w_ada = nrm((L, D, N_MOD * D), D ** -0.5)
    b_ada = nrm((L, N_MOD * D), 0.02)
    norm_ffn1 = 1.0 + nrm((L, D), 0.02)
    norm_mix = 1.0 + nrm((L, D), 0.02)
    norm_ffn2 = 1.0 + nrm((L, D), 0.02)
    ffn1_w_in = nrm((L, D, 2 * F), D ** -0.5)
    ffn1_w_out = nrm((L, F, D), F ** -0.5)
    ffn2_w_in = nrm((L, D, 2 * F), D ** -0.5)
    ffn2_w_out = nrm((L, F, D), F ** -0.5)
    w_in = nrm((L, D, N_IN), D ** -0.5)
    w_out = nrm((L, MIX_WIDTH, D), MIX_WIDTH ** -0.5)
    s5_lambda_re = -0.5 + nrm((L, 2, G, P), 0.01)
    s5_lambda_im = math.pi * jnp.arange(P, dtype=jnp.float32) + nrm((L, 2, G, P), 0.01)
    s5_log_step = unif((L, 2, G), math.log(S5_DT_MIN), math.log(S5_DT_MAX))
    s5_b_re = nrm((L, 2, G, P, N), (2 * N) ** -0.5)
    s5_b_im = nrm((L, 2, G, P, N), (2 * N) ** -0.5)
    s5_c_re = nrm((L, 2, G, N, P), (2 * P) ** -0.5)
    s5_c_im = nrm((L, 2, G, N, P), (2 * P) ** -0.5)
    s5_d = nrm((L, W), 1.0)
    s5_w_glu = nrm((L, W, 2 * W), W ** -0.5)
    s5_b_glu = nrm((L, 2 * W), 0.02)
    diff_q_norm = 1.0 + nrm((L, DIFF_HEAD_DIM), 0.02)
    diff_k_norm = 1.0 + nrm((L, DIFF_HEAD_DIM), 0.02)
    diff_lambda_q1 = nrm((L, DIFF_HEAD_DIM), 0.1)
    diff_lambda_k1 = nrm((L, DIFF_HEAD_DIM), 0.1)
    diff_lambda_q2 = nrm((L, DIFF_HEAD_DIM), 0.1)
    diff_lambda_k2 = nrm((L, DIFF_HEAD_DIM), 0.1)
    diff_subln = 1.0 + nrm((L, 2 * DIFF_HEAD_DIM), 0.02)
    gdn_conv = nrm((L, GDN_CONV, 3 * W), GDN_CONV ** -0.5)
    gdn_a_log = jnp.log(unif((L, 2, GDN_HEADS), 1.0, 16.0))
    dt = jnp.exp(unif((L, 2, GDN_HEADS), math.log(1e-3), math.log(1e-1)))
    gdn_dt_bias = dt + jnp.log(-jnp.expm1(-dt))
    gdn_norm = 1.0 + nrm((L, GDN_HEAD_DIM), 0.02)
    na_q_norm = 1.0 + nrm((L, NA_HEAD_DIM), 0.02)
    na_k_norm = 1.0 + nrm((L, NA_HEAD_DIM), 0.02)
    na_rpb = nrm((L, NA_HEADS, 2 * WIN_H - 1, 2 * WIN_W - 1), 0.05)
    return {'x': x, 'c': c, 'ctx': ctx, 'c_ctx': c_ctx, 'w_ada': w_ada, 'b_ada': b_ada,
            'norm_ffn1': norm_ffn1, 'norm_mix': norm_mix, 'norm_ffn2': norm_ffn2,
            'ffn1_w_in': ffn1_w_in, 'ffn1_w_out': ffn1_w_out,
            'ffn2_w_in': ffn2_w_in, 'ffn2_w_out': ffn2_w_out,
            'w_in': w_in, 'w_out': w_out,
            's5_lambda_re': s5_lambda_re, 's5_lambda_im': s5_lambda_im, 's5_log_step': s5_log_step,
            's5_b_re': s5_b_re, 's5_b_im': s5_b_im, 's5_c_re': s5_c_re, 's5_c_im': s5_c_im,
            's5_d': s5_d, 's5_w_glu': s5_w_glu, 's5_b_glu': s5_b_glu,
            'diff_q_norm': diff_q_norm, 'diff_k_norm': diff_k_norm,
            'diff_lambda_q1': diff_lambda_q1, 'diff_lambda_k1': diff_lambda_k1,
            'diff_lambda_q2': diff_lambda_q2, 'diff_lambda_k2': diff_lambda_k2,
            'diff_subln': diff_subln,
            'gdn_conv': gdn_conv, 'gdn_a_log': gdn_a_log, 'gdn_dt_bias': gdn_dt_bias, 'gdn_norm': gdn_norm,
            'na_q_norm': na_q_norm, 'na_k_norm': na_k_norm, 'na_rpb': na_rpb}


def reference(x, c, ctx, c_ctx, w_ada, b_ada, norm_ffn1, norm_mix, norm_ffn2,
              ffn1_w_in, ffn1_w_out, ffn2_w_in, ffn2_w_out, w_in, w_out,
              s5_lambda_re, s5_lambda_im, s5_log_step, s5_b_re, s5_b_im, s5_c_re, s5_c_im,
              s5_d, s5_w_glu, s5_b_glu,
              diff_q_norm, diff_k_norm, diff_lambda_q1, diff_lambda_k1,
              diff_lambda_q2, diff_lambda_k2, diff_subln,
              gdn_conv, gdn_a_log, gdn_dt_bias, gdn_norm,
              na_q_norm, na_k_norm, na_rpb):
    bsz, n_tok, _ = x.shape
    cos, sin = axial_rope_tables(n_tok, DIFF_HEAD_DIM)
    silu_c = jax.nn.silu(c)
    silu_cc = jax.nn.silu(c_ctx)
    xc = ctx
    for l in range(DEPTH):
        need_ctx = l != DEPTH - 1
        mod = (silu_c @ w_ada[l] + b_ada[l]).reshape(bsz, N_MOD, 1, D_MODEL)
        mod_c = (silu_cc @ w_ada[l] + b_ada[l]).reshape(N_MOD, 1, 1, D_MODEL)
        x = x + 0.5 * mod[:, 2] * swiglu(adaln(x, norm_ffn1[l], mod[:, 0], mod[:, 1]),
                                         ffn1_w_in[l], ffn1_w_out[l])
        xc = xc + 0.5 * mod_c[2] * swiglu(adaln(xc, norm_ffn1[l], mod_c[0], mod_c[1]),
                                          ffn1_w_in[l], ffn1_w_out[l])
        p_lat = adaln(x, norm_mix[l], mod[:, 3], mod[:, 4]) @ w_in[l]
        p_ctx = adaln(xc, norm_mix[l], mod_c[3], mod_c[4]) @ w_in[l]
        in_s5_l, in_diff_l, in_gdn_l, in_na_l = jnp.split(p_lat, IN_SPLITS, axis=-1)
        in_s5_c, in_diff_c, in_gdn_c, in_na_c = jnp.split(p_ctx, IN_SPLITS, axis=-1)
        lam_init = 0.8 - 0.6 * math.exp(-0.3 * l)
        ya_c, ya_l = s5_mixer(in_s5_c, in_s5_l, s5_lambda_re[l], s5_lambda_im[l], s5_log_step[l],
                              s5_b_re[l], s5_b_im[l], s5_c_re[l], s5_c_im[l],
                              s5_d[l], s5_w_glu[l], s5_b_glu[l], need_ctx)
        yb_c, yb_l = diff_attention(in_diff_c, in_diff_l, cos, sin, diff_q_norm[l], diff_k_norm[l],
                                    diff_lambda_q1[l], diff_lambda_k1[l], diff_lambda_q2[l],
                                    diff_lambda_k2[l], diff_subln[l], lam_init, need_ctx)
        yc_c, yc_l = gdn_mixer(in_gdn_c, in_gdn_l, gdn_conv[l], gdn_a_log[l], gdn_dt_bias[l],
                               gdn_norm[l], need_ctx)
        yd_c, yd_l = neighborhood_attention(in_na_c, in_na_l, na_q_norm[l], na_k_norm[l],
                                            na_rpb[l], need_ctx)
        x = x + mod[:, 5] * (jnp.concatenate([ya_l, yb_l, yc_l, yd_l], axis=-1) @ w_out[l])
        x = x + 0.5 * mod[:, 8] * swiglu(adaln(x, norm_ffn2[l], mod[:, 6], mod[:, 7]),
                                         ffn2_w_in[l], ffn2_w_out[l])
        if need_ctx:
            xc = xc + mod_c[5] * (jnp.concatenate([ya_c, yb_c, yc_c, yd_c], axis=-1) @ w_out[l])
            xc = xc + 0.5 * mod_c[8] * swiglu(adaln(xc, norm_ffn2[l], mod_c[6], mod_c[7]),
                                              ffn2_w_in[l], ffn2_w_out[l])
    return x
```

```python
import functools
import math

import jax
import jax.numpy as jnp
from jax import lax
from jax.experimental import pallas as pl
from jax.experimental.pallas import tpu as pltpu

D_MODEL = 2048
SEQ = 8192
DEPTH = 4
GRID_W = 64
CTX_LEN = 256
ROWS = CTX_LEN + SEQ
GROUP_WIDTH = 512
D_FF = 5632
N_MOD = 9
EPS = 1e-6

S5_CH = 16
S5_GROUPS = GROUP_WIDTH // S5_CH
S5_STATE = 64
DIFF_HEADS = 4
DIFF_HEAD_DIM = 64
ROPE_BASE = 10000.0
Q_BLOCK = 128
GDN_HEADS = 4
GDN_HEAD_DIM = 128
GDN_CONV = 5
GDN_CHUNK = 64
NA_HEADS = 8
NA_HEAD_DIM = 64
WIN_H = 8
WIN_W = 16
NA_KEY_COLS = 2 * WIN_W

P_S5 = 0
P_DIFF = 512
P_GDN_QKV = 2048
P_GDN_Z = 3584
P_NA = 4096
P_GDN_AB = 5632
P_WIDTH = 5760

VMEM_LIMIT_BYTES = 56 * 1024 * 1024

F32 = jnp.float32
BF16 = jnp.bfloat16


MOD_TN = 1024


def _mod_kernel(ct_ref, w_ref, b_ref, o_ref):
    c = ct_ref[...]
    s = c * jax.nn.sigmoid(c)
    v0 = jnp.broadcast_to(s[:, 0:1], (D_MODEL, 128))
    v1 = jnp.broadcast_to(s[:, 1:2], (D_MODEL, 128))
    for j in range(MOD_TN // 128):
        w = w_ref[:, j * 128:(j + 1) * 128]
        b = b_ref[:, j * 128:(j + 1) * 128]
        o_ref[0:1, j * 128:(j + 1) * 128] = jnp.sum(w * v0, axis=0, keepdims=True) + b
        o_ref[1:2, j * 128:(j + 1) * 128] = jnp.sum(w * v1, axis=0, keepdims=True) + b


def _modulation(c, c_ctx, w_ada, b_ada):
    n = N_MOD * D_MODEL
    ct = jnp.stack([c.reshape(D_MODEL), c_ctx.reshape(D_MODEL)], axis=1)
    out = pl.pallas_call(
        _mod_kernel,
        grid=(DEPTH, n // MOD_TN),
        in_specs=[
            pl.BlockSpec((D_MODEL, 2), lambda l, j: (0, 0)),
            pl.BlockSpec((None, D_MODEL, MOD_TN), lambda l, j: (l, 0, j)),
            pl.BlockSpec((None, 1, MOD_TN), lambda l, j: (l, 0, j)),
        ],
        out_specs=pl.BlockSpec((None, 2, MOD_TN), lambda l, j: (l, 0, j)),
        out_shape=jax.ShapeDtypeStruct((DEPTH, 2, n), F32),
        compiler_params=pltpu.CompilerParams(
            dimension_semantics=("arbitrary", "arbitrary"), vmem_limit_bytes=VMEM_LIMIT_BYTES),
        name="adaln_mod",
    )(ct, w_ada, b_ada.reshape(DEPTH, 1, n))
    return out.reshape(DEPTH, 2 * N_MOD, D_MODEL)


def _is_ctx_rows(tile_rows):
    rows = pl.program_id(0) * tile_rows + lax.broadcasted_iota(jnp.int32, (tile_rows, 1), 0)
    return rows < CTX_LEN


def _mod_row(mod_ref, is_ctx, k):
    return jnp.where(is_ctx, mod_ref[N_MOD + k:N_MOD + k + 1, :], mod_ref[k:k + 1, :])


def _adaln_rows(x, gamma, mod_ref, is_ctx, base):
    ms = jnp.mean(x * x, axis=-1, keepdims=True)
    y = x * lax.rsqrt(ms + EPS) * gamma
    return y * (1.0 + _mod_row(mod_ref, is_ctx, base + 1)) + _mod_row(mod_ref, is_ctx, base)


NORM_ROWS = 16


def _adaln_to_scratch(x_ref, gamma_ref, mod_ref, h_ref, tile_rows, base):
    row0 = pl.program_id(0) * tile_rows

    def body(r, carry):
        lo = pl.multiple_of(r * NORM_ROWS, NORM_ROWS)
        rows = row0 + lo + lax.broadcasted_iota(jnp.int32, (NORM_ROWS, 1), 0)
        h = _adaln_rows(x_ref[pl.ds(lo, NORM_ROWS), :], gamma_ref[...], mod_ref, rows < CTX_LEN, base)
        h_ref[pl.ds(lo, NORM_ROWS), :] = h.astype(BF16)
        return carry

    lax.fori_loop(0, tile_rows // NORM_ROWS, body, 0)


FFN_TM = 1408
FFN_RC = 352
FFN_TF = 512


def _ffn_kernel(x_ref, mod_ref, gamma_ref, wg_ref, wu_ref, wo_ref, o_ref, h_ref, *, base):
    f = pl.program_id(1)
    last = pl.num_programs(1) - 1
    row0 = pl.program_id(0) * FFN_TM

    @pl.when(f == 0)
    def _():
        _adaln_to_scratch(x_ref, gamma_ref, mod_ref, h_ref, FFN_TM, base)

    def chunk(r, carry):
        lo = pl.multiple_of(r * FFN_RC, FFN_RC)
        h = h_ref[pl.ds(lo, FFN_RC), :]
        g = jnp.dot(h, wg_ref[...], preferred_element_type=F32)
        u = jnp.dot(h, wu_ref[...], preferred_element_type=F32)
        a = (g * jax.nn.sigmoid(g) * u).astype(BF16)
        d = jnp.dot(a, wo_ref[...], preferred_element_type=F32)

        @pl.when(f == 0)
        def _():
            o_ref[pl.ds(lo, FFN_RC), :] = d

        @pl.when(jnp.logical_and(f > 0, f < last))
        def _():
            o_ref[pl.ds(lo, FFN_RC), :] += d

        @pl.when(f == last)
        def _():
            rows = row0 + lo + lax.broadcasted_iota(jnp.int32, (FFN_RC, 1), 0)
            gate = _mod_row(mod_ref, rows < CTX_LEN, base + 2)
            o_ref[pl.ds(lo, FFN_RC), :] = (
                x_ref[pl.ds(lo, FFN_RC), :] + 0.5 * gate * (o_ref[pl.ds(lo, FFN_RC), :] + d))
        return carry

    lax.fori_loop(0, FFN_TM // FFN_RC, chunk, 0)


def _ffn(s, mod_l, gamma, w_in_bf, w_out_bf, base):
    nf = D_FF // FFN_TF
    return pl.pallas_call(
        functools.partial(_ffn_kernel, base=base),
        grid=(ROWS // FFN_TM, nf),
        in_specs=[
            pl.BlockSpec((FFN_TM, D_MODEL), lambda i, f: (i, 0), pipeline_mode=pl.Buffered(1)),
            pl.BlockSpec((2 * N_MOD, D_MODEL), lambda i, f: (0, 0)),
            pl.BlockSpec((1, D_MODEL), lambda i, f: (0, 0)),
            pl.BlockSpec((D_MODEL, FFN_TF), lambda i, f: (0, f)),
            pl.BlockSpec((D_MODEL, FFN_TF), lambda i, f: (0, nf + f)),
            pl.BlockSpec((FFN_TF, D_MODEL), lambda i, f: (f, 0)),
        ],
        out_specs=pl.BlockSpec((FFN_TM, D_MODEL), lambda i, f: (i, 0), pipeline_mode=pl.Buffered(1)),
        out_shape=jax.ShapeDtypeStruct((ROWS, D_MODEL), F32),
        scratch_shapes=[pltpu.VMEM((FFN_TM, D_MODEL), BF16)],
        compiler_params=pltpu.CompilerParams(
            dimension_semantics=("arbitrary", "arbitrary"), vmem_limit_bytes=VMEM_LIMIT_BYTES),
        name="ffn_swiglu",
    )(s, mod_l, gamma.reshape(1, D_MODEL), w_in_bf, w_in_bf, w_out_bf)


INP_TM = 1408
INP_RC = 352
INP_TN = 640


def _inproj_kernel(x_ref, mod_ref, gamma_ref, w_ref, o_ref, h_ref):
    @pl.when(pl.program_id(1) == 0)
    def _():
        _adaln_to_scratch(x_ref, gamma_ref, mod_ref, h_ref, INP_TM, 3)

    def chunk(r, carry):
        lo = pl.multiple_of(r * INP_RC, INP_RC)
        o_ref[pl.ds(lo, INP_RC), :] = jnp.dot(h_ref[pl.ds(lo, INP_RC), :], w_ref[...],
                                              preferred_element_type=F32)
        return carry

    lax.fori_loop(0, INP_TM // INP_RC, chunk, 0)


def _inproj(s, mod_l, gamma, w_bf):
    return pl.pallas_call(
        _inproj_kernel,
        grid=(ROWS // INP_TM, P_WIDTH // INP_TN),
        in_specs=[
            pl.BlockSpec((INP_TM, D_MODEL), lambda i, n: (i, 0), pipeline_mode=pl.Buffered(1)),
            pl.BlockSpec((2 * N_MOD, D_MODEL), lambda i, n: (0, 0)),
            pl.BlockSpec((1, D_MODEL), lambda i, n: (0, 0)),
            pl.BlockSpec((D_MODEL, INP_TN), lambda i, n: (0, n)),
        ],
        out_specs=pl.BlockSpec((INP_TM, INP_TN), lambda i, n: (i, n)),
        out_shape=jax.ShapeDtypeStruct((ROWS, P_WIDTH), F32),
        scratch_shapes=[pltpu.VMEM((INP_TM, D_MODEL), BF16)],
        compiler_params=pltpu.CompilerParams(
            dimension_semantics=("arbitrary", "arbitrary"), vmem_limit_bytes=VMEM_LIMIT_BYTES),
        name="in_proj",
    )(s, mod_l, gamma.reshape(1, D_MODEL), w_bf)


OUT_TM = 384


def _outproj_kernel(x_ref, mod_ref, ya_ref, yb_ref, yc_ref, yd_ref, w_ref, o_ref):
    is_ctx = _is_ctx_rows(OUT_TM)
    acc = jnp.zeros((OUT_TM, D_MODEL), F32)
    for k, y_ref in enumerate((ya_ref, yb_ref, yc_ref, yd_ref)):
        acc += jnp.dot(y_ref[...].astype(BF16), w_ref[k * GROUP_WIDTH:(k + 1) * GROUP_WIDTH, :],
                       preferred_element_type=F32)
    o_ref[...] = x_ref[...] + _mod_row(mod_ref, is_ctx, 5) * acc


def _outproj(s, mod_l, ys, w_bf):
    yspec = pl.BlockSpec((OUT_TM, GROUP_WIDTH), lambda i: (i, 0))
    return pl.pallas_call(
        _outproj_kernel,
        grid=(ROWS // OUT_TM,),
        in_specs=[
            pl.BlockSpec((OUT_TM, D_MODEL), lambda i: (i, 0)),
            pl.BlockSpec((2 * N_MOD, D_MODEL), lambda i: (0, 0)),
            yspec, yspec, yspec, yspec,
            pl.BlockSpec((D_MODEL, D_MODEL), lambda i: (0, 0)),
        ],
        out_specs=pl.BlockSpec((OUT_TM, D_MODEL), lambda i: (i, 0)),
        out_shape=jax.ShapeDtypeStruct((ROWS, D_MODEL), F32),
        compiler_params=pltpu.CompilerParams(
            dimension_semantics=("arbitrary",), vmem_limit_bytes=VMEM_LIMIT_BYTES),
        name="out_proj",
    )(s, mod_l, *ys, w_bf)


def _rms_norm(x, g):
    xf = x.astype(F32)
    y = xf * lax.rsqrt(jnp.mean(xf * xf, axis=-1, keepdims=True) + EPS)
    return y * g.astype(F32)


def _l2norm(t):
    return t * lax.rsqrt(jnp.sum(t * t, axis=-1, keepdims=True) + EPS)


def _rope_tables(n_tok, head_dim):
    pos = jnp.arange(n_tok)
    row = (pos // GRID_W).astype(F32)
    col = (pos % GRID_W).astype(F32)
    n_freq = head_dim // 4
    inv = ROPE_BASE ** (-jnp.arange(n_freq, dtype=F32) / n_freq)
    ang_r = row[:, None] * inv
    ang_c = col[:, None] * inv
    ang = jnp.concatenate([ang_r, ang_r, ang_c, ang_c], axis=-1)
    return jnp.cos(ang), jnp.sin(ang)


def _apply_rope(x, cos, sin):
    extra = x.ndim - 3
    shp = cos.shape[:1] + (1,) * extra + cos.shape[1:]
    cos = cos.reshape(shp)
    sin = sin.reshape(shp)
    a, b, c2, d = jnp.split(x, 4, axis=-1)
    rot = jnp.concatenate([-b, a, -d, c2], axis=-1)
    return x * cos + rot * sin


def _plain_attention(q, k, v):
    s = jnp.einsum('bqhd,bkhd->bhqk', q, k).astype(F32) * (q.shape[-1] ** -0.5)
    p = jax.nn.softmax(s, axis=-1)
    return jnp.einsum('bhqk,bkhd->bqhd', p, v)


def _affine_combine(e1, e2):
    a1r, a1i, b1r, b1i = e1
    a2r, a2i, b2r, b2i = e2
    return (a2r * a1r - a2i * a1i, a2r * a1i + a2i * a1r,
            a2r * b1r - a2i * b1i + b2r, a2r * b1i + a2i * b1r + b2i)


def _s5_discretize(lam_re, lam_im, log_step, b_re, b_im):
    dt = jnp.exp(log_step)[:, None]
    mag = jnp.exp(lam_re * dt)
    ar = mag * jnp.cos(lam_im * dt)
    ai = mag * jnp.sin(lam_im * dt)
    den = lam_re * lam_re + lam_im * lam_im
    fr = ((ar - 1.0) * lam_re + ai * lam_im) / den
    fi = (ai * lam_re - (ar - 1.0) * lam_im) / den
    bbr = fr[..., None] * b_re - fi[..., None] * b_im
    bbi = fr[..., None] * b_im + fi[..., None] * b_re
    return ar, ai, bbr, bbi


def _s5_scan(u, ar, ai, bbr, bbi, h0):
    br = jnp.einsum('blgn,gpn->blgp', u, bbr)
    bi = jnp.einsum('blgn,gpn->blgp', u, bbi)
    if h0 is not None:
        h0r, h0i = h0
        br = br.at[:, 0].add(ar * h0r - ai * h0i)
        bi = bi.at[:, 0].add(ar * h0i + ai * h0r)
    a_r = jnp.broadcast_to(ar, br.shape)
    a_i = jnp.broadcast_to(ai, bi.shape)
    _, _, hr, hi = lax.associative_scan(_affine_combine, (a_r, a_i, br, bi), axis=1)
    return hr, hi


def _s5_readout(hr, hi, c_re, c_im):
    return jnp.einsum('blgp,gnp->blgn', hr, c_re) - jnp.einsum('blgp,gnp->blgn', hi, c_im)


def _s5_mixer(u_c, u_l, lam_re, lam_im, log_step, b_re, b_im, c_re, c_im, d_skip, w_glu, b_glu):
    def grp(u):
        return u.reshape(u.shape[0], u.shape[1], S5_GROUPS, S5_CH)

    gc_in, gl_in = grp(u_c), grp(u_l)
    y_c, y_l = 0.0, 0.0
    for d in (0, 1):
        fl = (lambda t: jnp.flip(t, 1)) if d else (lambda t: t)
        ar, ai, bbr, bbi = _s5_discretize(lam_re[d], lam_im[d], log_step[d], b_re[d], b_im[d])
        hcr, hci = _s5_scan(fl(gc_in), ar, ai, bbr, bbi, None)
        hlr, hli = _s5_scan(fl(gl_in), ar, ai, bbr, bbi, (hcr[:, -1], hci[:, -1]))
        y_l = y_l + fl(_s5_readout(hlr, hli, c_re[d], c_im[d]))
        y_c = y_c + fl(_s5_readout(hcr, hci, c_re[d], c_im[d]))

    def finish(y, u):
        y = y.reshape(u.shape) + d_skip * u
        h = jax.nn.gelu(y) @ w_glu + b_glu
        a, b = jnp.split(h, 2, axis=-1)
        return a * jax.nn.sigmoid(b)

    return finish(y_c, u_c), finish(y_l, u_l)


def _diff_attention(p_c, p_l, cos, sin, q_norm, k_norm, lq1, lk1, lq2, lk2, subln, lam_init):
    def heads(p):
        bsz, n = p.shape[:2]
        q, k, v = jnp.split(p, 3, axis=-1)
        q = _rms_norm(q.reshape(bsz, n, DIFF_HEADS, 2, DIFF_HEAD_DIM), q_norm)
        k = _rms_norm(k.reshape(bsz, n, DIFF_HEADS, 2, DIFF_HEAD_DIM), k_norm)
        v = v.reshape(bsz, n, DIFF_HEADS, 2 * DIFF_HEAD_DIM)
        return q, k, v

    qc, kc, vc = heads(p_c)
    ql, kl, vl = heads(p_l)
    ql = _apply_rope(ql, cos, sin)
    kl = _apply_rope(kl, cos, sin)
    lam = jnp.exp(jnp.sum(lq1 * lk1)) - jnp.exp(jnp.sum(lq2 * lk2)) + lam_init
    scale = DIFF_HEAD_DIM ** -0.5

    def attend(q, k, v):
        s = jnp.einsum('bqhcd,bkhcd->bhcqk', q, k).astype(F32) * scale
        p = jax.nn.softmax(s, axis=-1)
        a = p[:, :, 0] - lam * p[:, :, 1]
        return jnp.einsum('bhqk,bkhe->bqhe', a, v)

    k_all = jnp.concatenate([kc, kl], axis=1)
    v_all = jnp.concatenate([vc, vl], axis=1)
    bsz, n_tok = ql.shape[:2]
    nb = n_tok // Q_BLOCK
    qb = jnp.moveaxis(ql.reshape(bsz, nb, Q_BLOCK, DIFF_HEADS, 2, DIFF_HEAD_DIM), 1, 0)
    ob = lax.map(lambda q: attend(q, k_all, v_all), qb)
    o_l = jnp.moveaxis(ob, 0, 1).reshape(bsz, n_tok, DIFF_HEADS, 2 * DIFF_HEAD_DIM)

    def finish(o):
        return (_rms_norm(o, subln) * (1.0 - lam_init)).reshape(o.shape[0], o.shape[1], GROUP_WIDTH)

    return finish(attend(qc, kc, vc)), finish(o_l)


def _short_conv(x, w):
    width = w.shape[0]
    return lax.conv_general_dilated(
        x, w[:, None, :], window_strides=(1,), padding=[(width // 2, width // 2)],
        dimension_numbers=('NWC', 'WIO', 'NWC'), feature_group_count=x.shape[-1])


def _gated_delta_chunked(q, k, v, g, beta, s0):
    bsz, n_tok, nh, dk = k.shape
    dv = v.shape[-1]
    n = n_tok // GDN_CHUNK

    def chunks(t):
        t = t.reshape((bsz, n, GDN_CHUNK, nh) + t.shape[3:])
        return jnp.moveaxis(t, (1, 3), (0, 2))

    k_c, v_c, g_c, b_c = chunks(k), chunks(v), chunks(g), chunks(beta)
    g_cum = jnp.cumsum(g_c, axis=-1)
    idx = jnp.arange(GDN_CHUNK)
    incl = idx[:, None] >= idx[None, :]
    strict = idx[:, None] > idx[None, :]
    diff = g_cum[..., :, None] - g_cum[..., None, :]
    decay = jnp.where(incl, jnp.exp(jnp.where(incl, diff, 0.0)), 0.0)
    k_beta = k_c * b_c[..., None]
    a_mat = jnp.where(strict, jnp.einsum('nbhid,nbhjd->nbhij', k_beta, k_c) * decay, 0.0)
    eye = jnp.eye(GDN_CHUNK, dtype=F32)
    t_mat = lax.linalg.triangular_solve(eye + a_mat, jnp.broadcast_to(eye, a_mat.shape),
                                        left_side=True, lower=True)
    w = t_mat @ (k_beta * jnp.exp(g_cum)[..., None])
    u = t_mat @ (v_c * b_c[..., None])
    g_last = g_cum[..., -1]
    k_carry = k_c * jnp.exp(g_last[..., None] - g_cum)[..., None]
    q_c = chunks(q)
    attn = jnp.where(incl, jnp.einsum('nbhid,nbhjd->nbhij', q_c, k_c) * decay, 0.0)
    q_dec = q_c * jnp.exp(g_cum)[..., None]

    def step(S, inp):
        k_i, u_i, w_i, gl, qd_i, at_i = inp
        v_new = u_i - w_i @ S
        S_new = S * jnp.exp(gl)[..., None, None] + jnp.einsum('bhcd,bhce->bhde', k_i, v_new)
        o = qd_i @ S + at_i @ v_new
        return S_new, o

    S, o = lax.scan(step, s0, (k_carry, u, w, g_last, q_dec, attn))
    o = jnp.moveaxis(o, (0, 2), (1, 3)).reshape(bsz, n_tok, nh, dv)
    return o, S


def _gdn_mixer(p_c, p_l, conv_w, a_log, dt_bias, norm_w):
    W, H, dh = GROUP_WIDTH, GDN_HEADS, GDN_HEAD_DIM

    def prep(p):
        bsz, n = p.shape[:2]
        qkv, z, a, b = jnp.split(p, [3 * W, 4 * W, 4 * W + 2 * H], axis=-1)
        qkv = jax.nn.silu(_short_conv(qkv, conv_w))
        q, k, v = [t.reshape(bsz, n, H, dh) for t in jnp.split(qkv, 3, axis=-1)]
        q = _l2norm(q) * (dh ** -0.5)
        k = _l2norm(k)
        a = a.reshape(bsz, n, 2, H)
        b = b.reshape(bsz, n, 2, H)
        g = -jnp.exp(a_log) * jax.nn.softplus(a + dt_bias)
        return q, k, v, g, jax.nn.sigmoid(b), z

    qc, kc, vc, gc, bc, zc = prep(p_c)
    ql, kl, vl, gl, bl, zl = prep(p_l)
    bsz = p_l.shape[0]
    s0 = jnp.zeros((bsz, H, dh, dh), F32)
    o_c, o_l = 0.0, 0.0
    for d in (0, 1):
        fl = (lambda t: jnp.flip(t, 1)) if d else (lambda t: t)
        oc_d, s_ctx = _gated_delta_chunked(fl(qc), fl(kc), fl(vc), fl(gc[:, :, d]), fl(bc[:, :, d]), s0)
        ol_d, _ = _gated_delta_chunked(fl(ql), fl(kl), fl(vl), fl(gl[:, :, d]), fl(bl[:, :, d]), s_ctx)
        o_l = o_l + fl(ol_d)
        o_c = o_c + fl(oc_d)

    def finish(o, z):
        bz, n = z.shape[:2]
        y = _rms_norm(o, norm_w) * jax.nn.silu(z.reshape(bz, n, H, dh))
        return y.reshape(bz, n, W)

    return finish(o_c, zc), finish(o_l, zl)


def _neighborhood_attention(p_c, p_l, q_norm, k_norm, rpb):
    def heads(p):
        bsz, n = p.shape[:2]
        q, k, v = [t.reshape(bsz, n, NA_HEADS, NA_HEAD_DIM) for t in jnp.split(p, 3, axis=-1)]
        return _rms_norm(q, q_norm), _rms_norm(k, k_norm), v

    qc, kc, vc = heads(p_c)
    ql, kl, vl = heads(p_l)
    bsz, n_tok = p_l.shape[:2]
    rows = n_tok // GRID_W
    kh = min(WIN_H, rows)
    n_cb = GRID_W // WIN_W
    scale = NA_HEAD_DIM ** -0.5
    r = jnp.arange(rows)
    row0 = jnp.clip(r - kh // 2, 0, rows - kh)
    key_rows = row0[:, None] + jnp.arange(kh)
    q_cols = (jnp.arange(n_cb) * WIN_W)[:, None] + jnp.arange(WIN_W)
    col0 = jnp.clip(q_cols - WIN_W // 2, 0, GRID_W - WIN_W)
    kcol0 = jnp.clip(jnp.arange(n_cb) * WIN_W - WIN_W // 2, 0, GRID_W - NA_KEY_COLS)
    key_cols = kcol0[:, None] + jnp.arange(NA_KEY_COLS)
    nk = kh * NA_KEY_COLS
    key_idx = (key_rows[:, None, :, None] * GRID_W + key_cols[None, :, None, :]).reshape(rows, n_cb, nk)
    kg = kl[:, key_idx]
    vg = vl[:, key_idx]
    qb = ql.reshape(bsz, rows, n_cb, WIN_W, NA_HEADS, NA_HEAD_DIM)
    s_loc = jnp.einsum('brjqhd,brjkhd->bhrjqk', qb, kg).astype(F32) * scale
    dcol = key_cols[:, None, :] - q_cols[:, :, None]
    col_ok = (key_cols[:, None, :] >= col0[:, :, None]) & (key_cols[:, None, :] < col0[:, :, None] + WIN_W)
    drow = key_rows - r[:, None]
    ri = (drow + WIN_H - 1)[:, None, None, :, None]
    ci = jnp.clip(dcol + WIN_W - 1, 0, 2 * WIN_W - 2)[None, :, :, None, :]
    bias = rpb[:, ri, ci].reshape(NA_HEADS, rows, n_cb, WIN_W, nk)
    mask = jnp.broadcast_to(col_ok[:, :, None, :], (n_cb, WIN_W, kh, NA_KEY_COLS)).reshape(n_cb, WIN_W, nk)
    s_loc = jnp.where(mask, s_loc + bias[None], -jnp.inf)
    s_ctx = jnp.einsum('brjqhd,bkhd->bhrjqk', qb, kc).astype(F32) * scale
    p = jax.nn.softmax(jnp.concatenate([s_loc, s_ctx], axis=-1), axis=-1)
    o = (jnp.einsum('bhrjqk,brjkhd->brjqhd', p[..., :nk], vg)
         + jnp.einsum('bhrjqk,bkhd->brjqhd', p[..., nk:], vc))
    o_l = o.reshape(bsz, n_tok, GROUP_WIDTH)
    out_c = _plain_attention(qc, kc, vc).reshape(p_c.shape[0], p_c.shape[1], GROUP_WIDTH)
    return out_c, o_l


def _reorder_w_in(w):
    s5 = w[:, 0:512]
    diff = w[:, 512:2048]
    gdn_qkv = w[:, 2048:3584]
    gdn_z = w[:, 3584:4096]
    gdn_ab = w[:, 4096:4112]
    na = w[:, 4112:5648]
    pad = jnp.zeros((w.shape[0], P_WIDTH - P_GDN_AB - 16), w.dtype)
    return jnp.concatenate([s5, diff, gdn_qkv, gdn_z, na, gdn_ab, pad], axis=1)


def kernel(x, c, ctx, c_ctx, w_ada, b_ada, norm_ffn1, norm_mix, norm_ffn2, ffn1_w_in, ffn1_w_out,
           ffn2_w_in, ffn2_w_out, w_in, w_out, s5_lambda_re, s5_lambda_im, s5_log_step, s5_b_re,
           s5_b_im, s5_c_re, s5_c_im, s5_d, s5_w_glu, s5_b_glu, diff_q_norm, diff_k_norm,
           diff_lambda_q1, diff_lambda_k1, diff_lambda_q2, diff_lambda_k2, diff_subln, gdn_conv,
           gdn_a_log, gdn_dt_bias, gdn_norm, na_q_norm, na_k_norm, na_rpb):
    cos, sin = _rope_tables(SEQ, DIFF_HEAD_DIM)
    mod = _modulation(c, c_ctx, w_ada, b_ada)
    s = jnp.concatenate([ctx[0], x[0]], axis=0)
    for l in range(DEPTH):
        mod_l = mod[l]
        s = _ffn(s, mod_l, norm_ffn1[l], ffn1_w_in[l].astype(BF16), ffn1_w_out[l].astype(BF16), 0)
        p = _inproj(s, mod_l, norm_mix[l], _reorder_w_in(w_in[l]).astype(BF16))
        p_c, p_l = p[None, :CTX_LEN], p[None, CTX_LEN:]
        lam_init = 0.8 - 0.6 * math.exp(-0.3 * l)
        ya_c, ya_l = _s5_mixer(p_c[..., P_S5:P_DIFF], p_l[..., P_S5:P_DIFF], s5_lambda_re[l],
                               s5_lambda_im[l], s5_log_step[l], s5_b_re[l], s5_b_im[l], s5_c_re[l],
                               s5_c_im[l], s5_d[l], s5_w_glu[l], s5_b_glu[l])
        yb_c, yb_l = _diff_attention(p_c[..., P_DIFF:P_GDN_QKV], p_l[..., P_DIFF:P_GDN_QKV], cos, sin,
                                     diff_q_norm[l], diff_k_norm[l], diff_lambda_q1[l],
                                     diff_lambda_k1[l], diff_lambda_q2[l], diff_lambda_k2[l],
                                     diff_subln[l], lam_init)
        gdn_in = lambda t: jnp.concatenate([t[..., P_GDN_QKV:P_NA], t[..., P_GDN_AB:P_GDN_AB + 16]], -1)
        yc_c, yc_l = _gdn_mixer(gdn_in(p_c), gdn_in(p_l), gdn_conv[l], gdn_a_log[l], gdn_dt_bias[l],
                                gdn_norm[l])
        yd_c, yd_l = _neighborhood_attention(p_c[..., P_NA:P_GDN_AB], p_l[..., P_NA:P_GDN_AB],
                                             na_q_norm[l], na_k_norm[l], na_rpb[l])
        ys = [jnp.concatenate([yc_[0], yl_[0]], axis=0)
              for yc_, yl_ in ((ya_c, ya_l), (yb_c, yb_l), (yc_c, yc_l), (yd_c, yd_l))]
        s = _outproj(s, mod_l, ys, w_out[l].astype(BF16))
        s = _ffn(s, mod_l, norm_ffn2[l], ffn2_w_in[l].astype(BF16), ffn2_w_out[l].astype(BF16), 6)
    return s[None, CTX_LEN:]
```

```python
import functools
import math

import jax
import jax.numpy as jnp
from jax import lax
from jax.experimental import pallas as pl
from jax.experimental.pallas import tpu as pltpu

D_MODEL = 2048
SEQ = 8192
DEPTH = 4
GRID_W = 64
CTX_LEN = 256
ROWS = CTX_LEN + SEQ
GROUP_WIDTH = 512
D_FF = 5632
N_MOD = 9
EPS = 1e-6

S5_CH = 16
S5_GROUPS = GROUP_WIDTH // S5_CH
S5_STATE = 64
DIFF_HEADS = 4
DIFF_HEAD_DIM = 64
ROPE_BASE = 10000.0
Q_BLOCK = 128
GDN_HEADS = 4
GDN_HEAD_DIM = 128
GDN_CONV = 5
GDN_CHUNK = 64
NA_HEADS = 8
NA_HEAD_DIM = 64
WIN_H = 8
WIN_W = 16
NA_KEY_COLS = 2 * WIN_W

P_S5 = 0
P_DIFF = 512
P_GDN_QKV = 2048
P_GDN_Z = 3584
P_NA = 4096
P_GDN_AB = 5632
P_WIDTH = 5760

VMEM_LIMIT_BYTES = 56 * 1024 * 1024

F32 = jnp.float32
BF16 = jnp.bfloat16


MOD_TN = 1024


def _mod_kernel(ct_ref, w_ref, b_ref, o_ref):
    c = ct_ref[...]
    s = c * jax.nn.sigmoid(c)
    v0 = jnp.broadcast_to(s[:, 0:1], (D_MODEL, 128))
    v1 = jnp.broadcast_to(s[:, 1:2], (D_MODEL, 128))
    for j in range(MOD_TN // 128):
        w = w_ref[:, j * 128:(j + 1) * 128]
        b = b_ref[:, j * 128:(j + 1) * 128]
        o_ref[0:1, j * 128:(j + 1) * 128] = jnp.sum(w * v0, axis=0, keepdims=True) + b
        o_ref[1:2, j * 128:(j + 1) * 128] = jnp.sum(w * v1, axis=0, keepdims=True) + b


def _modulation(c, c_ctx, w_ada, b_ada):
    n = N_MOD * D_MODEL
    ct = jnp.stack([c.reshape(D_MODEL), c_ctx.reshape(D_MODEL)], axis=1)
    out = pl.pallas_call(
        _mod_kernel,
        grid=(DEPTH, n // MOD_TN),
        in_specs=[
            pl.BlockSpec((D_MODEL, 2), lambda l, j: (0, 0)),
            pl.BlockSpec((None, D_MODEL, MOD_TN), lambda l, j: (l, 0, j)),
            pl.BlockSpec((None, 1, MOD_TN), lambda l, j: (l, 0, j)),
        ],
        out_specs=pl.BlockSpec((None, 2, MOD_TN), lambda l, j: (l, 0, j)),
        out_shape=jax.ShapeDtypeStruct((DEPTH, 2, n), F32),
        compiler_params=pltpu.CompilerParams(
            dimension_semantics=("arbitrary", "arbitrary"), vmem_limit_bytes=VMEM_LIMIT_BYTES),
        name="adaln_mod",
    )(ct, w_ada, b_ada.reshape(DEPTH, 1, n))
    return out.reshape(DEPTH, 2 * N_MOD, D_MODEL)


def _is_ctx_rows(tile_rows):
    rows = pl.program_id(0) * tile_rows + lax.broadcasted_iota(jnp.int32, (tile_rows, 1), 0)
    return rows < CTX_LEN


def _mod_row(mod_ref, is_ctx, k):
    return jnp.where(is_ctx, mod_ref[N_MOD + k:N_MOD + k + 1, :], mod_ref[k:k + 1, :])


def _adaln_rows(x, gamma, mod_ref, is_ctx, base):
    ms = jnp.mean(x * x, axis=-1, keepdims=True)
    y = x * lax.rsqrt(ms + EPS) * gamma
    return y * (1.0 + _mod_row(mod_ref, is_ctx, base + 1)) + _mod_row(mod_ref, is_ctx, base)


NORM_ROWS = 16


def _adaln_to_scratch(x_ref, gamma_ref, mod_ref, h_ref, tile_rows, base):
    row0 = pl.program_id(0) * tile_rows

    def body(r, carry):
        lo = pl.multiple_of(r * NORM_ROWS, NORM_ROWS)
        rows = row0 + lo + lax.broadcasted_iota(jnp.int32, (NORM_ROWS, 1), 0)
        h = _adaln_rows(x_ref[pl.ds(lo, NORM_ROWS), :], gamma_ref[...], mod_ref, rows < CTX_LEN, base)
        h_ref[pl.ds(lo, NORM_ROWS), :] = h.astype(BF16)
        return carry

    lax.fori_loop(0, tile_rows // NORM_ROWS, body, 0)


FFN_TM = 1408
FFN_RC = 352
FFN_TF = 512


def _ffn_kernel(x_ref, mod_ref, gamma_ref, wg_ref, wu_ref, wo_ref, o_ref, h_ref, *, base):
    f = pl.program_id(1)
    last = pl.num_programs(1) - 1
    row0 = pl.program_id(0) * FFN_TM

    @pl.when(f == 0)
    def _():
        _adaln_to_scratch(x_ref, gamma_ref, mod_ref, h_ref, FFN_TM, base)

    def chunk(r, carry):
        lo = pl.multiple_of(r * FFN_RC, FFN_RC)
        h = h_ref[pl.ds(lo, FFN_RC), :]
        g = jnp.dot(h, wg_ref[...], preferred_element_type=F32)
        u = jnp.dot(h, wu_ref[...], preferred_element_type=F32)
        a = (g * jax.nn.sigmoid(g) * u).astype(BF16)
        d = jnp.dot(a, wo_ref[...], preferred_element_type=F32)

        @pl.when(f == 0)
        def _():
            o_ref[pl.ds(lo, FFN_RC), :] = d

        @pl.when(jnp.logical_and(f > 0, f < last))
        def _():
            o_ref[pl.ds(lo, FFN_RC), :] += d

        @pl.when(f == last)
        def _():
            rows = row0 + lo + lax.broadcasted_iota(jnp.int32, (FFN_RC, 1), 0)
            gate = _mod_row(mod_ref, rows < CTX_LEN, base + 2)
            o_ref[pl.ds(lo, FFN_RC), :] = (
                x_ref[pl.ds(lo, FFN_RC), :] + 0.5 * gate * (o_ref[pl.ds(lo, FFN_RC), :] + d))
        return carry

    lax.fori_loop(0, FFN_TM // FFN_RC, chunk, 0)


def _ffn(s, mod_l, gamma, w_in_bf, w_out_bf, base):
    nf = D_FF // FFN_TF
    return pl.pallas_call(
        functools.partial(_ffn_kernel, base=base),
        grid=(ROWS // FFN_TM, nf),
        in_specs=[
            pl.BlockSpec((FFN_TM, D_MODEL), lambda i, f: (i, 0), pipeline_mode=pl.Buffered(1)),
            pl.BlockSpec((2 * N_MOD, D_MODEL), lambda i, f: (0, 0)),
            pl.BlockSpec((1, D_MODEL), lambda i, f: (0, 0)),
            pl.BlockSpec((D_MODEL, FFN_TF), lambda i, f: (0, f)),
            pl.BlockSpec((D_MODEL, FFN_TF), lambda i, f: (0, nf + f)),
            pl.BlockSpec((FFN_TF, D_MODEL), lambda i, f: (f, 0)),
        ],
        out_specs=pl.BlockSpec((FFN_TM, D_MODEL), lambda i, f: (i, 0), pipeline_mode=pl.Buffered(1)),
        out_shape=jax.ShapeDtypeStruct((ROWS, D_MODEL), F32),
        scratch_shapes=[pltpu.VMEM((FFN_TM, D_MODEL), BF16)],
        compiler_params=pltpu.CompilerParams(
            dimension_semantics=("arbitrary", "arbitrary"), vmem_limit_bytes=VMEM_LIMIT_BYTES),
        name="ffn_swiglu",
    )(s, mod_l, gamma.reshape(1, D_MODEL), w_in_bf, w_in_bf, w_out_bf)


INP_TM = 1408
INP_RC = 352
INP_TN = 640


def _inproj_kernel(x_ref, mod_ref, gamma_ref, w_ref, o_ref, h_ref):
    @pl.when(pl.program_id(1) == 0)
    def _():
        _adaln_to_scratch(x_ref, gamma_ref, mod_ref, h_ref, INP_TM, 3)

    def chunk(r, carry):
        lo = pl.multiple_of(r * INP_RC, INP_RC)
        o_ref[pl.ds(lo, INP_RC), :] = jnp.dot(h_ref[pl.ds(lo, INP_RC), :], w_ref[...],
                                              preferred_element_type=F32)
        return carry

    lax.fori_loop(0, INP_TM // INP_RC, chunk, 0)


def _inproj(s, mod_l, gamma, w_bf):
    return pl.pallas_call(
        _inproj_kernel,
        grid=(ROWS // INP_TM, P_WIDTH // INP_TN),
        in_specs=[
            pl.BlockSpec((INP_TM, D_MODEL), lambda i, n: (i, 0), pipeline_mode=pl.Buffered(1)),
            pl.BlockSpec((2 * N_MOD, D_MODEL), lambda i, n: (0, 0)),
            pl.BlockSpec((1, D_MODEL), lambda i, n: (0, 0)),
            pl.BlockSpec((D_MODEL, INP_TN), lambda i, n: (0, n)),
        ],
        out_specs=pl.BlockSpec((INP_TM, INP_TN), lambda i, n: (i, n)),
        out_shape=jax.ShapeDtypeStruct((ROWS, P_WIDTH), F32),
        scratch_shapes=[pltpu.VMEM((INP_TM, D_MODEL), BF16)],
        compiler_params=pltpu.CompilerParams(
            dimension_semantics=("arbitrary", "arbitrary"), vmem_limit_bytes=VMEM_LIMIT_BYTES),
        name="in_proj",
    )(s, mod_l, gamma.reshape(1, D_MODEL), w_bf)


OUT_TM = 384


def _outproj_kernel(x_ref, mod_ref, ya_ref, yb_ref, yc_ref, yd_ref, w_ref, o_ref):
    is_ctx = _is_ctx_rows(OUT_TM)
    acc = jnp.zeros((OUT_TM, D_MODEL), F32)
    for k, y_ref in enumerate((ya_ref, yb_ref, yc_ref, yd_ref)):
        acc += jnp.dot(y_ref[...].astype(BF16), w_ref[k * GROUP_WIDTH:(k + 1) * GROUP_WIDTH, :],
                       preferred_element_type=F32)
    o_ref[...] = x_ref[...] + _mod_row(mod_ref, is_ctx, 5) * acc


def _outproj(s, mod_l, ys, w_bf):
    yspec = pl.BlockSpec((OUT_TM, GROUP_WIDTH), lambda i: (i, 0))
    return pl.pallas_call(
        _outproj_kernel,
        grid=(ROWS // OUT_TM,),
        in_specs=[
            pl.BlockSpec((OUT_TM, D_MODEL), lambda i: (i, 0)),
            pl.BlockSpec((2 * N_MOD, D_MODEL), lambda i: (0, 0)),
            yspec, yspec, yspec, yspec,
            pl.BlockSpec((D_MODEL, D_MODEL), lambda i: (0, 0)),
        ],
        out_specs=pl.BlockSpec((OUT_TM, D_MODEL), lambda i: (i, 0)),
        out_shape=jax.ShapeDtypeStruct((ROWS, D_MODEL), F32),
        compiler_params=pltpu.CompilerParams(
            dimension_semantics=("arbitrary",), vmem_limit_bytes=VMEM_LIMIT_BYTES),
        name="out_proj",
    )(s, mod_l, *ys, w_bf)


S5_TC = 256
S5_MM_ROWS = 512


def _s5_prepare(lam_re, lam_im, log_step, b_re, b_im, c_re, c_im):
    dt = jnp.exp(log_step)[..., None]
    mag = jnp.exp(lam_re * dt)
    ar = mag * jnp.cos(lam_im * dt)
    ai = mag * jnp.sin(lam_im * dt)
    den = lam_re * lam_re + lam_im * lam_im
    fr = ((ar - 1.0) * lam_re + ai * lam_im) / den
    fi = (ai * lam_re - (ar - 1.0) * lam_im) / den
    bbr = fr[..., None] * b_re - fi[..., None] * b_im
    bbi = fr[..., None] * b_im + fi[..., None] * b_re
    in_oct = jnp.eye(4, dtype=F32)[jnp.arange(S5_GROUPS) % 4]
    half = 4 * S5_STATE

    def w_in_half(bb):
        return jnp.einsum('dgpn,gj->dgnjp', bb, in_oct).reshape(2, GROUP_WIDTH, half)

    def w_out_half(cc):
        return jnp.einsum('dgcp,gj->djpgc', cc, in_oct).reshape(2, half, GROUP_WIDTH)

    w_in = jnp.concatenate([w_in_half(bbr), w_in_half(bbi)], axis=2)
    w_out = jnp.concatenate([w_out_half(c_re), w_out_half(-c_im)], axis=1)
    return (ar.reshape(2, 8, half), ai.reshape(2, 8, half), w_in.astype(BF16), w_out.astype(BF16))


def _s5_scan_kernel(uf_ref, ub_ref, win_ref, wout_ref, ar_ref, ai_ref, yf_ref, yb_ref,
                    ls_ref, buf_ref, bub_ref, r_ref, h_ref):
    tc = S5_TC
    half = 256
    n_mm = 8 * tc // S5_MM_ROWS

    @pl.when(pl.program_id(0) == 0)
    def _():
        ls_ref[...] = jnp.zeros_like(ls_ref)
        h_ref[...] = jnp.zeros_like(h_ref)

    lo_half = lax.broadcasted_iota(jnp.int32, (tc, 128), 1) < 64

    def project_in(u_ref, d, bu_ref):
        for b in range(4):
            ub = u_ref[:, b * 128:(b + 1) * 128]
            ls_ref[b, pl.ds(2 * b, tc, stride=8), :] = jnp.where(lo_half, ub, 0.0)
            ls_ref[b, pl.ds(2 * b + 1, tc, stride=8), :] = jnp.where(lo_half, 0.0, ub)

        def mm(c, carry):
            lo = pl.multiple_of(c * S5_MM_ROWS, S5_MM_ROWS)
            lhs = jnp.concatenate([ls_ref[b, pl.ds(lo, S5_MM_ROWS), :] for b in range(4)], axis=1)
            bu_ref[pl.ds(lo, S5_MM_ROWS), :] = jnp.dot(lhs.astype(BF16), win_ref[d],
                                                      preferred_element_type=F32)
            return carry

        lax.fori_loop(0, n_mm, mm, 0)

    project_in(uf_ref, 0, buf_ref)
    project_in(ub_ref, 1, bub_ref)

    arf, aif, arb, aib = ar_ref[0], ai_ref[0], ar_ref[1], ai_ref[1]

    def step(t, carry):
        hfr, hfi, hbr, hbi = carry
        of = pl.multiple_of(t * 8, 8)
        ob = pl.multiple_of((tc - 1 - t) * 8, 8)
        bf = buf_ref[pl.ds(of, 8), :]
        bb = bub_ref[pl.ds(ob, 8), :]
        nfr = arf * hfr - aif * hfi + bf[:, :half]
        nfi = arf * hfi + aif * hfr + bf[:, half:]
        nbr = arb * hbr - aib * hbi + bb[:, :half]
        nbi = arb * hbi + aib * hbr + bb[:, half:]
        buf_ref[pl.ds(of, 8), :] = jnp.concatenate([nfr, nfi], axis=1)
        bub_ref[pl.ds(ob, 8), :] = jnp.concatenate([nbr, nbi], axis=1)
        return nfr, nfi, nbr, nbi

    carry = lax.fori_loop(0, tc, step, (h_ref[0], h_ref[1], h_ref[2], h_ref[3]), unroll=8)
    for i in range(4):
        h_ref[i] = carry[i]

    def project_out(bu_ref, d, y_ref):
        def mm(c, carry):
            lo = pl.multiple_of(c * S5_MM_ROWS, S5_MM_ROWS)
            res = jnp.dot(bu_ref[pl.ds(lo, S5_MM_ROWS), :].astype(BF16), wout_ref[d],
                          preferred_element_type=F32)
            for b in range(4):
                r_ref[b, pl.ds(lo, S5_MM_ROWS), :] = res[:, b * 128:(b + 1) * 128]
            return carry

        lax.fori_loop(0, n_mm, mm, 0)
        for b in range(4):
            even = r_ref[b, pl.ds(2 * b, tc, stride=8), :]
            odd = r_ref[b, pl.ds(2 * b + 1, tc, stride=8), :]
            y_ref[:, b * 128:(b + 1) * 128] = jnp.where(lo_half, even, odd)

    project_out(buf_ref, 0, yf_ref)
    project_out(bub_ref, 1, yb_ref)


def _s5_scan_call(p, a_r, a_i, w_in, w_out):
    rows = p.shape[0]
    n_chunks = rows // S5_TC
    bwd = lambda j: jnp.where(j == 0, 0, n_chunks - j)
    full = lambda shape: pl.BlockSpec(shape, lambda j: (0,) * len(shape))
    y = jax.ShapeDtypeStruct((rows, GROUP_WIDTH), F32)
    return pl.pallas_call(
        _s5_scan_kernel,
        grid=(n_chunks,),
        in_specs=[
            pl.BlockSpec((S5_TC, GROUP_WIDTH), lambda j: (j, 0)),
            pl.BlockSpec((S5_TC, GROUP_WIDTH), lambda j: (bwd(j), 0)),
            full((2, GROUP_WIDTH, GROUP_WIDTH)), full((2, GROUP_WIDTH, GROUP_WIDTH)),
            full((2, 8, 256)), full((2, 8, 256)),
        ],
        out_specs=[pl.BlockSpec((S5_TC, GROUP_WIDTH), lambda j: (j, 0)),
                   pl.BlockSpec((S5_TC, GROUP_WIDTH), lambda j: (bwd(j), 0))],
        out_shape=[y, y],
        scratch_shapes=[
            pltpu.VMEM((4, 8 * S5_TC, 128), F32),
            pltpu.VMEM((8 * S5_TC, GROUP_WIDTH), F32),
            pltpu.VMEM((8 * S5_TC, GROUP_WIDTH), F32),
            pltpu.VMEM((4, 8 * S5_TC, 128), F32),
            pltpu.VMEM((4, 8, 256), F32),
        ],
        compiler_params=pltpu.CompilerParams(
            dimension_semantics=("arbitrary",), vmem_limit_bytes=VMEM_LIMIT_BYTES),
        name="s5_scan",
    )(p, p, w_in, w_out, a_r, a_i)


S5_FIN_TM = 384


def _s5_finish_kernel(yf_ref, yb_ref, u_ref, d_ref, w_ref, b_ref, o_ref):
    y = yf_ref[...] + yb_ref[...] + d_ref[...] * u_ref[...]
    h = jnp.dot(jax.nn.gelu(y).astype(BF16), w_ref[...], preferred_element_type=F32) + b_ref[...]
    o_ref[...] = h[:, :GROUP_WIDTH] * jax.nn.sigmoid(h[:, GROUP_WIDTH:])


def _s5_finish(yf, yb, p, d_skip, w_glu_bf, b_glu):
    rows = p.shape[0]
    row_blk = pl.BlockSpec((S5_FIN_TM, GROUP_WIDTH), lambda i: (i, 0))
    return pl.pallas_call(
        _s5_finish_kernel,
        grid=(rows // S5_FIN_TM,),
        in_specs=[row_blk, row_blk, row_blk,
                  pl.BlockSpec((1, GROUP_WIDTH), lambda i: (0, 0)),
                  pl.BlockSpec((GROUP_WIDTH, 2 * GROUP_WIDTH), lambda i: (0, 0)),
                  pl.BlockSpec((1, 2 * GROUP_WIDTH), lambda i: (0, 0))],
        out_specs=row_blk,
        out_shape=jax.ShapeDtypeStruct((rows, GROUP_WIDTH), F32),
        compiler_params=pltpu.CompilerParams(
            dimension_semantics=("arbitrary",), vmem_limit_bytes=VMEM_LIMIT_BYTES),
        name="s5_finish",
    )(yf, yb, p, d_skip.reshape(1, GROUP_WIDTH), w_glu_bf, b_glu.reshape(1, 2 * GROUP_WIDTH))


def _s5_pallas(p, lam_re, lam_im, log_step, b_re, b_im, c_re, c_im, d_skip, w_glu, b_glu):
    a_r, a_i, w_in, w_out = _s5_prepare(lam_re, lam_im, log_step, b_re, b_im, c_re, c_im)
    yf, yb = _s5_scan_call(p, a_r, a_i, w_in, w_out)
    return _s5_finish(yf, yb, p, d_skip, w_glu.astype(BF16), b_glu)


PREP_TM = 256
PREP_RC = 128


def _segment_ones(width, seg):
    i = jnp.arange(width) // seg
    return (i[:, None] == i[None, :]).astype(BF16)


def _seg_sumsq(x, e):
    x2 = x * x
    hi = x2.astype(BF16)
    lo = (x2 - hi.astype(F32)).astype(BF16)
    return jnp.dot(hi, e, preferred_element_type=F32) + jnp.dot(lo, e, preferred_element_type=F32)


NA_QROWS = 4
NA_KROWS = NA_QROWS + WIN_H
NA_NQ = NA_QROWS * GRID_W
NA_NK = NA_KROWS * GRID_W
MASKED = -1e30


def _na_prep_kernel(q_ref, k_ref, v_ref, e_ref, qw_ref, kw_ref, qo_ref, ko_ref, vo_ref):
    def body(c, carry):
        sl = pl.ds(pl.multiple_of(c * PREP_RC, PREP_RC), PREP_RC)
        q = q_ref[sl, :]
        k = k_ref[sl, :]
        qn = q * lax.rsqrt(_seg_sumsq(q, e_ref[...]) * (1.0 / NA_HEAD_DIM) + EPS) * qw_ref[...]
        kn = k * lax.rsqrt(_seg_sumsq(k, e_ref[...]) * (1.0 / NA_HEAD_DIM) + EPS) * kw_ref[...]
        qo_ref[sl, :] = (qn * NA_HEAD_DIM ** -0.5).astype(BF16)
        ko_ref[sl, :] = kn.astype(BF16)
        vo_ref[sl, :] = v_ref[sl, :].astype(BF16)
        return carry

    lax.fori_loop(0, PREP_TM // PREP_RC, body, 0)


def _na_prep(p, q_norm, k_norm):
    rows = p.shape[0]
    assert rows % PREP_TM == 0
    c0 = P_NA // GROUP_WIDTH
    col = lambda j: pl.BlockSpec((PREP_TM, GROUP_WIDTH), lambda i: (i, c0 + j))
    vec = pl.BlockSpec((1, GROUP_WIDTH), lambda i: (0, 0))
    out = jax.ShapeDtypeStruct((rows, GROUP_WIDTH), BF16)
    blk = pl.BlockSpec((PREP_TM, GROUP_WIDTH), lambda i: (i, 0))
    return pl.pallas_call(
        _na_prep_kernel,
        grid=(rows // PREP_TM,),
        in_specs=[col(0), col(1), col(2),
                  pl.BlockSpec((GROUP_WIDTH, GROUP_WIDTH), lambda i: (0, 0)), vec, vec],
        out_specs=[blk, blk, blk],
        out_shape=[out, out, out],
        compiler_params=pltpu.CompilerParams(
            dimension_semantics=("arbitrary",), vmem_limit_bytes=VMEM_LIMIT_BYTES),
        name="na_prep",
    )(p, p, p, _segment_ones(GROUP_WIDTH, NA_HEAD_DIM),
      jnp.tile(q_norm, NA_HEADS).reshape(1, GROUP_WIDTH), jnp.tile(k_norm, NA_HEADS).reshape(1, GROUP_WIDTH))


def _na_bias_table(rpb, n_grid_rows):
    import numpy as np
    c = np.arange(GRID_W)
    col0 = np.clip(c - WIN_W // 2, 0, GRID_W - WIN_W)
    kc = np.arange(GRID_W)
    col_ok = (kc[None, :] >= col0[:, None]) & (kc[None, :] < col0[:, None] + WIN_W)
    dcol = kc[None, :] - c[:, None] + WIN_W - 1
    shift = ((dcol[None] == np.arange(2 * WIN_W - 1)[:, None, None]) & col_ok[None]).astype(np.float32)
    jr = np.arange(NA_QROWS)[:, None]
    kr = np.arange(NA_KROWS)[None, :]
    ri = np.stack([kr - jr + WIN_H - 1, kr - jr + WIN_H - 1 - WIN_H // 2, kr - jr + WIN_H - 1 - WIN_H])
    ok = np.stack([(kr < WIN_H) & (jr >= 0), (kr - jr >= 0) & (kr - jr < WIN_H), (kr >= NA_QROWS) & (jr >= 0)])
    sel = (ok[..., None] & (ri[..., None] == np.arange(2 * WIN_H - 1))).astype(np.float32)
    valid = ok[:, :, None, :, None] & col_ok[None, None, :, None, :]
    neg = np.where(valid, 0.0, MASKED).astype(np.float32).reshape(3, 1, NA_NQ, NA_NK)
    hi = lax.Precision.HIGHEST
    band = jnp.einsum('hrj,jck->hrck', rpb, jnp.asarray(shift), precision=hi)
    tab = jnp.einsum('vjkr,hrcx->vhjckx', jnp.asarray(sel), band, precision=hi)
    return tab.reshape(3, NA_HEADS, NA_NQ, NA_NK) + jnp.asarray(neg)


def _softmax_pv(scores, values):
    m = functools.reduce(jnp.maximum, [jnp.max(s, axis=1, keepdims=True) for s in scores])
    ps = [jnp.exp(s - m) for s in scores]
    denom = functools.reduce(lambda a, b: a + b, [jnp.sum(p, axis=1, keepdims=True) for p in ps])
    acc = functools.reduce(lambda a, b: a + b,
                           [jnp.dot(p.astype(BF16), v, preferred_element_type=F32) for p, v in zip(ps, values)])
    return acc / denom


def _dot_nt(a, b):
    return lax.dot_general(a, b, (((1,), (1,)), ((), ())), preferred_element_type=F32)


def _na_kernel(q_ref, k_ref, v_ref, tab_ref, o_ref, *, n_blocks):
    i = pl.program_id(1)
    lo_half = lax.broadcasted_iota(jnp.int32, (NA_NQ, 128), 1) < NA_HEAD_DIM
    q = q_ref[...]
    zero = jnp.zeros_like(q)
    q_heads = (jnp.where(lo_half, q, zero), jnp.where(lo_half, zero, q))
    kc = k_ref[0:CTX_LEN, :]
    vc = v_ref[0:CTX_LEN, :]

    @pl.when(i == 0)
    def _():
        outs = [_softmax_pv([_dot_nt(qh, kc)], [vc]) for qh in q_heads]
        o_ref[...] = jnp.where(lo_half, outs[0], outs[1])

    @pl.when(i > 0)
    def _():
        ib = i - 1
        kr0 = jnp.clip(NA_QROWS * ib - WIN_H // 2, 0, NA_QROWS * n_blocks - NA_KROWS)
        start = pl.multiple_of(CTX_LEN + kr0 * GRID_W, GRID_W)
        kwin = k_ref[pl.ds(start, NA_NK), :]
        vwin = v_ref[pl.ds(start, NA_NK), :]
        variant = jnp.where(ib == 0, 0, jnp.where(ib == n_blocks - 1, 2, 1))
        outs = []
        for e, qh in enumerate(q_heads):
            s_loc = _dot_nt(qh, kwin) + tab_ref[variant, e]
            outs.append(_softmax_pv([s_loc, _dot_nt(qh, kc)], [vwin, vc]))
        o_ref[...] = jnp.where(lo_half, outs[0], outs[1])


def _na_pallas(p, q_norm, k_norm, rpb):
    rows = p.shape[0]
    n_grid_rows = (rows - CTX_LEN) // GRID_W
    n_blocks = n_grid_rows // NA_QROWS
    q, k, v = _na_prep(p, q_norm, k_norm)
    tab = _na_bias_table(rpb, n_grid_rows).reshape(3, NA_HEADS // 2, 2, NA_NQ, NA_NK)
    kv = pl.BlockSpec((rows, 128), lambda h, i: (0, h))
    return pl.pallas_call(
        functools.partial(_na_kernel, n_blocks=n_blocks),
        grid=(NA_HEADS // 2, n_blocks + 1),
        in_specs=[pl.BlockSpec((NA_NQ, 128), lambda h, i: (i, h)), kv, kv,
                  pl.BlockSpec((3, None, 2, NA_NQ, NA_NK), lambda h, i: (0, h, 0, 0, 0))],
        out_specs=pl.BlockSpec((NA_NQ, 128), lambda h, i: (i, h)),
        out_shape=jax.ShapeDtypeStruct((rows, GROUP_WIDTH), F32),
        compiler_params=pltpu.CompilerParams(
            dimension_semantics=("arbitrary", "arbitrary"), vmem_limit_bytes=VMEM_LIMIT_BYTES),
        name="na_attention",
    )(q, k, v, tab)


DIFF_TQ = 256
DIFF_TK = 768
DIFF_HD = 2 * DIFF_HEAD_DIM


def _rope_tables_rows(rows):
    pos = jnp.arange(rows - CTX_LEN)
    row = (pos // GRID_W).astype(F32)
    col = (pos % GRID_W).astype(F32)
    n_freq = DIFF_HEAD_DIM // 4
    inv = ROPE_BASE ** (-jnp.arange(n_freq, dtype=F32) / n_freq)
    ang = jnp.concatenate([row[:, None] * inv, row[:, None] * inv, col[:, None] * inv, col[:, None] * inv], -1)
    sign = jnp.tile(jnp.repeat(jnp.array([-1.0, 1.0, -1.0, 1.0], F32), n_freq), 2)
    cos = jnp.concatenate([jnp.ones((CTX_LEN, DIFF_HD), F32), jnp.tile(jnp.cos(ang), (1, 2))], axis=0)
    sin = jnp.concatenate([jnp.zeros((CTX_LEN, DIFF_HD), F32), jnp.tile(jnp.sin(ang), (1, 2)) * sign], axis=0)
    return cos, sin


def _diff_prep_kernel(q_ref, k_ref, v_ref, e_ref, qw_ref, kw_ref, cos_ref, sin_ref, qo_ref, ko_ref, vo_ref):
    quarter = lax.broadcasted_iota(jnp.int32, (PREP_RC, GROUP_WIDTH), 1) // (DIFF_HEAD_DIM // 4)
    first_of_pair = quarter % 2 == 0

    def rope(x, cos, sin):
        partner = jnp.where(first_of_pair, pltpu.roll(x, GROUP_WIDTH - DIFF_HEAD_DIM // 4, 1),
                            pltpu.roll(x, DIFF_HEAD_DIM // 4, 1))
        return x * cos + partner * sin

    def body(c, carry):
        sl = pl.ds(pl.multiple_of(c * PREP_RC, PREP_RC), PREP_RC)
        cos = jnp.concatenate([cos_ref[sl, :]] * DIFF_HEADS, axis=1)
        sin = jnp.concatenate([sin_ref[sl, :]] * DIFF_HEADS, axis=1)
        q = q_ref[sl, :]
        k = k_ref[sl, :]
        qn = q * lax.rsqrt(_seg_sumsq(q, e_ref[...]) * (1.0 / DIFF_HEAD_DIM) + EPS) * qw_ref[...]
        kn = k * lax.rsqrt(_seg_sumsq(k, e_ref[...]) * (1.0 / DIFF_HEAD_DIM) + EPS) * kw_ref[...]
        qo_ref[sl, :] = (rope(qn, cos, sin) * DIFF_HEAD_DIM ** -0.5).astype(BF16)
        ko_ref[sl, :] = rope(kn, cos, sin).astype(BF16)
        v = v_ref[sl, :].astype(BF16)
        ones = jnp.ones((PREP_RC, DIFF_HD), BF16)
        for h in range(DIFF_HEADS):
            vo_ref[sl, 2 * h * DIFF_HD:(2 * h + 1) * DIFF_HD] = v[:, h * DIFF_HD:(h + 1) * DIFF_HD]
            vo_ref[sl, (2 * h + 1) * DIFF_HD:(2 * h + 2) * DIFF_HD] = ones
        return carry

    lax.fori_loop(0, PREP_TM // PREP_RC, body, 0)


def _diff_prep(p, q_norm, k_norm):
    rows = p.shape[0]
    assert rows % PREP_TM == 0
    c0 = P_DIFF // GROUP_WIDTH
    col = lambda j: pl.BlockSpec((PREP_TM, GROUP_WIDTH), lambda i: (i, c0 + j))
    vec = pl.BlockSpec((1, GROUP_WIDTH), lambda i: (0, 0))
    tab = pl.BlockSpec((PREP_TM, DIFF_HD), lambda i: (i, 0))
    blk = pl.BlockSpec((PREP_TM, GROUP_WIDTH), lambda i: (i, 0))
    cos, sin = _rope_tables_rows(rows)
    return pl.pallas_call(
        _diff_prep_kernel,
        grid=(rows // PREP_TM,),
        in_specs=[col(0), col(1), col(2),
                  pl.BlockSpec((GROUP_WIDTH, GROUP_WIDTH), lambda i: (0, 0)), vec, vec, tab, tab],
        out_specs=[blk, blk, pl.BlockSpec((PREP_TM, 2 * GROUP_WIDTH), lambda i: (i, 0))],
        out_shape=[jax.ShapeDtypeStruct((rows, GROUP_WIDTH), BF16), jax.ShapeDtypeStruct((rows, GROUP_WIDTH), BF16),
                   jax.ShapeDtypeStruct((rows, 2 * GROUP_WIDTH), BF16)],
        compiler_params=pltpu.CompilerParams(
            dimension_semantics=("arbitrary",), vmem_limit_bytes=VMEM_LIMIT_BYTES),
        name="diff_prep",
    )(p, p, p, _segment_ones(GROUP_WIDTH, DIFF_HEAD_DIM),
      jnp.tile(q_norm, 2 * DIFF_HEADS).reshape(1, GROUP_WIDTH),
      jnp.tile(k_norm, 2 * DIFF_HEADS).reshape(1, GROUP_WIDTH), cos, sin)


def _diff_kernel(q_ref, k_ref, v_ref, lv_ref, sw_ref, o_ref, m_ref, acc_ref, *, n_kchunks):
    i = pl.program_id(1)
    lo_half = lax.broadcasted_iota(jnp.int32, (DIFF_TQ, DIFF_HD), 1) < DIFF_HEAD_DIM
    q = q_ref[...]
    zero = jnp.zeros_like(q)
    q_maps = (jnp.where(lo_half, q, zero), jnp.where(lo_half, zero, q))

    m_ref[...] = jnp.full(m_ref.shape, MASKED, F32)
    acc_ref[...] = jnp.zeros_like(acc_ref)

    def update(kc, vc):
        for e in range(2):
            s = _dot_nt(q_maps[e], kc)
            m_old = m_ref[e]
            m_new = jnp.maximum(m_old, jnp.max(s, axis=1, keepdims=True))
            alpha = jnp.exp(m_old - m_new)
            p = jnp.exp(s - m_new[:, 0:1])
            acc_ref[e] = (jnp.concatenate([alpha, alpha], axis=1) * acc_ref[e]
                          + jnp.dot(p.astype(BF16), vc, preferred_element_type=F32))
            m_ref[e] = m_new

    @pl.when(i == 0)
    def _():
        update(k_ref[0:CTX_LEN, :], v_ref[0:CTX_LEN, :])

    @pl.when(i > 0)
    def _():
        def body(c, carry):
            sl = pl.ds(pl.multiple_of(c * DIFF_TK, DIFF_TK), DIFF_TK)
            update(k_ref[sl, :], v_ref[sl, :])
            return carry

        lax.fori_loop(0, n_kchunks, body, 0)

    lam_init = lv_ref[4:5, 0:1]
    lam = (jnp.exp(jnp.sum(lv_ref[0:1, :] * lv_ref[1:2, :], axis=1, keepdims=True))
           - jnp.exp(jnp.sum(lv_ref[2:3, :] * lv_ref[3:4, :], axis=1, keepdims=True)) + lam_init)
    a1 = acc_ref[0]
    a2 = acc_ref[1]
    o = a1[:, :DIFF_HD] / a1[:, DIFF_HD:] - lam * (a2[:, :DIFF_HD] / a2[:, DIFF_HD:])
    y = o * lax.rsqrt(jnp.mean(o * o, axis=1, keepdims=True) + EPS) * sw_ref[...]
    o_ref[...] = y * (1.0 - lam_init)


def _diff_pallas(p, q_norm, k_norm, lq1, lk1, lq2, lk2, subln, lam_init):
    rows = p.shape[0]
    assert rows % DIFF_TK == 0 and rows % DIFF_TQ == 0 and CTX_LEN == DIFF_TQ
    q, k, v = _diff_prep(p, q_norm, k_norm)
    pad = lambda t: jnp.pad(t, (0, DIFF_HD - DIFF_HEAD_DIM))
    lvec = jnp.stack([pad(lq1), pad(lk1), pad(lq2), pad(lk2), jnp.full((DIFF_HD,), lam_init, F32),
                      jnp.zeros((DIFF_HD,), F32), jnp.zeros((DIFF_HD,), F32), jnp.zeros((DIFF_HD,), F32)])
    return pl.pallas_call(
        functools.partial(_diff_kernel, n_kchunks=rows // DIFF_TK),
        grid=(DIFF_HEADS, rows // DIFF_TQ),
        in_specs=[pl.BlockSpec((DIFF_TQ, DIFF_HD), lambda h, i: (i, h)),
                  pl.BlockSpec((rows, DIFF_HD), lambda h, i: (0, h)),
                  pl.BlockSpec((rows, 2 * DIFF_HD), lambda h, i: (0, h)),
                  pl.BlockSpec((8, DIFF_HD), lambda h, i: (0, 0)),
                  pl.BlockSpec((1, DIFF_HD), lambda h, i: (0, 0))],
        out_specs=pl.BlockSpec((DIFF_TQ, DIFF_HD), lambda h, i: (i, h)),
        out_shape=jax.ShapeDtypeStruct((rows, GROUP_WIDTH), F32),
        scratch_shapes=[pltpu.VMEM((2, DIFF_TQ, DIFF_HD), F32), pltpu.VMEM((2, DIFF_TQ, 2 * DIFF_HD), F32)],
        compiler_params=pltpu.CompilerParams(
            dimension_semantics=("arbitrary", "arbitrary"), vmem_limit_bytes=VMEM_LIMIT_BYTES),
        name="diff_attention",
    )(q, k, v, lvec, subln.reshape(1, DIFF_HD))


GDN_TB = 256
GDN_HALO = 8
GDN_GATES = 2 * GDN_HEADS


def _softplus(x):
    return jnp.maximum(x, 0.0) + jnp.log(1.0 + jnp.exp(-jnp.abs(x)))


def _gdn_prep_kernel(*refs):
    (qp, qc, qn, kp, kc, kn, vp, vc, vn, ab_ref, w_ref, alog_ref, dtb_ref,
     qo_ref, ko_ref, vo_ref, go_ref, pad_ref) = refs
    i = pl.program_id(0)
    last = pl.num_programs(0) - 1
    prev_ok = (i >= 2).astype(F32)
    next_ok = jnp.logical_and(i >= 1, i < last).astype(F32)
    half = GDN_CONV // 2

    def conv_silu(prev_ref, cur_ref, next_ref, sec):
        pad_ref[0:GDN_HALO, :] = prev_ref[...] * prev_ok
        pad_ref[GDN_HALO:GDN_HALO + GDN_TB, :] = cur_ref[...]
        pad_ref[GDN_HALO + GDN_TB:, :] = next_ref[...] * next_ok
        acc = jnp.zeros((GDN_TB, GROUP_WIDTH), F32)
        for j in range(GDN_CONV):
            w = w_ref[j:j + 1, sec * GROUP_WIDTH:(sec + 1) * GROUP_WIDTH]
            acc = acc + pad_ref[GDN_HALO - half + j:GDN_HALO - half + j + GDN_TB, :] * w
        return acc * jax.nn.sigmoid(acc)

    def l2n(x):
        parts = []
        for h in range(GDN_HEADS):
            xh = x[:, h * GDN_HEAD_DIM:(h + 1) * GDN_HEAD_DIM]
            parts.append(xh * lax.rsqrt(jnp.sum(xh * xh, axis=1, keepdims=True) + EPS))
        return jnp.concatenate(parts, axis=1)

    qo_ref[...] = l2n(conv_silu(qp, qc, qn, 0)) * GDN_HEAD_DIM ** -0.5
    ko_ref[...] = l2n(conv_silu(kp, kc, kn, 1))
    vo_ref[...] = conv_silu(vp, vc, vn, 2)
    x = ab_ref[...]
    lane = lax.broadcasted_iota(jnp.int32, x.shape, 1)
    go_ref[...] = jnp.where(lane < GDN_GATES, -jnp.exp(alog_ref[...]) * _softplus(x + dtb_ref[...]),
                            jax.nn.sigmoid(x))


def _gdn_prep(p, conv_w, a_log, dt_bias):
    rows = p.shape[0]
    assert rows % GDN_TB == 0 and CTX_LEN == GDN_TB
    n_halo = rows // GDN_HALO
    per = GDN_TB // GDN_HALO
    c0 = P_GDN_QKV // GROUP_WIDTH
    specs = []
    for sec in range(3):
        specs += [
            pl.BlockSpec((GDN_HALO, GROUP_WIDTH), lambda i, sec=sec: (jnp.maximum(i * per - 1, 0), c0 + sec)),
            pl.BlockSpec((GDN_TB, GROUP_WIDTH), lambda i, sec=sec: (i, c0 + sec)),
            pl.BlockSpec((GDN_HALO, GROUP_WIDTH),
                         lambda i, sec=sec: (jnp.minimum((i + 1) * per, n_halo - 1), c0 + sec)),
        ]
    vec = pl.BlockSpec((1, 128), lambda i: (0, 0))
    specs += [pl.BlockSpec((GDN_TB, 128), lambda i: (i, P_GDN_AB // 128)),
              pl.BlockSpec((8, 3 * GROUP_WIDTH), lambda i: (0, 0)), vec, vec]
    blk = pl.BlockSpec((GDN_TB, GROUP_WIDTH), lambda i: (i, 0))
    out = jax.ShapeDtypeStruct((rows, GROUP_WIDTH), F32)
    pad8 = lambda t: jnp.pad(t.reshape(1, GDN_GATES), ((0, 0), (0, 128 - GDN_GATES)))
    return pl.pallas_call(
        _gdn_prep_kernel,
        grid=(rows // GDN_TB,),
        in_specs=specs,
        out_specs=[blk, blk, blk, pl.BlockSpec((GDN_TB, 128), lambda i: (i, 0))],
        out_shape=[out, out, out, jax.ShapeDtypeStruct((rows, 128), F32)],
        scratch_shapes=[pltpu.VMEM((GDN_TB + 2 * GDN_HALO, GROUP_WIDTH), F32)],
        compiler_params=pltpu.CompilerParams(
            dimension_semantics=("arbitrary",), vmem_limit_bytes=VMEM_LIMIT_BYTES),
        name="gdn_prep",
    )(*([p] * 10), jnp.pad(conv_w, ((0, 8 - GDN_CONV), (0, 0))), pad8(a_log), pad8(dt_bias))


def _mm3(a, b):
    ah = a.astype(BF16)
    al = (a - ah.astype(F32)).astype(BF16)
    bh = b.astype(BF16)
    bl = (b - bh.astype(F32)).astype(BF16)
    dot = lambda x, y: jnp.dot(x, y, preferred_element_type=F32)
    return dot(ah, bh) + dot(ah, bl) + dot(al, bh)


def _gdn_kernel(q_ref, k_ref, v_ref, g_ref, o_ref, s_ref, *, reverse):
    j = pl.program_id(0)
    tb, ck, hd = GDN_TB, GDN_CHUNK, GDN_HEAD_DIM
    d = 1 if reverse else 0

    @pl.when(j == 0)
    def _():
        s_ref[...] = jnp.zeros_like(s_ref)

    ri = lax.broadcasted_iota(jnp.int32, (tb, tb), 0)
    ci = lax.broadcasted_iota(jnp.int32, (tb, tb), 1)
    same = lambda n: (ri // n) == (ci // n)
    same_chunk = same(ck)
    before = (ci > ri) if reverse else (ci < ri)
    strict = jnp.logical_and(same_chunk, before)
    incl = jnp.logical_and(same_chunk, jnp.logical_or(before, ri == ci))
    eye = (ri == ci).astype(F32)

    g = g_ref[...]
    g1 = g.astype(BF16)
    r1 = g - g1.astype(F32)
    g2 = r1.astype(BF16)
    g3 = (r1 - g2.astype(F32)).astype(BF16)
    dot = lambda x, y: jnp.dot(x, y, preferred_element_type=F32)
    cum_m = incl.astype(BF16)
    tot_m = same_chunk.astype(BF16)
    gcum = dot(cum_m, g1) + dot(cum_m, g2) + dot(cum_m, g3)
    gtot = dot(tot_m, g1) + dot(tot_m, g2) + dot(tot_m, g3)
    gcum_t = gcum.T

    order = range(tb // ck - 1, -1, -1) if reverse else range(tb // ck)
    for h in range(GDN_HEADS):
        lane = d * GDN_HEADS + h
        gcol = gcum[:, lane:lane + 1]
        grow = gcum_t[lane:lane + 1, :]
        beta = g[:, GDN_GATES + lane:GDN_GATES + lane + 1]
        qh = q_ref[:, h * hd:(h + 1) * hd]
        kh = k_ref[:, h * hd:(h + 1) * hd]
        vh = v_ref[:, h * hd:(h + 1) * hd]
        decay = jnp.where(incl, jnp.exp(jnp.where(incl, gcol - grow, 0.0)), 0.0)
        kb = kh * beta
        kh_b = kh.astype(BF16)
        a_mat = jnp.where(strict, _dot_nt(kb.astype(BF16), kh_b) * decay, 0.0)
        attn = _dot_nt(qh.astype(BF16), kh_b) * decay

        s16, s32 = same(16), same(32)
        n = -jnp.where(s16, a_mat, 0.0)
        p1 = _mm3(n, n)
        p2 = _mm3(p1, p1)
        p3 = _mm3(p2, p2)
        x = eye + n
        x = x + _mm3(x, p1)
        x = x + _mm3(x, p2)
        x = x + _mm3(x, p3)
        l1 = jnp.where(jnp.logical_and(s32, jnp.logical_not(s16)), a_mat, 0.0)
        x = x - _mm3(_mm3(x, l1), x)
        l2 = jnp.where(jnp.logical_and(same_chunk, jnp.logical_not(s32)), a_mat, 0.0)
        t_mat = x - _mm3(_mm3(x, l2), x)

        eg = jnp.exp(gcol)
        rhs = jnp.concatenate([kb * eg, vh * beta], axis=1).astype(BF16)
        wu = dot(t_mat.astype(BF16), rhs)
        w_all = wu[:, :hd]
        u_all = wu[:, hd:]
        glast = gtot[:, lane:lane + 1]
        k_carry = (kh * jnp.exp(glast - gcol)).astype(BF16)
        wq = jnp.concatenate([w_all, qh * eg], axis=1)

        s = s_ref[h]
        v_new = [None] * (tb // ck)
        o_inter = [None] * (tb // ck)
        for c in order:
            rows = slice(c * ck, (c + 1) * ck)
            s_b = s.astype(BF16)
            v_new[c] = u_all[rows] - dot(wq[rows, :hd].astype(BF16), s_b)
            o_inter[c] = dot(wq[rows, hd:].astype(BF16), s_b)
            s = s * jnp.exp(glast[c * ck:c * ck + 1, :]) + lax.dot_general(
                k_carry[rows], v_new[c].astype(BF16), (((0,), (0,)), ((), ())), preferred_element_type=F32)
        s_ref[h] = s
        v_all = jnp.concatenate(v_new, axis=0)
        o_ref[:, h * hd:(h + 1) * hd] = jnp.concatenate(o_inter, axis=0) + dot(attn.astype(BF16),
                                                                           v_all.astype(BF16))


def _gdn_scan(q, k, v, gates, reverse):
    rows = q.shape[0]
    n_blocks = rows // GDN_TB
    blk_of = (lambda j: jnp.where(j == 0, 0, n_blocks - j)) if reverse else (lambda j: j)
    blk = pl.BlockSpec((GDN_TB, GROUP_WIDTH), lambda j: (blk_of(j), 0))
    return pl.pallas_call(
        functools.partial(_gdn_kernel, reverse=reverse),
        grid=(n_blocks,),
        in_specs=[blk, blk, blk, pl.BlockSpec((GDN_TB, 128), lambda j: (blk_of(j), 0))],
        out_specs=blk,
        out_shape=jax.ShapeDtypeStruct((rows, GROUP_WIDTH), F32),
        scratch_shapes=[pltpu.VMEM((GDN_HEADS, GDN_HEAD_DIM, GDN_HEAD_DIM), F32)],
        compiler_params=pltpu.CompilerParams(
            dimension_semantics=("arbitrary",), vmem_limit_bytes=VMEM_LIMIT_BYTES),
        name="gdn_scan_bwd" if reverse else "gdn_scan_fwd",
    )(q, k, v, gates)


def _gdn_finish_kernel(of_ref, ob_ref, z_ref, w_ref, y_ref):
    o = of_ref[...] + ob_ref[...]
    z = z_ref[...]
    parts = []
    for h in range(GDN_HEADS):
        oh = o[:, h * GDN_HEAD_DIM:(h + 1) * GDN_HEAD_DIM]
        parts.append(oh * lax.rsqrt(jnp.mean(oh * oh, axis=1, keepdims=True) + EPS) * w_ref[...])
    y_ref[...] = jnp.concatenate(parts, axis=1) * (z * jax.nn.sigmoid(z))


def _gdn_pallas(p, conv_w, a_log, dt_bias, norm_w):
    rows = p.shape[0]
    q, k, v, gates = _gdn_prep(p, conv_w, a_log, dt_bias)
    o_f = _gdn_scan(q, k, v, gates, False)
    o_b = _gdn_scan(q, k, v, gates, True)
    blk = pl.BlockSpec((GDN_TB, GROUP_WIDTH), lambda i: (i, 0))
    return pl.pallas_call(
        _gdn_finish_kernel,
        grid=(rows // GDN_TB,),
        in_specs=[blk, blk, pl.BlockSpec((GDN_TB, GROUP_WIDTH), lambda i: (i, P_GDN_Z // GROUP_WIDTH)),
                  pl.BlockSpec((1, GDN_HEAD_DIM), lambda i: (0, 0))],
        out_specs=blk,
        out_shape=jax.ShapeDtypeStruct((rows, GROUP_WIDTH), F32),
        compiler_params=pltpu.CompilerParams(
            dimension_semantics=("arbitrary",), vmem_limit_bytes=VMEM_LIMIT_BYTES),
        name="gdn_finish",
    )(o_f, o_b, p, norm_w.reshape(1, GDN_HEAD_DIM))


def _reorder_w_in(w):
    s5 = w[:, 0:512]
    diff = w[:, 512:2048]
    gdn_qkv = w[:, 2048:3584]
    gdn_z = w[:, 3584:4096]
    gdn_ab = w[:, 4096:4112]
    na = w[:, 4112:5648]
    pad = jnp.zeros((w.shape[0], P_WIDTH - P_GDN_AB - 16), w.dtype)
    return jnp.concatenate([s5, diff, gdn_qkv, gdn_z, na, gdn_ab, pad], axis=1)


def kernel(x, c, ctx, c_ctx, w_ada, b_ada, norm_ffn1, norm_mix, norm_ffn2, ffn1_w_in, ffn1_w_out,
           ffn2_w_in, ffn2_w_out, w_in, w_out, s5_lambda_re, s5_lambda_im, s5_log_step, s5_b_re,
           s5_b_im, s5_c_re, s5_c_im, s5_d, s5_w_glu, s5_b_glu, diff_q_norm, diff_k_norm,
           diff_lambda_q1, diff_lambda_k1, diff_lambda_q2, diff_lambda_k2, diff_subln, gdn_conv,
           gdn_a_log, gdn_dt_bias, gdn_norm, na_q_norm, na_k_norm, na_rpb):
    mod = _modulation(c, c_ctx, w_ada, b_ada)
    s = jnp.concatenate([ctx[0], x[0]], axis=0)
    for l in range(DEPTH):
        mod_l = mod[l]
        s = _ffn(s, mod_l, norm_ffn1[l], ffn1_w_in[l].astype(BF16), ffn1_w_out[l].astype(BF16), 0)
        p = _inproj(s, mod_l, norm_mix[l], _reorder_w_in(w_in[l]).astype(BF16))
        lam_init = 0.8 - 0.6 * math.exp(-0.3 * l)
        ya = _s5_pallas(p, s5_lambda_re[l], s5_lambda_im[l], s5_log_step[l], s5_b_re[l], s5_b_im[l],
                        s5_c_re[l], s5_c_im[l], s5_d[l], s5_w_glu[l], s5_b_glu[l])
        yb = _diff_pallas(p, diff_q_norm[l], diff_k_norm[l], diff_lambda_q1[l], diff_lambda_k1[l],
                          diff_lambda_q2[l], diff_lambda_k2[l], diff_subln[l], lam_init)
        yc = _gdn_pallas(p, gdn_conv[l], gdn_a_log[l], gdn_dt_bias[l], gdn_norm[l])
        yd = _na_pallas(p, na_q_norm[l], na_k_norm[l], na_rpb[l])
        s = _outproj(s, mod_l, [ya, yb, yc, yd], w_out[l].astype(BF16))
        s = _ffn(s, mod_l, norm_ffn2[l], ffn2_w_in[l].astype(BF16), ffn2_w_out[l].astype(BF16), 6)
    return s[None, CTX_LEN:]
```

```python
import functools
import math

import jax
import jax.numpy as jnp
from jax import lax
from jax.experimental import pallas as pl
from jax.experimental.pallas import tpu as pltpu

D_MODEL = 2048
SEQ = 8192
DEPTH = 4
GRID_W = 64
CTX_LEN = 256
ROWS = CTX_LEN + SEQ
GROUP_WIDTH = 512
D_FF = 5632
N_MOD = 9
EPS = 1e-6

S5_CH = 16
S5_GROUPS = GROUP_WIDTH // S5_CH
S5_STATE = 64
DIFF_HEADS = 4
DIFF_HEAD_DIM = 64
ROPE_BASE = 10000.0
Q_BLOCK = 128
GDN_HEADS = 4
GDN_HEAD_DIM = 128
GDN_CONV = 5
GDN_CHUNK = 64
NA_HEADS = 8
NA_HEAD_DIM = 64
WIN_H = 8
WIN_W = 16
NA_KEY_COLS = 2 * WIN_W

P_S5 = 0
P_DIFF = 512
P_GDN_QKV = 2048
P_GDN_Z = 3584
P_NA = 4096
P_GDN_AB = 5632
P_WIDTH = 5760

VMEM_LIMIT_BYTES = 56 * 1024 * 1024

F32 = jnp.float32
BF16 = jnp.bfloat16


MOD_TN = 1024


def _mod_kernel(ct_ref, w_ref, b_ref, o_ref):
    c = ct_ref[...]
    s = c * jax.nn.sigmoid(c)
    v0 = jnp.broadcast_to(s[:, 0:1], (D_MODEL, 128))
    v1 = jnp.broadcast_to(s[:, 1:2], (D_MODEL, 128))
    for j in range(MOD_TN // 128):
        w = w_ref[:, j * 128:(j + 1) * 128]
        b = b_ref[:, j * 128:(j + 1) * 128]
        o_ref[0:1, j * 128:(j + 1) * 128] = jnp.sum(w * v0, axis=0, keepdims=True) + b
        o_ref[1:2, j * 128:(j + 1) * 128] = jnp.sum(w * v1, axis=0, keepdims=True) + b


def _modulation(c, c_ctx, w_ada, b_ada):
    n = N_MOD * D_MODEL
    ct = jnp.stack([c.reshape(D_MODEL), c_ctx.reshape(D_MODEL)], axis=1)
    out = pl.pallas_call(
        _mod_kernel,
        grid=(DEPTH, n // MOD_TN),
        in_specs=[
            pl.BlockSpec((D_MODEL, 2), lambda l, j: (0, 0)),
            pl.BlockSpec((None, D_MODEL, MOD_TN), lambda l, j: (l, 0, j)),
            pl.BlockSpec((None, 1, MOD_TN), lambda l, j: (l, 0, j)),
        ],
        out_specs=pl.BlockSpec((None, 2, MOD_TN), lambda l, j: (l, 0, j)),
        out_shape=jax.ShapeDtypeStruct((DEPTH, 2, n), F32),
        compiler_params=pltpu.CompilerParams(
            dimension_semantics=("arbitrary", "arbitrary"), vmem_limit_bytes=VMEM_LIMIT_BYTES),
        name="adaln_mod",
    )(ct, w_ada, b_ada.reshape(DEPTH, 1, n))
    return out.reshape(DEPTH, 2 * N_MOD, D_MODEL)


def _is_ctx_rows(tile_rows):
    rows = pl.program_id(0) * tile_rows + lax.broadcasted_iota(jnp.int32, (tile_rows, 1), 0)
    return rows < CTX_LEN


def _mod_row(mod_ref, is_ctx, k):
    return jnp.where(is_ctx, mod_ref[N_MOD + k:N_MOD + k + 1, :], mod_ref[k:k + 1, :])


NORM_ROWS = 16


def _mod_chunk_row(mod_ref, first_row, k):
    off = jnp.where(first_row < CTX_LEN, N_MOD, 0)
    return mod_ref[pl.ds(off + k, 1), :]


def _adaln_to_scratch(x_ref, gamma_ref, mod_ref, h_ref, tile_rows, base):
    row0 = pl.program_id(0) * tile_rows

    def body(r, carry):
        lo = pl.multiple_of(r * NORM_ROWS, NORM_ROWS)
        x = x_ref[pl.ds(lo, NORM_ROWS), :]
        ms = jnp.mean(x * x, axis=-1, keepdims=True)
        y = x * lax.rsqrt(ms + EPS) * gamma_ref[...]
        h = y * (1.0 + _mod_chunk_row(mod_ref, row0 + lo, base + 1)) + _mod_chunk_row(mod_ref, row0 + lo, base)
        h_ref[pl.ds(lo, NORM_ROWS), :] = h.astype(BF16)
        return carry

    lax.fori_loop(0, tile_rows // NORM_ROWS, body, 0, unroll=4)


FFN_TM = 1408
FFN_RC = 352
FFN_TF = 512


def _ffn_kernel(x_ref, mod_ref, gamma_ref, wg_ref, wu_ref, wo_ref, o_ref, h_ref, *, base):
    f = pl.program_id(1)
    last = pl.num_programs(1) - 1
    row0 = pl.program_id(0) * FFN_TM

    @pl.when(f == 0)
    def _():
        _adaln_to_scratch(x_ref, gamma_ref, mod_ref, h_ref, FFN_TM, base)
        o_ref[...] = jnp.zeros_like(o_ref)

    for r in range(FFN_TM // FFN_RC):
        rows = pl.ds(r * FFN_RC, FFN_RC)
        h = h_ref[rows, :]
        g = jnp.dot(h, wg_ref[...], preferred_element_type=F32)
        u = jnp.dot(h, wu_ref[...], preferred_element_type=F32)
        a = (g * jax.nn.sigmoid(g) * u).astype(BF16)
        o_ref[rows, :] += jnp.dot(a, wo_ref[...], preferred_element_type=F32)

    @pl.when(f == last)
    def _():
        def fin(r, carry):
            lo = pl.multiple_of(r * NORM_ROWS, NORM_ROWS)
            gate = _mod_chunk_row(mod_ref, row0 + lo, base + 2)
            sl = pl.ds(lo, NORM_ROWS)
            o_ref[sl, :] = x_ref[sl, :] + 0.5 * gate * o_ref[sl, :]
            return carry

        lax.fori_loop(0, FFN_TM // NORM_ROWS, fin, 0, unroll=4)


def _ffn(s, mod_l, gamma, w_in_bf, w_out_bf, layer, base):
    nf = D_FF // FFN_TF
    return pl.pallas_call(
        functools.partial(_ffn_kernel, base=base),
        grid=(ROWS // FFN_TM, nf),
        in_specs=[
            pl.BlockSpec((FFN_TM, D_MODEL), lambda i, f: (i, 0), pipeline_mode=pl.Buffered(1)),
            pl.BlockSpec((2 * N_MOD, D_MODEL), lambda i, f: (0, 0)),
            pl.BlockSpec((1, D_MODEL), lambda i, f: (0, 0)),
            pl.BlockSpec((None, D_MODEL, FFN_TF), lambda i, f: (layer, 0, f)),
            pl.BlockSpec((None, D_MODEL, FFN_TF), lambda i, f: (layer, 0, nf + f)),
            pl.BlockSpec((None, FFN_TF, D_MODEL), lambda i, f: (layer, f, 0)),
        ],
        out_specs=pl.BlockSpec((FFN_TM, D_MODEL), lambda i, f: (i, 0), pipeline_mode=pl.Buffered(1)),
        out_shape=jax.ShapeDtypeStruct((ROWS, D_MODEL), F32),
        scratch_shapes=[pltpu.VMEM((FFN_TM, D_MODEL), BF16)],
        compiler_params=pltpu.CompilerParams(
            dimension_semantics=("arbitrary", "arbitrary"), vmem_limit_bytes=VMEM_LIMIT_BYTES),
        name="ffn_swiglu",
    )(s, mod_l, gamma.reshape(1, D_MODEL), w_in_bf, w_in_bf, w_out_bf)


INP_TM = 1408
INP_RC = 352
INP_TN = 640


def _inproj_kernel(x_ref, mod_ref, gamma_ref, w_ref, o_ref, h_ref):
    @pl.when(pl.program_id(1) == 0)
    def _():
        _adaln_to_scratch(x_ref, gamma_ref, mod_ref, h_ref, INP_TM, 3)

    for r in range(INP_TM // INP_RC):
        rows = pl.ds(r * INP_RC, INP_RC)
        o_ref[rows, :] = jnp.dot(h_ref[rows, :], w_ref[...], preferred_element_type=F32)


def _inproj(s, mod_l, gamma, w_bf, layer):
    return pl.pallas_call(
        _inproj_kernel,
        grid=(ROWS // INP_TM, P_WIDTH // INP_TN),
        in_specs=[
            pl.BlockSpec((INP_TM, D_MODEL), lambda i, n: (i, 0), pipeline_mode=pl.Buffered(1)),
            pl.BlockSpec((2 * N_MOD, D_MODEL), lambda i, n: (0, 0)),
            pl.BlockSpec((1, D_MODEL), lambda i, n: (0, 0)),
            pl.BlockSpec((None, D_MODEL, INP_TN), lambda i, n: (layer, 0, n)),
        ],
        out_specs=pl.BlockSpec((INP_TM, INP_TN), lambda i, n: (i, n)),
        out_shape=jax.ShapeDtypeStruct((ROWS, P_WIDTH), F32),
        scratch_shapes=[pltpu.VMEM((INP_TM, D_MODEL), BF16)],
        compiler_params=pltpu.CompilerParams(
            dimension_semantics=("arbitrary", "arbitrary"), vmem_limit_bytes=VMEM_LIMIT_BYTES),
        name="in_proj",
    )(s, mod_l, gamma.reshape(1, D_MODEL), w_bf)


OUT_TM = 384


def _outproj_kernel(x_ref, mod_ref, ya_ref, yb_ref, yc_ref, yd_ref, w_ref, o_ref):
    is_ctx = _is_ctx_rows(OUT_TM)
    acc = jnp.zeros((OUT_TM, D_MODEL), F32)
    for k, y_ref in enumerate((ya_ref, yb_ref, yc_ref, yd_ref)):
        acc += jnp.dot(y_ref[...].astype(BF16), w_ref[k * GROUP_WIDTH:(k + 1) * GROUP_WIDTH, :],
                       preferred_element_type=F32)
    o_ref[...] = x_ref[...] + _mod_row(mod_ref, is_ctx, 5) * acc


def _outproj(s, mod_l, ys, w_bf, layer):
    yspec = pl.BlockSpec((OUT_TM, GROUP_WIDTH), lambda i: (i, 0))
    return pl.pallas_call(
        _outproj_kernel,
        grid=(ROWS // OUT_TM,),
        in_specs=[
            pl.BlockSpec((OUT_TM, D_MODEL), lambda i: (i, 0)),
            pl.BlockSpec((2 * N_MOD, D_MODEL), lambda i: (0, 0)),
            yspec, yspec, yspec, yspec,
            pl.BlockSpec((None, D_MODEL, D_MODEL), lambda i: (layer, 0, 0)),
        ],
        out_specs=pl.BlockSpec((OUT_TM, D_MODEL), lambda i: (i, 0)),
        out_shape=jax.ShapeDtypeStruct((ROWS, D_MODEL), F32),
        compiler_params=pltpu.CompilerParams(
            dimension_semantics=("arbitrary",), vmem_limit_bytes=VMEM_LIMIT_BYTES),
        name="out_proj",
    )(s, mod_l, *ys, w_bf)


S5_TC = 256
S5_MM_ROWS = 512


def _s5_prepare(lam_re, lam_im, log_step, b_re, b_im, c_re, c_im):
    dt = jnp.exp(log_step)[..., None]
    mag = jnp.exp(lam_re * dt)
    ar = mag * jnp.cos(lam_im * dt)
    ai = mag * jnp.sin(lam_im * dt)
    den = lam_re * lam_re + lam_im * lam_im
    fr = ((ar - 1.0) * lam_re + ai * lam_im) / den
    fi = (ai * lam_re - (ar - 1.0) * lam_im) / den
    bbr = fr[..., None] * b_re - fi[..., None] * b_im
    bbi = fr[..., None] * b_im + fi[..., None] * b_re
    in_oct = jnp.eye(4, dtype=F32)[jnp.arange(S5_GROUPS) % 4]
    half = 4 * S5_STATE

    def w_in_half(bb):
        return jnp.einsum('dgpn,gj->dgnjp', bb, in_oct).reshape(2, GROUP_WIDTH, half)

    def w_out_half(cc):
        return jnp.einsum('dgcp,gj->djpgc', cc, in_oct).reshape(2, half, GROUP_WIDTH)

    w_in = jnp.concatenate([w_in_half(bbr), w_in_half(bbi)], axis=2)
    w_out = jnp.concatenate([w_out_half(c_re), w_out_half(-c_im)], axis=1)
    return (ar.reshape(2, 8, half), ai.reshape(2, 8, half), w_in.astype(BF16), w_out.astype(BF16))


def _s5_scan_kernel(uf_ref, ub_ref, win_ref, wout_ref, ar_ref, ai_ref, yf_ref, yb_ref,
                    ls_ref, buf_ref, bub_ref, r_ref, h_ref):
    tc = S5_TC
    half = 256
    n_mm = 8 * tc // S5_MM_ROWS

    @pl.when(pl.program_id(0) == 0)
    def _():
        ls_ref[...] = jnp.zeros_like(ls_ref)
        h_ref[...] = jnp.zeros_like(h_ref)

    lo_half = lax.broadcasted_iota(jnp.int32, (tc, 128), 1) < 64

    def project_in(u_ref, d, bu_ref):
        for b in range(4):
            ub = u_ref[:, b * 128:(b + 1) * 128]
            ls_ref[b, pl.ds(2 * b, tc, stride=8), :] = jnp.where(lo_half, ub, 0.0)
            ls_ref[b, pl.ds(2 * b + 1, tc, stride=8), :] = jnp.where(lo_half, 0.0, ub)

        def mm(c, carry):
            lo = pl.multiple_of(c * S5_MM_ROWS, S5_MM_ROWS)
            lhs = jnp.concatenate([ls_ref[b, pl.ds(lo, S5_MM_ROWS), :] for b in range(4)], axis=1)
            bu_ref[pl.ds(lo, S5_MM_ROWS), :] = jnp.dot(lhs.astype(BF16), win_ref[d],
                                                      preferred_element_type=F32)
            return carry

        lax.fori_loop(0, n_mm, mm, 0)

    project_in(uf_ref, 0, buf_ref)
    project_in(ub_ref, 1, bub_ref)

    arf, aif, arb, aib = ar_ref[0], ai_ref[0], ar_ref[1], ai_ref[1]

    def step(t, carry):
        hfr, hfi, hbr, hbi = carry
        of = pl.multiple_of(t * 8, 8)
        ob = pl.multiple_of((tc - 1 - t) * 8, 8)
        bf = buf_ref[pl.ds(of, 8), :]
        bb = bub_ref[pl.ds(ob, 8), :]
        nfr = arf * hfr - aif * hfi + bf[:, :half]
        nfi = arf * hfi + aif * hfr + bf[:, half:]
        nbr = arb * hbr - aib * hbi + bb[:, :half]
        nbi = arb * hbi + aib * hbr + bb[:, half:]
        buf_ref[pl.ds(of, 8), :] = jnp.concatenate([nfr, nfi], axis=1)
        bub_ref[pl.ds(ob, 8), :] = jnp.concatenate([nbr, nbi], axis=1)
        return nfr, nfi, nbr, nbi

    carry = lax.fori_loop(0, tc, step, (h_ref[0], h_ref[1], h_ref[2], h_ref[3]), unroll=8)
    for i in range(4):
        h_ref[i] = carry[i]

    def project_out(bu_ref, d, y_ref):
        def mm(c, carry):
            lo = pl.multiple_of(c * S5_MM_ROWS, S5_MM_ROWS)
            res = jnp.dot(bu_ref[pl.ds(lo, S5_MM_ROWS), :].astype(BF16), wout_ref[d],
                          preferred_element_type=F32)
            for b in range(4):
                r_ref[b, pl.ds(lo, S5_MM_ROWS), :] = res[:, b * 128:(b + 1) * 128]
            return carry

        lax.fori_loop(0, n_mm, mm, 0)
        for b in range(4):
            even = r_ref[b, pl.ds(2 * b, tc, stride=8), :]
            odd = r_ref[b, pl.ds(2 * b + 1, tc, stride=8), :]
            y_ref[:, b * 128:(b + 1) * 128] = jnp.where(lo_half, even, odd)

    project_out(buf_ref, 0, yf_ref)
    project_out(bub_ref, 1, yb_ref)


def _s5_scan_call(p, a_r, a_i, w_in, w_out):
    rows = p.shape[0]
    n_chunks = rows // S5_TC
    bwd = lambda j: jnp.where(j == 0, 0, n_chunks - j)
    full = lambda shape: pl.BlockSpec(shape, lambda j: (0,) * len(shape))
    y = jax.ShapeDtypeStruct((rows, GROUP_WIDTH), F32)
    return pl.pallas_call(
        _s5_scan_kernel,
        grid=(n_chunks,),
        in_specs=[
            pl.BlockSpec((S5_TC, GROUP_WIDTH), lambda j: (j, 0)),
            pl.BlockSpec((S5_TC, GROUP_WIDTH), lambda j: (bwd(j), 0)),
            full((2, GROUP_WIDTH, GROUP_WIDTH)), full((2, GROUP_WIDTH, GROUP_WIDTH)),
            full((2, 8, 256)), full((2, 8, 256)),
        ],
        out_specs=[pl.BlockSpec((S5_TC, GROUP_WIDTH), lambda j: (j, 0)),
                   pl.BlockSpec((S5_TC, GROUP_WIDTH), lambda j: (bwd(j), 0))],
        out_shape=[y, y],
        scratch_shapes=[
            pltpu.VMEM((4, 8 * S5_TC, 128), F32),
            pltpu.VMEM((8 * S5_TC, GROUP_WIDTH), F32),
            pltpu.VMEM((8 * S5_TC, GROUP_WIDTH), F32),
            pltpu.VMEM((4, 8 * S5_TC, 128), F32),
            pltpu.VMEM((4, 8, 256), F32),
        ],
        compiler_params=pltpu.CompilerParams(
            dimension_semantics=("arbitrary",), vmem_limit_bytes=VMEM_LIMIT_BYTES),
        name="s5_scan",
    )(p, p, w_in, w_out, a_r, a_i)


S5_FIN_TM = 384


def _s5_finish_kernel(yf_ref, yb_ref, u_ref, d_ref, w_ref, b_ref, o_ref):
    y = yf_ref[...] + yb_ref[...] + d_ref[...] * u_ref[...]
    h = jnp.dot(jax.nn.gelu(y).astype(BF16), w_ref[...], preferred_element_type=F32) + b_ref[...]
    o_ref[...] = h[:, :GROUP_WIDTH] * jax.nn.sigmoid(h[:, GROUP_WIDTH:])


def _s5_finish(yf, yb, p, d_skip, w_glu_bf, b_glu):
    rows = p.shape[0]
    row_blk = pl.BlockSpec((S5_FIN_TM, GROUP_WIDTH), lambda i: (i, 0))
    return pl.pallas_call(
        _s5_finish_kernel,
        grid=(rows // S5_FIN_TM,),
        in_specs=[row_blk, row_blk, row_blk,
                  pl.BlockSpec((1, GROUP_WIDTH), lambda i: (0, 0)),
                  pl.BlockSpec((GROUP_WIDTH, 2 * GROUP_WIDTH), lambda i: (0, 0)),
                  pl.BlockSpec((1, 2 * GROUP_WIDTH), lambda i: (0, 0))],
        out_specs=row_blk,
        out_shape=jax.ShapeDtypeStruct((rows, GROUP_WIDTH), F32),
        compiler_params=pltpu.CompilerParams(
            dimension_semantics=("arbitrary",), vmem_limit_bytes=VMEM_LIMIT_BYTES),
        name="s5_finish",
    )(yf, yb, p, d_skip.reshape(1, GROUP_WIDTH), w_glu_bf, b_glu.reshape(1, 2 * GROUP_WIDTH))


def _s5_pallas(p, lam_re, lam_im, log_step, b_re, b_im, c_re, c_im, d_skip, w_glu, b_glu):
    a_r, a_i, w_in, w_out = _s5_prepare(lam_re, lam_im, log_step, b_re, b_im, c_re, c_im)
    yf, yb = _s5_scan_call(p, a_r, a_i, w_in, w_out)
    return _s5_finish(yf, yb, p, d_skip, w_glu.astype(BF16), b_glu)


PREP_TM = 256
PREP_RC = 128


def _segment_ones(width, seg):
    i = jnp.arange(width) // seg
    return (i[:, None] == i[None, :]).astype(BF16)


def _seg_sumsq(x, e):
    x2 = x * x
    hi = x2.astype(BF16)
    lo = (x2 - hi.astype(F32)).astype(BF16)
    return jnp.dot(hi, e, preferred_element_type=F32) + jnp.dot(lo, e, preferred_element_type=F32)


NA_QROWS = 4
NA_KROWS = NA_QROWS + WIN_H
NA_NQ = NA_QROWS * GRID_W
NA_NK = NA_KROWS * GRID_W
MASKED = -1e30


def _na_prep_kernel(q_ref, k_ref, v_ref, e_ref, qw_ref, kw_ref, qo_ref, ko_ref, vo_ref):
    def body(c, carry):
        sl = pl.ds(pl.multiple_of(c * PREP_RC, PREP_RC), PREP_RC)
        q = q_ref[sl, :]
        k = k_ref[sl, :]
        qn = q * lax.rsqrt(_seg_sumsq(q, e_ref[...]) * (1.0 / NA_HEAD_DIM) + EPS) * qw_ref[...]
        kn = k * lax.rsqrt(_seg_sumsq(k, e_ref[...]) * (1.0 / NA_HEAD_DIM) + EPS) * kw_ref[...]
        qo_ref[sl, :] = (qn * NA_HEAD_DIM ** -0.5).astype(BF16)
        ko_ref[sl, :] = kn.astype(BF16)
        vo_ref[sl, :] = v_ref[sl, :].astype(BF16)
        return carry

    lax.fori_loop(0, PREP_TM // PREP_RC, body, 0)


def _na_prep(p, q_norm, k_norm):
    rows = p.shape[0]
    assert rows % PREP_TM == 0
    c0 = P_NA // GROUP_WIDTH
    col = lambda j: pl.BlockSpec((PREP_TM, GROUP_WIDTH), lambda i: (i, c0 + j))
    vec = pl.BlockSpec((1, GROUP_WIDTH), lambda i: (0, 0))
    out = jax.ShapeDtypeStruct((rows, GROUP_WIDTH), BF16)
    blk = pl.BlockSpec((PREP_TM, GROUP_WIDTH), lambda i: (i, 0))
    return pl.pallas_call(
        _na_prep_kernel,
        grid=(rows // PREP_TM,),
        in_specs=[col(0), col(1), col(2),
                  pl.BlockSpec((GROUP_WIDTH, GROUP_WIDTH), lambda i: (0, 0)), vec, vec],
        out_specs=[blk, blk, blk],
        out_shape=[out, out, out],
        compiler_params=pltpu.CompilerParams(
            dimension_semantics=("arbitrary",), vmem_limit_bytes=VMEM_LIMIT_BYTES),
        name="na_prep",
    )(p, p, p, _segment_ones(GROUP_WIDTH, NA_HEAD_DIM),
      jnp.tile(q_norm, NA_HEADS).reshape(1, GROUP_WIDTH), jnp.tile(k_norm, NA_HEADS).reshape(1, GROUP_WIDTH))


def _na_bias_table(rpb, n_grid_rows):
    import numpy as np
    c = np.arange(GRID_W)
    col0 = np.clip(c - WIN_W // 2, 0, GRID_W - WIN_W)
    kc = np.arange(GRID_W)
    col_ok = (kc[None, :] >= col0[:, None]) & (kc[None, :] < col0[:, None] + WIN_W)
    dcol = kc[None, :] - c[:, None] + WIN_W - 1
    shift = ((dcol[None] == np.arange(2 * WIN_W - 1)[:, None, None]) & col_ok[None]).astype(np.float32)
    jr = np.arange(NA_QROWS)[:, None]
    kr = np.arange(NA_KROWS)[None, :]
    ri = np.stack([kr - jr + WIN_H - 1, kr - jr + WIN_H - 1 - WIN_H // 2, kr - jr + WIN_H - 1 - WIN_H])
    ok = np.stack([(kr < WIN_H) & (jr >= 0), (kr - jr >= 0) & (kr - jr < WIN_H), (kr >= NA_QROWS) & (jr >= 0)])
    sel = (ok[..., None] & (ri[..., None] == np.arange(2 * WIN_H - 1))).astype(np.float32)
    valid = ok[:, :, None, :, None] & col_ok[None, None, :, None, :]
    neg = np.where(valid, 0.0, MASKED).astype(np.float32).reshape(3, 1, NA_NQ, NA_NK)
    hi = lax.Precision.HIGHEST
    band = jnp.einsum('hrj,jck->hrck', rpb, jnp.asarray(shift), precision=hi)
    tab = jnp.einsum('vjkr,hrcx->vhjckx', jnp.asarray(sel), band, precision=hi)
    return tab.reshape(3, NA_HEADS, NA_NQ, NA_NK) + jnp.asarray(neg)


def _softmax_pv(scores, values):
    m = functools.reduce(jnp.maximum, [jnp.max(s, axis=1, keepdims=True) for s in scores])
    ps = [jnp.exp(s - m) for s in scores]
    denom = functools.reduce(lambda a, b: a + b, [jnp.sum(p, axis=1, keepdims=True) for p in ps])
    acc = functools.reduce(lambda a, b: a + b,
                           [jnp.dot(p.astype(BF16), v, preferred_element_type=F32) for p, v in zip(ps, values)])
    return acc / denom


def _dot_nt(a, b):
    return lax.dot_general(a, b, (((1,), (1,)), ((), ())), preferred_element_type=F32)


def _na_kernel(q_ref, k_ref, v_ref, tab_ref, o_ref, *, n_blocks):
    i = pl.program_id(1)
    lo_half = lax.broadcasted_iota(jnp.int32, (NA_NQ, 128), 1) < NA_HEAD_DIM
    q = q_ref[...]
    zero = jnp.zeros_like(q)
    q_heads = (jnp.where(lo_half, q, zero), jnp.where(lo_half, zero, q))
    kc = k_ref[0:CTX_LEN, :]
    vc = v_ref[0:CTX_LEN, :]

    @pl.when(i == 0)
    def _():
        outs = [_softmax_pv([_dot_nt(qh, kc)], [vc]) for qh in q_heads]
        o_ref[...] = jnp.where(lo_half, outs[0], outs[1])

    @pl.when(i > 0)
    def _():
        ib = i - 1
        kr0 = jnp.clip(NA_QROWS * ib - WIN_H // 2, 0, NA_QROWS * n_blocks - NA_KROWS)
        start = pl.multiple_of(CTX_LEN + kr0 * GRID_W, GRID_W)
        kwin = k_ref[pl.ds(start, NA_NK), :]
        vwin = v_ref[pl.ds(start, NA_NK), :]
        variant = jnp.where(ib == 0, 0, jnp.where(ib == n_blocks - 1, 2, 1))
        outs = []
        for e, qh in enumerate(q_heads):
            s_loc = _dot_nt(qh, kwin) + tab_ref[variant, e]
            outs.append(_softmax_pv([s_loc, _dot_nt(qh, kc)], [vwin, vc]))
        o_ref[...] = jnp.where(lo_half, outs[0], outs[1])


def _na_pallas(p, q_norm, k_norm, rpb):
    rows = p.shape[0]
    n_grid_rows = (rows - CTX_LEN) // GRID_W
    n_blocks = n_grid_rows // NA_QROWS
    q, k, v = _na_prep(p, q_norm, k_norm)
    tab = _na_bias_table(rpb, n_grid_rows).reshape(3, NA_HEADS // 2, 2, NA_NQ, NA_NK)
    kv = pl.BlockSpec((rows, 128), lambda h, i: (0, h))
    return pl.pallas_call(
        functools.partial(_na_kernel, n_blocks=n_blocks),
        grid=(NA_HEADS // 2, n_blocks + 1),
        in_specs=[pl.BlockSpec((NA_NQ, 128), lambda h, i: (i, h)), kv, kv,
                  pl.BlockSpec((3, None, 2, NA_NQ, NA_NK), lambda h, i: (0, h, 0, 0, 0))],
        out_specs=pl.BlockSpec((NA_NQ, 128), lambda h, i: (i, h)),
        out_shape=jax.ShapeDtypeStruct((rows, GROUP_WIDTH), F32),
        compiler_params=pltpu.CompilerParams(
            dimension_semantics=("arbitrary", "arbitrary"), vmem_limit_bytes=VMEM_LIMIT_BYTES),
        name="na_attention",
    )(q, k, v, tab)


DIFF_TQ = 256
DIFF_TK = 768
DIFF_HD = 2 * DIFF_HEAD_DIM


def _rope_tables_rows(rows):
    pos = jnp.arange(rows - CTX_LEN)
    row = (pos // GRID_W).astype(F32)
    col = (pos % GRID_W).astype(F32)
    n_freq = DIFF_HEAD_DIM // 4
    inv = ROPE_BASE ** (-jnp.arange(n_freq, dtype=F32) / n_freq)
    ang = jnp.concatenate([row[:, None] * inv, row[:, None] * inv, col[:, None] * inv, col[:, None] * inv], -1)
    sign = jnp.tile(jnp.repeat(jnp.array([-1.0, 1.0, -1.0, 1.0], F32), n_freq), 2)
    cos = jnp.concatenate([jnp.ones((CTX_LEN, DIFF_HD), F32), jnp.tile(jnp.cos(ang), (1, 2))], axis=0)
    sin = jnp.concatenate([jnp.zeros((CTX_LEN, DIFF_HD), F32), jnp.tile(jnp.sin(ang), (1, 2)) * sign], axis=0)
    return cos, sin


def _diff_prep_kernel(q_ref, k_ref, v_ref, e_ref, qw_ref, kw_ref, cos_ref, sin_ref, qo_ref, ko_ref, vo_ref):
    quarter = lax.broadcasted_iota(jnp.int32, (PREP_RC, GROUP_WIDTH), 1) // (DIFF_HEAD_DIM // 4)
    first_of_pair = quarter % 2 == 0

    def rope(x, cos, sin):
        partner = jnp.where(first_of_pair, pltpu.roll(x, GROUP_WIDTH - DIFF_HEAD_DIM // 4, 1),
                            pltpu.roll(x, DIFF_HEAD_DIM // 4, 1))
        return x * cos + partner * sin

    def body(c, carry):
        sl = pl.ds(pl.multiple_of(c * PREP_RC, PREP_RC), PREP_RC)
        cos = jnp.concatenate([cos_ref[sl, :]] * DIFF_HEADS, axis=1)
        sin = jnp.concatenate([sin_ref[sl, :]] * DIFF_HEADS, axis=1)
        q = q_ref[sl, :]
        k = k_ref[sl, :]
        qn = q * lax.rsqrt(_seg_sumsq(q, e_ref[...]) * (1.0 / DIFF_HEAD_DIM) + EPS) * qw_ref[...]
        kn = k * lax.rsqrt(_seg_sumsq(k, e_ref[...]) * (1.0 / DIFF_HEAD_DIM) + EPS) * kw_ref[...]
        qo_ref[sl, :] = (rope(qn, cos, sin) * DIFF_HEAD_DIM ** -0.5).astype(BF16)
        ko_ref[sl, :] = rope(kn, cos, sin).astype(BF16)
        v = v_ref[sl, :].astype(BF16)
        ones = jnp.ones((PREP_RC, DIFF_HD), BF16)
        for h in range(DIFF_HEADS):
            vo_ref[sl, 2 * h * DIFF_HD:(2 * h + 1) * DIFF_HD] = v[:, h * DIFF_HD:(h + 1) * DIFF_HD]
            vo_ref[sl, (2 * h + 1) * DIFF_HD:(2 * h + 2) * DIFF_HD] = ones
        return carry

    lax.fori_loop(0, PREP_TM // PREP_RC, body, 0)


def _diff_prep(p, q_norm, k_norm):
    rows = p.shape[0]
    assert rows % PREP_TM == 0
    c0 = P_DIFF // GROUP_WIDTH
    col = lambda j: pl.BlockSpec((PREP_TM, GROUP_WIDTH), lambda i: (i, c0 + j))
    vec = pl.BlockSpec((1, GROUP_WIDTH), lambda i: (0, 0))
    tab = pl.BlockSpec((PREP_TM, DIFF_HD), lambda i: (i, 0))
    blk = pl.BlockSpec((PREP_TM, GROUP_WIDTH), lambda i: (i, 0))
    cos, sin = _rope_tables_rows(rows)
    return pl.pallas_call(
        _diff_prep_kernel,
        grid=(rows // PREP_TM,),
        in_specs=[col(0), col(1), col(2),
                  pl.BlockSpec((GROUP_WIDTH, GROUP_WIDTH), lambda i: (0, 0)), vec, vec, tab, tab],
        out_specs=[blk, blk, pl.BlockSpec((PREP_TM, 2 * GROUP_WIDTH), lambda i: (i, 0))],
        out_shape=[jax.ShapeDtypeStruct((rows, GROUP_WIDTH), BF16), jax.ShapeDtypeStruct((rows, GROUP_WIDTH), BF16),
                   jax.ShapeDtypeStruct((rows, 2 * GROUP_WIDTH), BF16)],
        compiler_params=pltpu.CompilerParams(
            dimension_semantics=("arbitrary",), vmem_limit_bytes=VMEM_LIMIT_BYTES),
        name="diff_prep",
    )(p, p, p, _segment_ones(GROUP_WIDTH, DIFF_HEAD_DIM),
      jnp.tile(q_norm, 2 * DIFF_HEADS).reshape(1, GROUP_WIDTH),
      jnp.tile(k_norm, 2 * DIFF_HEADS).reshape(1, GROUP_WIDTH), cos, sin)


def _diff_kernel(q_ref, k_ref, v_ref, lv_ref, sw_ref, o_ref, m_ref, acc_ref, s_ref, *, n_kchunks):
    i = pl.program_id(1)
    lo_half = lax.broadcasted_iota(jnp.int32, (DIFF_TQ, DIFF_HD), 1) < DIFF_HEAD_DIM
    q = q_ref[...]
    zero = jnp.zeros_like(q)
    q2 = jnp.concatenate([jnp.where(lo_half, q, zero), jnp.where(lo_half, zero, q)], axis=0)

    m_ref[...] = jnp.full(m_ref.shape, MASKED, F32)
    acc_ref[...] = jnp.zeros_like(acc_ref)

    def keys(c):
        return pl.ds(pl.multiple_of(c * DIFF_TK, DIFF_TK), DIFF_TK)

    def scores(c, slot):
        s_ref[slot] = _dot_nt(q2, k_ref[keys(c), :])

    def accumulate(s, vc):
        m_old = m_ref[...]
        m_new = jnp.maximum(m_old, jnp.max(s, axis=1, keepdims=True))
        alpha = jnp.exp(m_old - m_new)
        p = jnp.exp(s - m_new[:, 0:1])
        acc_ref[...] = (jnp.concatenate([alpha, alpha], axis=1) * acc_ref[...]
                        + jnp.dot(p.astype(BF16), vc, preferred_element_type=F32))
        m_ref[...] = m_new

    @pl.when(i == 0)
    def _():
        accumulate(_dot_nt(q2, k_ref[0:CTX_LEN, :]), v_ref[0:CTX_LEN, :])

    @pl.when(i > 0)
    def _():
        scores(0, 0)

        def pair(t, carry):
            c = 2 * t
            scores(c + 1, 1)
            accumulate(s_ref[0], v_ref[keys(c), :])
            scores(c + 2, 0)
            accumulate(s_ref[1], v_ref[keys(c + 1), :])
            return carry

        lax.fori_loop(0, (n_kchunks - 1) // 2, pair, 0)
        if n_kchunks % 2 == 0:
            scores(n_kchunks - 1, 1)
            accumulate(s_ref[0], v_ref[keys(n_kchunks - 2), :])
            accumulate(s_ref[1], v_ref[keys(n_kchunks - 1), :])
        else:
            accumulate(s_ref[0], v_ref[keys(n_kchunks - 1), :])

    lam_init = lv_ref[4:5, 0:1]
    lam = (jnp.exp(jnp.sum(lv_ref[0:1, :] * lv_ref[1:2, :], axis=1, keepdims=True))
           - jnp.exp(jnp.sum(lv_ref[2:3, :] * lv_ref[3:4, :], axis=1, keepdims=True)) + lam_init)
    a1 = acc_ref[0:DIFF_TQ, :]
    a2 = acc_ref[DIFF_TQ:, :]
    o = a1[:, :DIFF_HD] / a1[:, DIFF_HD:] - lam * (a2[:, :DIFF_HD] / a2[:, DIFF_HD:])
    y = o * lax.rsqrt(jnp.mean(o * o, axis=1, keepdims=True) + EPS) * sw_ref[...]
    o_ref[...] = y * (1.0 - lam_init)


def _diff_pallas(p, q_norm, k_norm, lq1, lk1, lq2, lk2, subln, lam_init):
    rows = p.shape[0]
    assert rows % DIFF_TK == 0 and rows % DIFF_TQ == 0 and CTX_LEN == DIFF_TQ
    q, k, v = _diff_prep(p, q_norm, k_norm)
    pad = lambda t: jnp.pad(t, (0, DIFF_HD - DIFF_HEAD_DIM))
    lvec = jnp.stack([pad(lq1), pad(lk1), pad(lq2), pad(lk2), jnp.full((DIFF_HD,), lam_init, F32),
                      jnp.zeros((DIFF_HD,), F32), jnp.zeros((DIFF_HD,), F32), jnp.zeros((DIFF_HD,), F32)])
    return pl.pallas_call(
        functools.partial(_diff_kernel, n_kchunks=rows // DIFF_TK),
        grid=(DIFF_HEADS, rows // DIFF_TQ),
        in_specs=[pl.BlockSpec((DIFF_TQ, DIFF_HD), lambda h, i: (i, h)),
                  pl.BlockSpec((rows, DIFF_HD), lambda h, i: (0, h)),
                  pl.BlockSpec((rows, 2 * DIFF_HD), lambda h, i: (0, h)),
                  pl.BlockSpec((8, DIFF_HD), lambda h, i: (0, 0)),
                  pl.BlockSpec((1, DIFF_HD), lambda h, i: (0, 0))],
        out_specs=pl.BlockSpec((DIFF_TQ, DIFF_HD), lambda h, i: (i, h)),
        out_shape=jax.ShapeDtypeStruct((rows, GROUP_WIDTH), F32),
        scratch_shapes=[pltpu.VMEM((2 * DIFF_TQ, DIFF_HD), F32),
                        pltpu.VMEM((2 * DIFF_TQ, 2 * DIFF_HD), F32),
                        pltpu.VMEM((2, 2 * DIFF_TQ, DIFF_TK), F32)],
        compiler_params=pltpu.CompilerParams(
            dimension_semantics=("arbitrary", "arbitrary"), vmem_limit_bytes=VMEM_LIMIT_BYTES),
        name="diff_attention",
    )(q, k, v, lvec, subln.reshape(1, DIFF_HD))


GDN_TB = 256
GDN_HALO = 8
GDN_GATES = 2 * GDN_HEADS


def _softplus(x):
    return jnp.maximum(x, 0.0) + jnp.log(1.0 + jnp.exp(-jnp.abs(x)))


def _gdn_prep_kernel(*refs):
    (qp, qc, qn, kp, kc, kn, vp, vc, vn, ab_ref, w_ref, alog_ref, dtb_ref,
     qo_ref, ko_ref, vo_ref, go_ref, pad_ref) = refs
    i = pl.program_id(0)
    last = pl.num_programs(0) - 1
    prev_ok = (i >= 2).astype(F32)
    next_ok = jnp.logical_and(i >= 1, i < last).astype(F32)
    half = GDN_CONV // 2

    def conv_silu(prev_ref, cur_ref, next_ref, sec):
        pad_ref[0:GDN_HALO, :] = prev_ref[...] * prev_ok
        pad_ref[GDN_HALO:GDN_HALO + GDN_TB, :] = cur_ref[...]
        pad_ref[GDN_HALO + GDN_TB:, :] = next_ref[...] * next_ok
        acc = jnp.zeros((GDN_TB, GROUP_WIDTH), F32)
        for j in range(GDN_CONV):
            w = w_ref[j:j + 1, sec * GROUP_WIDTH:(sec + 1) * GROUP_WIDTH]
            acc = acc + pad_ref[GDN_HALO - half + j:GDN_HALO - half + j + GDN_TB, :] * w
        return acc * jax.nn.sigmoid(acc)

    def l2n(x):
        parts = []
        for h in range(GDN_HEADS):
            xh = x[:, h * GDN_HEAD_DIM:(h + 1) * GDN_HEAD_DIM]
            parts.append(xh * lax.rsqrt(jnp.sum(xh * xh, axis=1, keepdims=True) + EPS))
        return jnp.concatenate(parts, axis=1)

    qo_ref[...] = l2n(conv_silu(qp, qc, qn, 0)) * GDN_HEAD_DIM ** -0.5
    ko_ref[...] = l2n(conv_silu(kp, kc, kn, 1))
    vo_ref[...] = conv_silu(vp, vc, vn, 2)
    x = ab_ref[...]
    lane = lax.broadcasted_iota(jnp.int32, x.shape, 1)
    go_ref[...] = jnp.where(lane < GDN_GATES, -jnp.exp(alog_ref[...]) * _softplus(x + dtb_ref[...]),
                            jax.nn.sigmoid(x))


def _gdn_prep(p, conv_w, a_log, dt_bias):
    rows = p.shape[0]
    assert rows % GDN_TB == 0 and CTX_LEN == GDN_TB
    n_halo = rows // GDN_HALO
    per = GDN_TB // GDN_HALO
    c0 = P_GDN_QKV // GROUP_WIDTH
    specs = []
    for sec in range(3):
        specs += [
            pl.BlockSpec((GDN_HALO, GROUP_WIDTH), lambda i, sec=sec: (jnp.maximum(i * per - 1, 0), c0 + sec)),
            pl.BlockSpec((GDN_TB, GROUP_WIDTH), lambda i, sec=sec: (i, c0 + sec)),
            pl.BlockSpec((GDN_HALO, GROUP_WIDTH),
                         lambda i, sec=sec: (jnp.minimum((i + 1) * per, n_halo - 1), c0 + sec)),
        ]
    vec = pl.BlockSpec((1, 128), lambda i: (0, 0))
    specs += [pl.BlockSpec((GDN_TB, 128), lambda i: (i, P_GDN_AB // 128)),
              pl.BlockSpec((8, 3 * GROUP_WIDTH), lambda i: (0, 0)), vec, vec]
    blk = pl.BlockSpec((GDN_TB, GROUP_WIDTH), lambda i: (i, 0))
    out = jax.ShapeDtypeStruct((rows, GROUP_WIDTH), F32)
    pad8 = lambda t: jnp.pad(t.reshape(1, GDN_GATES), ((0, 0), (0, 128 - GDN_GATES)))
    return pl.pallas_call(
        _gdn_prep_kernel,
        grid=(rows // GDN_TB,),
        in_specs=specs,
        out_specs=[blk, blk, blk, pl.BlockSpec((GDN_TB, 128), lambda i: (i, 0))],
        out_shape=[out, out, out, jax.ShapeDtypeStruct((rows, 128), F32)],
        scratch_shapes=[pltpu.VMEM((GDN_TB + 2 * GDN_HALO, GROUP_WIDTH), F32)],
        compiler_params=pltpu.CompilerParams(
            dimension_semantics=("arbitrary",), vmem_limit_bytes=VMEM_LIMIT_BYTES),
        name="gdn_prep",
    )(*([p] * 10), jnp.pad(conv_w, ((0, 8 - GDN_CONV), (0, 0))), pad8(a_log), pad8(dt_bias))


def _mm3(a, b):
    ah = a.astype(BF16)
    al = (a - ah.astype(F32)).astype(BF16)
    bh = b.astype(BF16)
    bl = (b - bh.astype(F32)).astype(BF16)
    dot = lambda x, y: jnp.dot(x, y, preferred_element_type=F32)
    return dot(ah, bh) + dot(ah, bl) + dot(al, bh)


def _gdn_kernel(q_ref, k_ref, v_ref, g_ref, o_ref, s_ref, *, reverse):
    j = pl.program_id(0)
    tb, ck, hd = GDN_TB, GDN_CHUNK, GDN_HEAD_DIM
    d = 1 if reverse else 0

    @pl.when(j == 0)
    def _():
        s_ref[...] = jnp.zeros_like(s_ref)

    ri = lax.broadcasted_iota(jnp.int32, (tb, tb), 0)
    ci = lax.broadcasted_iota(jnp.int32, (tb, tb), 1)
    same = lambda n: (ri // n) == (ci // n)
    same_chunk = same(ck)
    before = (ci > ri) if reverse else (ci < ri)
    strict = jnp.logical_and(same_chunk, before)
    incl = jnp.logical_and(same_chunk, jnp.logical_or(before, ri == ci))
    eye = (ri == ci).astype(F32)

    g = g_ref[...]
    g1 = g.astype(BF16)
    r1 = g - g1.astype(F32)
    g2 = r1.astype(BF16)
    g3 = (r1 - g2.astype(F32)).astype(BF16)
    dot = lambda x, y: jnp.dot(x, y, preferred_element_type=F32)
    cum_m = incl.astype(BF16)
    tot_m = same_chunk.astype(BF16)
    gcum = dot(cum_m, g1) + dot(cum_m, g2) + dot(cum_m, g3)
    gtot = dot(tot_m, g1) + dot(tot_m, g2) + dot(tot_m, g3)
    gcum_t = gcum.T

    order = range(tb // ck - 1, -1, -1) if reverse else range(tb // ck)
    heads = range(GDN_HEADS)
    hs = lambda h: slice(h * hd, (h + 1) * hd)
    lanes = [d * GDN_HEADS + h for h in heads]
    gcol = [gcum[:, l:l + 1] for l in lanes]
    glast = [gtot[:, l:l + 1] for l in lanes]
    beta = [g[:, GDN_GATES + l:GDN_GATES + l + 1] for l in lanes]
    decay = [jnp.where(incl, jnp.exp(jnp.where(incl, gcol[h] - gcum_t[lanes[h]:lanes[h] + 1, :], 0.0)), 0.0)
             for h in heads]
    kb = [k_ref[:, hs(h)] * beta[h] for h in heads]
    k_b = [k_ref[:, hs(h)].astype(BF16) for h in heads]
    a_mat = [jnp.where(strict, _dot_nt(kb[h].astype(BF16), k_b[h]) * decay[h], 0.0) for h in heads]
    attn = [(_dot_nt(q_ref[:, hs(h)].astype(BF16), k_b[h]) * decay[h]).astype(BF16) for h in heads]

    x = [eye - jnp.where(same(2), a_mat[h], 0.0) for h in heads]
    for half_blk in (2, 4, 8, 16, 32):
        level = jnp.logical_and(same(2 * half_blk), jnp.logical_not(same(half_blk)))
        x_b = [x[h].astype(BF16) for h in heads]
        xl = [dot(x_b[h], jnp.where(level, a_mat[h], 0.0).astype(BF16)).astype(BF16) for h in heads]
        x = [x[h] - dot(xl[h], x_b[h]) for h in heads]
    resid = [eye - x[h] - _mm3(a_mat[h], x[h]) for h in heads]
    t_mat = [(x[h] + dot(x[h].astype(BF16), resid[h].astype(BF16))).astype(BF16) for h in heads]

    eg = [jnp.exp(gcol[h]) for h in heads]
    wu = [dot(t_mat[h], jnp.concatenate([kb[h] * eg[h], v_ref[:, hs(h)] * beta[h]], axis=1).astype(BF16))
          for h in heads]
    w_b = [wu[h][:, :hd].astype(BF16) for h in heads]
    qd_b = [(q_ref[:, hs(h)] * eg[h]).astype(BF16) for h in heads]
    k_carry = [(k_ref[:, hs(h)] * jnp.exp(glast[h] - gcol[h])).astype(BF16) for h in heads]

    s = [s_ref[h] for h in heads]
    v_new = [[None] * (tb // ck) for _ in heads]
    o_inter = [[None] * (tb // ck) for _ in heads]
    for c in order:
        rows = slice(c * ck, (c + 1) * ck)
        for h in heads:
            s_b = s[h].astype(BF16)
            v_new[h][c] = wu[h][rows, hd:] - dot(w_b[h][rows], s_b)
            o_inter[h][c] = dot(qd_b[h][rows], s_b)
            s[h] = s[h] * jnp.exp(glast[h][c * ck:c * ck + 1, :]) + lax.dot_general(
                k_carry[h][rows], v_new[h][c].astype(BF16), (((0,), (0,)), ((), ())),
                preferred_element_type=F32)
    for h in heads:
        s_ref[h] = s[h]
        v_all = jnp.concatenate(v_new[h], axis=0).astype(BF16)
        o_ref[:, hs(h)] = jnp.concatenate(o_inter[h], axis=0) + dot(attn[h], v_all)


def _gdn_scan(q, k, v, gates, reverse):
    rows = q.shape[0]
    n_blocks = rows // GDN_TB
    blk_of = (lambda j: jnp.where(j == 0, 0, n_blocks - j)) if reverse else (lambda j: j)
    blk = pl.BlockSpec((GDN_TB, GROUP_WIDTH), lambda j: (blk_of(j), 0))
    return pl.pallas_call(
        functools.partial(_gdn_kernel, reverse=reverse),
        grid=(n_blocks,),
        in_specs=[blk, blk, blk, pl.BlockSpec((GDN_TB, 128), lambda j: (blk_of(j), 0))],
        out_specs=blk,
        out_shape=jax.ShapeDtypeStruct((rows, GROUP_WIDTH), F32),
        scratch_shapes=[pltpu.VMEM((GDN_HEADS, GDN_HEAD_DIM, GDN_HEAD_DIM), F32)],
        compiler_params=pltpu.CompilerParams(
            dimension_semantics=("arbitrary",), vmem_limit_bytes=VMEM_LIMIT_BYTES),
        name="gdn_scan_bwd" if reverse else "gdn_scan_fwd",
    )(q, k, v, gates)


def _gdn_finish_kernel(of_ref, ob_ref, z_ref, w_ref, y_ref):
    o = of_ref[...] + ob_ref[...]
    z = z_ref[...]
    parts = []
    for h in range(GDN_HEADS):
        oh = o[:, h * GDN_HEAD_DIM:(h + 1) * GDN_HEAD_DIM]
        parts.append(oh * lax.rsqrt(jnp.mean(oh * oh, axis=1, keepdims=True) + EPS) * w_ref[...])
    y_ref[...] = jnp.concatenate(parts, axis=1) * (z * jax.nn.sigmoid(z))


def _gdn_pallas(p, conv_w, a_log, dt_bias, norm_w):
    rows = p.shape[0]
    q, k, v, gates = _gdn_prep(p, conv_w, a_log, dt_bias)
    o_f = _gdn_scan(q, k, v, gates, False)
    o_b = _gdn_scan(q, k, v, gates, True)
    blk = pl.BlockSpec((GDN_TB, GROUP_WIDTH), lambda i: (i, 0))
    return pl.pallas_call(
        _gdn_finish_kernel,
        grid=(rows // GDN_TB,),
        in_specs=[blk, blk, pl.BlockSpec((GDN_TB, GROUP_WIDTH), lambda i: (i, P_GDN_Z // GROUP_WIDTH)),
                  pl.BlockSpec((1, GDN_HEAD_DIM), lambda i: (0, 0))],
        out_specs=blk,
        out_shape=jax.ShapeDtypeStruct((rows, GROUP_WIDTH), F32),
        compiler_params=pltpu.CompilerParams(
            dimension_semantics=("arbitrary",), vmem_limit_bytes=VMEM_LIMIT_BYTES),
        name="gdn_finish",
    )(o_f, o_b, p, norm_w.reshape(1, GDN_HEAD_DIM))


def _reorder_w_in(w):
    s5 = w[..., 0:512]
    diff = w[..., 512:2048]
    gdn_qkv = w[..., 2048:3584]
    gdn_z = w[..., 3584:4096]
    gdn_ab = w[..., 4096:4112]
    na = w[..., 4112:5648]
    pad = jnp.zeros(w.shape[:-1] + (P_WIDTH - P_GDN_AB - 16,), w.dtype)
    return jnp.concatenate([s5, diff, gdn_qkv, gdn_z, na, gdn_ab, pad], axis=-1)


def kernel(x, c, ctx, c_ctx, w_ada, b_ada, norm_ffn1, norm_mix, norm_ffn2, ffn1_w_in, ffn1_w_out,
           ffn2_w_in, ffn2_w_out, w_in, w_out, s5_lambda_re, s5_lambda_im, s5_log_step, s5_b_re,
           s5_b_im, s5_c_re, s5_c_im, s5_d, s5_w_glu, s5_b_glu, diff_q_norm, diff_k_norm,
           diff_lambda_q1, diff_lambda_k1, diff_lambda_q2, diff_lambda_k2, diff_subln, gdn_conv,
           gdn_a_log, gdn_dt_bias, gdn_norm, na_q_norm, na_k_norm, na_rpb):
    mod = _modulation(c, c_ctx, w_ada, b_ada)
    s = jnp.concatenate([ctx[0], x[0]], axis=0)
    ffn1_in, ffn1_out = ffn1_w_in.astype(BF16), ffn1_w_out.astype(BF16)
    ffn2_in, ffn2_out = ffn2_w_in.astype(BF16), ffn2_w_out.astype(BF16)
    w_in_bf, w_out_bf = _reorder_w_in(w_in).astype(BF16), w_out.astype(BF16)
    for l in range(DEPTH):
        mod_l = mod[l]
        s = _ffn(s, mod_l, norm_ffn1[l], ffn1_in, ffn1_out, l, 0)
        p = _inproj(s, mod_l, norm_mix[l], w_in_bf, l)
        lam_init = 0.8 - 0.6 * math.exp(-0.3 * l)
        ya = _s5_pallas(p, s5_lambda_re[l], s5_lambda_im[l], s5_log_step[l], s5_b_re[l], s5_b_im[l],
                        s5_c_re[l], s5_c_im[l], s5_d[l], s5_w_glu[l], s5_b_glu[l])
        yb = _diff_pallas(p, diff_q_norm[l], diff_k_norm[l], diff_lambda_q1[l], diff_lambda_k1[l],
                          diff_lambda_q2[l], diff_lambda_k2[l], diff_subln[l], lam_init)
        yc = _gdn_pallas(p, gdn_conv[l], gdn_a_log[l], gdn_dt_bias[l], gdn_norm[l])
        yd = _na_pallas(p, na_q_norm[l], na_k_norm[l], na_rpb[l])
        s = _outproj(s, mod_l, [ya, yb, yc, yd], w_out_bf, l)
        s = _ffn(s, mod_l, norm_ffn2[l], ffn2_in, ffn2_out, l, 6)
    return s[None, CTX_LEN:]
```

```python
import functools
import math

import jax
import jax.numpy as jnp
from jax import lax
from jax.experimental import pallas as pl
from jax.experimental.pallas import tpu as pltpu

D_MODEL = 2048
SEQ = 8192
DEPTH = 4
GRID_W = 64
CTX_LEN = 256
ROWS = CTX_LEN + SEQ
GROUP_WIDTH = 512
D_FF = 5632
N_MOD = 9
EPS = 1e-6

S5_CH = 16
S5_GROUPS = GROUP_WIDTH // S5_CH
S5_STATE = 64
DIFF_HEADS = 4
DIFF_HEAD_DIM = 64
ROPE_BASE = 10000.0
Q_BLOCK = 128
GDN_HEADS = 4
GDN_HEAD_DIM = 128
GDN_CONV = 5
GDN_CHUNK = 64
NA_HEADS = 8
NA_HEAD_DIM = 64
WIN_H = 8
WIN_W = 16
NA_KEY_COLS = 2 * WIN_W

P_S5 = 0
P_DIFF = 512
P_GDN_QKV = 2048
P_GDN_Z = 3584
P_NA = 4096
P_GDN_AB = 5632
P_WIDTH = 5760

VMEM_LIMIT_BYTES = 56 * 1024 * 1024

F32 = jnp.float32
BF16 = jnp.bfloat16


MOD_TN = 1024


def _mod_kernel(ct_ref, w_ref, b_ref, o_ref):
    c = ct_ref[...]
    s = c * jax.nn.sigmoid(c)
    v0 = jnp.broadcast_to(s[:, 0:1], (D_MODEL, 128))
    v1 = jnp.broadcast_to(s[:, 1:2], (D_MODEL, 128))
    for j in range(MOD_TN // 128):
        w = w_ref[:, j * 128:(j + 1) * 128]
        b = b_ref[:, j * 128:(j + 1) * 128]
        o_ref[0:1, j * 128:(j + 1) * 128] = jnp.sum(w * v0, axis=0, keepdims=True) + b
        o_ref[1:2, j * 128:(j + 1) * 128] = jnp.sum(w * v1, axis=0, keepdims=True) + b


def _modulation(c, c_ctx, w_ada, b_ada):
    n = N_MOD * D_MODEL
    ct = jnp.stack([c.reshape(D_MODEL), c_ctx.reshape(D_MODEL)], axis=1)
    out = pl.pallas_call(
        _mod_kernel,
        grid=(DEPTH, n // MOD_TN),
        in_specs=[
            pl.BlockSpec((D_MODEL, 2), lambda l, j: (0, 0)),
            pl.BlockSpec((None, D_MODEL, MOD_TN), lambda l, j: (l, 0, j)),
            pl.BlockSpec((None, 1, MOD_TN), lambda l, j: (l, 0, j)),
        ],
        out_specs=pl.BlockSpec((None, 2, MOD_TN), lambda l, j: (l, 0, j)),
        out_shape=jax.ShapeDtypeStruct((DEPTH, 2, n), F32),
        compiler_params=pltpu.CompilerParams(
            dimension_semantics=("arbitrary", "arbitrary"), vmem_limit_bytes=VMEM_LIMIT_BYTES),
        name="adaln_mod",
    )(ct, w_ada, b_ada.reshape(DEPTH, 1, n))
    return out.reshape(DEPTH, 2 * N_MOD, D_MODEL)


def _is_ctx_rows(tile_rows):
    rows = pl.program_id(0) * tile_rows + lax.broadcasted_iota(jnp.int32, (tile_rows, 1), 0)
    return rows < CTX_LEN


def _mod_row(mod_ref, is_ctx, k):
    return jnp.where(is_ctx, mod_ref[N_MOD + k:N_MOD + k + 1, :], mod_ref[k:k + 1, :])


NORM_ROWS = 16


def _mod_chunk_row(mod_ref, first_row, k):
    off = jnp.where(first_row < CTX_LEN, N_MOD, 0)
    return mod_ref[pl.ds(off + k, 1), :]


def _adaln_to_scratch(x_ref, gamma_ref, mod_ref, h_ref, tile_rows, base):
    row0 = pl.program_id(0) * tile_rows

    def body(r, carry):
        lo = pl.multiple_of(r * NORM_ROWS, NORM_ROWS)
        x = x_ref[pl.ds(lo, NORM_ROWS), :]
        ms = jnp.mean(x * x, axis=-1, keepdims=True)
        y = x * lax.rsqrt(ms + EPS) * gamma_ref[...]
        h = y * (1.0 + _mod_chunk_row(mod_ref, row0 + lo, base + 1)) + _mod_chunk_row(mod_ref, row0 + lo, base)
        h_ref[pl.ds(lo, NORM_ROWS), :] = h.astype(BF16)
        return carry

    lax.fori_loop(0, tile_rows // NORM_ROWS, body, 0, unroll=4)


FFN_TM = 1408
FFN_RC = 352
FFN_TF = 512


def _ffn_kernel(x_ref, mod_ref, gamma_ref, wg_ref, wu_ref, wo_ref, o_ref, h_ref, *, base):
    f = pl.program_id(1)
    last = pl.num_programs(1) - 1
    row0 = pl.program_id(0) * FFN_TM

    @pl.when(f == 0)
    def _():
        _adaln_to_scratch(x_ref, gamma_ref, mod_ref, h_ref, FFN_TM, base)
        o_ref[...] = jnp.zeros_like(o_ref)

    for r in range(FFN_TM // FFN_RC):
        rows = pl.ds(r * FFN_RC, FFN_RC)
        h = h_ref[rows, :]
        g = jnp.dot(h, wg_ref[...], preferred_element_type=F32)
        u = jnp.dot(h, wu_ref[...], preferred_element_type=F32)
        a = (g * jax.nn.sigmoid(g) * u).astype(BF16)
        o_ref[rows, :] += jnp.dot(a, wo_ref[...], preferred_element_type=F32)

    @pl.when(f == last)
    def _():
        def fin(r, carry):
            lo = pl.multiple_of(r * NORM_ROWS, NORM_ROWS)
            gate = _mod_chunk_row(mod_ref, row0 + lo, base + 2)
            sl = pl.ds(lo, NORM_ROWS)
            o_ref[sl, :] = x_ref[sl, :] + 0.5 * gate * o_ref[sl, :]
            return carry

        lax.fori_loop(0, FFN_TM // NORM_ROWS, fin, 0, unroll=4)


def _ffn(s, mod_l, gamma, w_in_bf, w_out_bf, layer, base):
    nf = D_FF // FFN_TF
    return pl.pallas_call(
        functools.partial(_ffn_kernel, base=base),
        grid=(ROWS // FFN_TM, nf),
        in_specs=[
            pl.BlockSpec((FFN_TM, D_MODEL), lambda i, f: (i, 0)),
            pl.BlockSpec((2 * N_MOD, D_MODEL), lambda i, f: (0, 0)),
            pl.BlockSpec((1, D_MODEL), lambda i, f: (0, 0)),
            pl.BlockSpec((None, D_MODEL, FFN_TF), lambda i, f: (layer, 0, f)),
            pl.BlockSpec((None, D_MODEL, FFN_TF), lambda i, f: (layer, 0, nf + f)),
            pl.BlockSpec((None, FFN_TF, D_MODEL), lambda i, f: (layer, f, 0)),
        ],
        out_specs=pl.BlockSpec((FFN_TM, D_MODEL), lambda i, f: (i, 0), pipeline_mode=pl.Buffered(1)),
        out_shape=jax.ShapeDtypeStruct((ROWS, D_MODEL), F32),
        scratch_shapes=[pltpu.VMEM((FFN_TM, D_MODEL), BF16)],
        compiler_params=pltpu.CompilerParams(
            dimension_semantics=("arbitrary", "arbitrary"), vmem_limit_bytes=VMEM_LIMIT_BYTES),
        name="ffn_swiglu",
    )(s, mod_l, gamma.reshape(1, D_MODEL), w_in_bf, w_in_bf, w_out_bf)


INP_TM = 1408
INP_RC = 352
INP_TN = 1152


def _inproj_kernel(x_ref, mod_ref, gamma_ref, w_ref, o_ref, h_ref):
    @pl.when(pl.program_id(1) == 0)
    def _():
        _adaln_to_scratch(x_ref, gamma_ref, mod_ref, h_ref, INP_TM, 3)

    for r in range(INP_TM // INP_RC):
        rows = pl.ds(r * INP_RC, INP_RC)
        o_ref[rows, :] = jnp.dot(h_ref[rows, :], w_ref[...], preferred_element_type=F32)


def _inproj(s, mod_l, gamma, w_bf, layer):
    return pl.pallas_call(
        _inproj_kernel,
        grid=(ROWS // INP_TM, P_WIDTH // INP_TN),
        in_specs=[
            pl.BlockSpec((INP_TM, D_MODEL), lambda i, n: (i, 0)),
            pl.BlockSpec((2 * N_MOD, D_MODEL), lambda i, n: (0, 0)),
            pl.BlockSpec((1, D_MODEL), lambda i, n: (0, 0)),
            pl.BlockSpec((None, D_MODEL, INP_TN), lambda i, n: (layer, 0, n)),
        ],
        out_specs=pl.BlockSpec((INP_TM, INP_TN), lambda i, n: (i, n)),
        out_shape=jax.ShapeDtypeStruct((ROWS, P_WIDTH), F32),
        scratch_shapes=[pltpu.VMEM((INP_TM, D_MODEL), BF16)],
        compiler_params=pltpu.CompilerParams(
            dimension_semantics=("arbitrary", "arbitrary"), vmem_limit_bytes=VMEM_LIMIT_BYTES),
        name="in_proj",
    )(s, mod_l, gamma.reshape(1, D_MODEL), w_bf)


OUT_TM = 384


def _outproj_kernel(x_ref, mod_ref, ya_ref, yb_ref, yc_ref, yd_ref, w_ref, o_ref):
    is_ctx = _is_ctx_rows(OUT_TM)
    acc = jnp.zeros((OUT_TM, D_MODEL), F32)
    for k, y_ref in enumerate((ya_ref, yb_ref, yc_ref, yd_ref)):
        acc += jnp.dot(y_ref[...].astype(BF16), w_ref[k * GROUP_WIDTH:(k + 1) * GROUP_WIDTH, :],
                       preferred_element_type=F32)
    o_ref[...] = x_ref[...] + _mod_row(mod_ref, is_ctx, 5) * acc


def _outproj(s, mod_l, ys, w_bf, layer):
    yspec = pl.BlockSpec((OUT_TM, GROUP_WIDTH), lambda i: (i, 0))
    return pl.pallas_call(
        _outproj_kernel,
        grid=(ROWS // OUT_TM,),
        in_specs=[
            pl.BlockSpec((OUT_TM, D_MODEL), lambda i: (i, 0)),
            pl.BlockSpec((2 * N_MOD, D_MODEL), lambda i: (0, 0)),
            yspec, yspec, yspec, yspec,
            pl.BlockSpec((None, D_MODEL, D_MODEL), lambda i: (layer, 0, 0)),
        ],
        out_specs=pl.BlockSpec((OUT_TM, D_MODEL), lambda i: (i, 0)),
        out_shape=jax.ShapeDtypeStruct((ROWS, D_MODEL), F32),
        compiler_params=pltpu.CompilerParams(
            dimension_semantics=("arbitrary",), vmem_limit_bytes=VMEM_LIMIT_BYTES),
        name="out_proj",
    )(s, mod_l, *ys, w_bf)


S5_TC = 256


def _s5_prepare(lam_re, lam_im, log_step, b_re, b_im, c_re, c_im):
    dt = jnp.exp(log_step)[..., None]
    mag = jnp.exp(lam_re * dt)
    ar = mag * jnp.cos(lam_im * dt)
    ai = mag * jnp.sin(lam_im * dt)
    den = lam_re * lam_re + lam_im * lam_im
    fr = ((ar - 1.0) * lam_re + ai * lam_im) / den
    fi = (ai * lam_re - (ar - 1.0) * lam_im) / den
    bbr = fr[..., None] * b_re - fi[..., None] * b_im
    bbi = fr[..., None] * b_im + fi[..., None] * b_re
    in_oct = jnp.eye(4, dtype=F32)[jnp.arange(S5_GROUPS) % 4]
    half = 4 * S5_STATE

    def w_in_half(bb):
        return jnp.einsum('dgpn,gj->dgnjp', bb, in_oct).reshape(2, GROUP_WIDTH, half)

    def w_out_half(cc):
        return jnp.einsum('dgcp,gj->djpgc', cc, in_oct).reshape(2, half, GROUP_WIDTH)

    w_in = jnp.concatenate([w_in_half(bbr), w_in_half(bbi)], axis=2)
    w_out = jnp.concatenate([w_out_half(c_re), w_out_half(-c_im)], axis=1)
    return (ar.reshape(2, 8, half), ai.reshape(2, 8, half), w_in.astype(BF16), w_out.astype(BF16))


def _s5_scan_kernel(uf_ref, ub_ref, win_ref, wout_ref, ar_ref, ai_ref, yf_ref, yb_ref,
                    buf_ref, bub_ref, h_ref):
    tc = S5_TC
    lane_blk = lambda b: slice(b * 128, (b + 1) * 128)

    @pl.when(pl.program_id(0) == 0)
    def _():
        h_ref[...] = jnp.zeros_like(h_ref)

    lo_half = lax.broadcasted_iota(jnp.int32, (tc, 128), 1) < 64
    octants = [(b, 2 * b + e, lo_half if e == 0 else jnp.logical_not(lo_half)) for b in range(4) for e in range(2)]

    def project_in(u_ref, d, bu_ref):
        for b, q, mask in octants:
            lhs = jnp.where(mask, u_ref[:, lane_blk(b)], 0.0).astype(BF16)
            res = jnp.dot(lhs, win_ref[d, lane_blk(b), :], preferred_element_type=F32)
            for j in range(4):
                bu_ref[j, pl.ds(q, tc, stride=8), :] = res[:, lane_blk(j)]

    project_in(uf_ref, 0, buf_ref)
    project_in(ub_ref, 1, bub_ref)

    arf, aif, arb, aib = ar_ref[0], ai_ref[0], ar_ref[1], ai_ref[1]

    def load(ref, rows):
        return (jnp.concatenate([ref[0, rows, :], ref[1, rows, :]], axis=1),
                jnp.concatenate([ref[2, rows, :], ref[3, rows, :]], axis=1))

    def store(ref, rows, re, im):
        ref[0, rows, :] = re[:, :128]
        ref[1, rows, :] = re[:, 128:]
        ref[2, rows, :] = im[:, :128]
        ref[3, rows, :] = im[:, 128:]

    def step(t, carry):
        hfr, hfi, hbr, hbi = carry
        rf = pl.ds(pl.multiple_of(t * 8, 8), 8)
        rb = pl.ds(pl.multiple_of((tc - 1 - t) * 8, 8), 8)
        bfr, bfi = load(buf_ref, rf)
        bbr, bbi = load(bub_ref, rb)
        nfr = arf * hfr - aif * hfi + bfr
        nfi = arf * hfi + aif * hfr + bfi
        nbr = arb * hbr - aib * hbi + bbr
        nbi = arb * hbi + aib * hbr + bbi
        store(buf_ref, rf, nfr, nfi)
        store(bub_ref, rb, nbr, nbi)
        return nfr, nfi, nbr, nbi

    carry = lax.fori_loop(0, tc, step, (h_ref[0], h_ref[1], h_ref[2], h_ref[3]), unroll=8)
    for i in range(4):
        h_ref[i] = carry[i]

    def project_out(bu_ref, d, y_ref):
        for b in range(4):
            res = []
            for q in (2 * b, 2 * b + 1):
                states = jnp.concatenate([bu_ref[j, pl.ds(q, tc, stride=8), :] for j in range(4)], axis=1)
                res.append(jnp.dot(states.astype(BF16), wout_ref[d, :, lane_blk(b)], preferred_element_type=F32))
            y_ref[:, lane_blk(b)] = jnp.where(lo_half, res[0], res[1])

    project_out(buf_ref, 0, yf_ref)
    project_out(bub_ref, 1, yb_ref)


def _s5_scan_call(p, a_r, a_i, w_in, w_out):
    rows = p.shape[0]
    n_chunks = rows // S5_TC
    bwd = lambda j: jnp.where(j == 0, 0, n_chunks - j)
    full = lambda shape: pl.BlockSpec(shape, lambda j: (0,) * len(shape))
    y = jax.ShapeDtypeStruct((rows, GROUP_WIDTH), F32)
    return pl.pallas_call(
        _s5_scan_kernel,
        grid=(n_chunks,),
        in_specs=[
            pl.BlockSpec((S5_TC, GROUP_WIDTH), lambda j: (j, 0)),
            pl.BlockSpec((S5_TC, GROUP_WIDTH), lambda j: (bwd(j), 0)),
            full((2, GROUP_WIDTH, GROUP_WIDTH)), full((2, GROUP_WIDTH, GROUP_WIDTH)),
            full((2, 8, 256)), full((2, 8, 256)),
        ],
        out_specs=[pl.BlockSpec((S5_TC, GROUP_WIDTH), lambda j: (j, 0)),
                   pl.BlockSpec((S5_TC, GROUP_WIDTH), lambda j: (bwd(j), 0))],
        out_shape=[y, y],
        scratch_shapes=[
            pltpu.VMEM((4, 8 * S5_TC, 128), F32),
            pltpu.VMEM((4, 8 * S5_TC, 128), F32),
            pltpu.VMEM((4, 8, 256), F32),
        ],
        compiler_params=pltpu.CompilerParams(
            dimension_semantics=("arbitrary",), vmem_limit_bytes=VMEM_LIMIT_BYTES),
        name="s5_scan",
    )(p, p, w_in, w_out, a_r, a_i)


S5_FIN_TM = 384


def _s5_finish_kernel(yf_ref, yb_ref, u_ref, d_ref, w_ref, b_ref, o_ref):
    y = yf_ref[...] + yb_ref[...] + d_ref[...] * u_ref[...]
    h = jnp.dot(jax.nn.gelu(y).astype(BF16), w_ref[...], preferred_element_type=F32) + b_ref[...]
    o_ref[...] = h[:, :GROUP_WIDTH] * jax.nn.sigmoid(h[:, GROUP_WIDTH:])


def _s5_finish(yf, yb, p, d_skip, w_glu_bf, b_glu):
    rows = p.shape[0]
    row_blk = pl.BlockSpec((S5_FIN_TM, GROUP_WIDTH), lambda i: (i, 0))
    return pl.pallas_call(
        _s5_finish_kernel,
        grid=(rows // S5_FIN_TM,),
        in_specs=[row_blk, row_blk, row_blk,
                  pl.BlockSpec((1, GROUP_WIDTH), lambda i: (0, 0)),
                  pl.BlockSpec((GROUP_WIDTH, 2 * GROUP_WIDTH), lambda i: (0, 0)),
                  pl.BlockSpec((1, 2 * GROUP_WIDTH), lambda i: (0, 0))],
        out_specs=row_blk,
        out_shape=jax.ShapeDtypeStruct((rows, GROUP_WIDTH), F32),
        compiler_params=pltpu.CompilerParams(
            dimension_semantics=("arbitrary",), vmem_limit_bytes=VMEM_LIMIT_BYTES),
        name="s5_finish",
    )(yf, yb, p, d_skip.reshape(1, GROUP_WIDTH), w_glu_bf, b_glu.reshape(1, 2 * GROUP_WIDTH))


def _s5_pallas(p, lam_re, lam_im, log_step, b_re, b_im, c_re, c_im, d_skip, w_glu, b_glu):
    a_r, a_i, w_in, w_out = _s5_prepare(lam_re, lam_im, log_step, b_re, b_im, c_re, c_im)
    yf, yb = _s5_scan_call(p, a_r, a_i, w_in, w_out)
    return _s5_finish(yf, yb, p, d_skip, w_glu.astype(BF16), b_glu)


PREP_TM = 256
PREP_RC = 128


def _segment_ones(width, seg):
    i = jnp.arange(width) // seg
    return (i[:, None] == i[None, :]).astype(BF16)


def _seg_sumsq(x, e):
    x2 = x * x
    hi = x2.astype(BF16)
    lo = (x2 - hi.astype(F32)).astype(BF16)
    return jnp.dot(hi, e, preferred_element_type=F32) + jnp.dot(lo, e, preferred_element_type=F32)


NA_QROWS = 4
NA_KROWS = NA_QROWS + WIN_H
NA_NQ = NA_QROWS * GRID_W
NA_NK = NA_KROWS * GRID_W
MASKED = -1e30


def _na_prep_kernel(q_ref, k_ref, v_ref, e_ref, qw_ref, kw_ref, qo_ref, ko_ref, vo_ref):
    def body(c, carry):
        sl = pl.ds(pl.multiple_of(c * PREP_RC, PREP_RC), PREP_RC)
        q = q_ref[sl, :]
        k = k_ref[sl, :]
        qn = q * lax.rsqrt(_seg_sumsq(q, e_ref[...]) * (1.0 / NA_HEAD_DIM) + EPS) * qw_ref[...]
        kn = k * lax.rsqrt(_seg_sumsq(k, e_ref[...]) * (1.0 / NA_HEAD_DIM) + EPS) * kw_ref[...]
        qo_ref[sl, :] = (qn * NA_HEAD_DIM ** -0.5).astype(BF16)
        ko_ref[sl, :] = kn.astype(BF16)
        vo_ref[sl, :] = v_ref[sl, :].astype(BF16)
        return carry

    lax.fori_loop(0, PREP_TM // PREP_RC, body, 0)


def _na_prep(p, q_norm, k_norm):
    rows = p.shape[0]
    assert rows % PREP_TM == 0
    c0 = P_NA // GROUP_WIDTH
    col = lambda j: pl.BlockSpec((PREP_TM, GROUP_WIDTH), lambda i: (i, c0 + j))
    vec = pl.BlockSpec((1, GROUP_WIDTH), lambda i: (0, 0))
    out = jax.ShapeDtypeStruct((rows, GROUP_WIDTH), BF16)
    blk = pl.BlockSpec((PREP_TM, GROUP_WIDTH), lambda i: (i, 0))
    return pl.pallas_call(
        _na_prep_kernel,
        grid=(rows // PREP_TM,),
        in_specs=[col(0), col(1), col(2),
                  pl.BlockSpec((GROUP_WIDTH, GROUP_WIDTH), lambda i: (0, 0)), vec, vec],
        out_specs=[blk, blk, blk],
        out_shape=[out, out, out],
        compiler_params=pltpu.CompilerParams(
            dimension_semantics=("arbitrary",), vmem_limit_bytes=VMEM_LIMIT_BYTES),
        name="na_prep",
    )(p, p, p, _segment_ones(GROUP_WIDTH, NA_HEAD_DIM),
      jnp.tile(q_norm, NA_HEADS).reshape(1, GROUP_WIDTH), jnp.tile(k_norm, NA_HEADS).reshape(1, GROUP_WIDTH))


def _na_bias_table(rpb, n_grid_rows):
    import numpy as np
    c = np.arange(GRID_W)
    col0 = np.clip(c - WIN_W // 2, 0, GRID_W - WIN_W)
    kc = np.arange(GRID_W)
    col_ok = (kc[None, :] >= col0[:, None]) & (kc[None, :] < col0[:, None] + WIN_W)
    dcol = kc[None, :] - c[:, None] + WIN_W - 1
    shift = ((dcol[None] == np.arange(2 * WIN_W - 1)[:, None, None]) & col_ok[None]).astype(np.float32)
    jr = np.arange(NA_QROWS)[:, None]
    kr = np.arange(NA_KROWS)[None, :]
    ri = np.stack([kr - jr + WIN_H - 1, kr - jr + WIN_H - 1 - WIN_H // 2, kr - jr + WIN_H - 1 - WIN_H])
    ok = np.stack([(kr < WIN_H) & (jr >= 0), (kr - jr >= 0) & (kr - jr < WIN_H), (kr >= NA_QROWS) & (jr >= 0)])
    sel = (ok[..., None] & (ri[..., None] == np.arange(2 * WIN_H - 1))).astype(np.float32)
    valid = ok[:, :, None, :, None] & col_ok[None, None, :, None, :]
    neg = np.where(valid, 0.0, MASKED).astype(np.float32).reshape(3, 1, NA_NQ, NA_NK)
    hi = lax.Precision.HIGHEST
    band = jnp.einsum('hrj,jck->hrck', rpb, jnp.asarray(shift), precision=hi)
    tab = jnp.einsum('vjkr,hrcx->vhjckx', jnp.asarray(sel), band, precision=hi)
    return tab.reshape(3, NA_HEADS, NA_NQ, NA_NK) + jnp.asarray(neg)


def _softmax_pv(scores, values):
    m = functools.reduce(jnp.maximum, [jnp.max(s, axis=1, keepdims=True) for s in scores])
    ps = [jnp.exp(s - m) for s in scores]
    denom = functools.reduce(lambda a, b: a + b, [jnp.sum(p, axis=1, keepdims=True) for p in ps])
    acc = functools.reduce(lambda a, b: a + b,
                           [jnp.dot(p.astype(BF16), v, preferred_element_type=F32) for p, v in zip(ps, values)])
    return acc / denom


def _dot_nt(a, b):
    return lax.dot_general(a, b, (((1,), (1,)), ((), ())), preferred_element_type=F32)


def _na_kernel(q_ref, k_ref, v_ref, tab_ref, o_ref, *, n_blocks):
    i = pl.program_id(1)
    lo_half = lax.broadcasted_iota(jnp.int32, (NA_NQ, 128), 1) < NA_HEAD_DIM
    q = q_ref[...]
    zero = jnp.zeros_like(q)
    q_heads = (jnp.where(lo_half, q, zero), jnp.where(lo_half, zero, q))
    kc = k_ref[0:CTX_LEN, :]
    vc = v_ref[0:CTX_LEN, :]

    @pl.when(i == 0)
    def _():
        outs = [_softmax_pv([_dot_nt(qh, kc)], [vc]) for qh in q_heads]
        o_ref[...] = jnp.where(lo_half, outs[0], outs[1])

    @pl.when(i > 0)
    def _():
        ib = i - 1
        kr0 = jnp.clip(NA_QROWS * ib - WIN_H // 2, 0, NA_QROWS * n_blocks - NA_KROWS)
        start = pl.multiple_of(CTX_LEN + kr0 * GRID_W, GRID_W)
        kwin = k_ref[pl.ds(start, NA_NK), :]
        vwin = v_ref[pl.ds(start, NA_NK), :]
        variant = jnp.where(ib == 0, 0, jnp.where(ib == n_blocks - 1, 2, 1))
        outs = []
        for e, qh in enumerate(q_heads):
            s_loc = _dot_nt(qh, kwin) + tab_ref[variant, e]
            outs.append(_softmax_pv([s_loc, _dot_nt(qh, kc)], [vwin, vc]))
        o_ref[...] = jnp.where(lo_half, outs[0], outs[1])


def _na_pallas(p, q_norm, k_norm, rpb):
    rows = p.shape[0]
    n_grid_rows = (rows - CTX_LEN) // GRID_W
    n_blocks = n_grid_rows // NA_QROWS
    q, k, v = _na_prep(p, q_norm, k_norm)
    tab = _na_bias_table(rpb, n_grid_rows).reshape(3, NA_HEADS // 2, 2, NA_NQ, NA_NK)
    kv = pl.BlockSpec((rows, 128), lambda h, i: (0, h))
    return pl.pallas_call(
        functools.partial(_na_kernel, n_blocks=n_blocks),
        grid=(NA_HEADS // 2, n_blocks + 1),
        in_specs=[pl.BlockSpec((NA_NQ, 128), lambda h, i: (i, h)), kv, kv,
                  pl.BlockSpec((3, None, 2, NA_NQ, NA_NK), lambda h, i: (0, h, 0, 0, 0))],
        out_specs=pl.BlockSpec((NA_NQ, 128), lambda h, i: (i, h)),
        out_shape=jax.ShapeDtypeStruct((rows, GROUP_WIDTH), F32),
        compiler_params=pltpu.CompilerParams(
            dimension_semantics=("arbitrary", "arbitrary"), vmem_limit_bytes=VMEM_LIMIT_BYTES),
        name="na_attention",
    )(q, k, v, tab)


DIFF_TQ = 256
DIFF_TK = 768
DIFF_HD = 2 * DIFF_HEAD_DIM
DIFF_HPS = 2


def _rope_tables_rows(rows):
    pos = jnp.arange(rows - CTX_LEN)
    row = (pos // GRID_W).astype(F32)
    col = (pos % GRID_W).astype(F32)
    n_freq = DIFF_HEAD_DIM // 4
    inv = ROPE_BASE ** (-jnp.arange(n_freq, dtype=F32) / n_freq)
    ang = jnp.concatenate([row[:, None] * inv, row[:, None] * inv, col[:, None] * inv, col[:, None] * inv], -1)
    sign = jnp.tile(jnp.repeat(jnp.array([-1.0, 1.0, -1.0, 1.0], F32), n_freq), 2)
    cos = jnp.concatenate([jnp.ones((CTX_LEN, DIFF_HD), F32), jnp.tile(jnp.cos(ang), (1, 2))], axis=0)
    sin = jnp.concatenate([jnp.zeros((CTX_LEN, DIFF_HD), F32), jnp.tile(jnp.sin(ang), (1, 2)) * sign], axis=0)
    return cos, sin


def _diff_prep_kernel(q_ref, k_ref, v_ref, e_ref, qw_ref, kw_ref, cos_ref, sin_ref, qo_ref, ko_ref, vo_ref):
    quarter = lax.broadcasted_iota(jnp.int32, (PREP_RC, GROUP_WIDTH), 1) // (DIFF_HEAD_DIM // 4)
    first_of_pair = quarter % 2 == 0

    def rope(x, cos, sin):
        partner = jnp.where(first_of_pair, pltpu.roll(x, GROUP_WIDTH - DIFF_HEAD_DIM // 4, 1),
                            pltpu.roll(x, DIFF_HEAD_DIM // 4, 1))
        return x * cos + partner * sin

    def body(c, carry):
        sl = pl.ds(pl.multiple_of(c * PREP_RC, PREP_RC), PREP_RC)
        cos = jnp.concatenate([cos_ref[sl, :]] * DIFF_HEADS, axis=1)
        sin = jnp.concatenate([sin_ref[sl, :]] * DIFF_HEADS, axis=1)
        q = q_ref[sl, :]
        k = k_ref[sl, :]
        qn = q * lax.rsqrt(_seg_sumsq(q, e_ref[...]) * (1.0 / DIFF_HEAD_DIM) + EPS) * qw_ref[...]
        kn = k * lax.rsqrt(_seg_sumsq(k, e_ref[...]) * (1.0 / DIFF_HEAD_DIM) + EPS) * kw_ref[...]
        qo_ref[sl, :] = (rope(qn, cos, sin) * DIFF_HEAD_DIM ** -0.5).astype(BF16)
        ko_ref[sl, :] = rope(kn, cos, sin).astype(BF16)
        v = v_ref[sl, :].astype(BF16)
        ones = jnp.ones((PREP_RC, DIFF_HD), BF16)
        for h in range(DIFF_HEADS):
            vo_ref[sl, 2 * h * DIFF_HD:(2 * h + 1) * DIFF_HD] = v[:, h * DIFF_HD:(h + 1) * DIFF_HD]
            vo_ref[sl, (2 * h + 1) * DIFF_HD:(2 * h + 2) * DIFF_HD] = ones
        return carry

    lax.fori_loop(0, PREP_TM // PREP_RC, body, 0)


def _diff_prep(p, q_norm, k_norm):
    rows = p.shape[0]
    assert rows % PREP_TM == 0
    c0 = P_DIFF // GROUP_WIDTH
    col = lambda j: pl.BlockSpec((PREP_TM, GROUP_WIDTH), lambda i: (i, c0 + j))
    vec = pl.BlockSpec((1, GROUP_WIDTH), lambda i: (0, 0))
    tab = pl.BlockSpec((PREP_TM, DIFF_HD), lambda i: (i, 0))
    blk = pl.BlockSpec((PREP_TM, GROUP_WIDTH), lambda i: (i, 0))
    cos, sin = _rope_tables_rows(rows)
    return pl.pallas_call(
        _diff_prep_kernel,
        grid=(rows // PREP_TM,),
        in_specs=[col(0), col(1), col(2),
                  pl.BlockSpec((GROUP_WIDTH, GROUP_WIDTH), lambda i: (0, 0)), vec, vec, tab, tab],
        out_specs=[blk, blk, pl.BlockSpec((PREP_TM, 2 * GROUP_WIDTH), lambda i: (i, 0))],
        out_shape=[jax.ShapeDtypeStruct((rows, GROUP_WIDTH), BF16), jax.ShapeDtypeStruct((rows, GROUP_WIDTH), BF16),
                   jax.ShapeDtypeStruct((rows, 2 * GROUP_WIDTH), BF16)],
        compiler_params=pltpu.CompilerParams(
            dimension_semantics=("arbitrary",), vmem_limit_bytes=VMEM_LIMIT_BYTES),
        name="diff_prep",
    )(p, p, p, _segment_ones(GROUP_WIDTH, DIFF_HEAD_DIM),
      jnp.tile(q_norm, 2 * DIFF_HEADS).reshape(1, GROUP_WIDTH),
      jnp.tile(k_norm, 2 * DIFF_HEADS).reshape(1, GROUP_WIDTH), cos, sin)


def _diff_kernel(q_ref, k_ref, v_ref, lv_ref, sw_ref, o_ref, m_ref, acc_ref, s_ref, *, n_kchunks):
    i = pl.program_id(1)
    heads = range(DIFF_HPS)
    lo_half = lax.broadcasted_iota(jnp.int32, (DIFF_TQ, DIFF_HD), 1) < DIFF_HEAD_DIM
    q2 = []
    for h in heads:
        q = q_ref[:, h * DIFF_HD:(h + 1) * DIFF_HD]
        zero = jnp.zeros_like(q)
        q2.append(jnp.concatenate([jnp.where(lo_half, q, zero), jnp.where(lo_half, zero, q)], axis=0))

    m_ref[...] = jnp.full(m_ref.shape, MASKED, F32)
    acc_ref[...] = jnp.zeros_like(acc_ref)

    def keys(c):
        return pl.ds(pl.multiple_of(c * DIFF_TK, DIFF_TK), DIFF_TK)

    def k_of(h, rows):
        return k_ref[rows, h * DIFF_HD:(h + 1) * DIFF_HD]

    def v_of(h, rows):
        return v_ref[rows, 2 * h * DIFF_HD:2 * (h + 1) * DIFF_HD]

    def scores(c, slot):
        for h in heads:
            s_ref[h, slot] = _dot_nt(q2[h], k_of(h, keys(c)))

    def accumulate(s, rows):
        m_old = [m_ref[h] for h in heads]
        m_new = [jnp.maximum(m_old[h], jnp.max(s[h], axis=1, keepdims=True)) for h in heads]
        p = [jnp.exp(s[h] - m_new[h][:, 0:1]).astype(BF16) for h in heads]
        pv = [jnp.dot(p[h], v_of(h, rows), preferred_element_type=F32) for h in heads]
        for h in heads:
            alpha = jnp.exp(m_old[h] - m_new[h])
            acc_ref[h] = jnp.concatenate([alpha, alpha], axis=1) * acc_ref[h] + pv[h]
            m_ref[h] = m_new[h]

    @pl.when(i == 0)
    def _():
        ctx = slice(0, CTX_LEN)
        accumulate([_dot_nt(q2[h], k_of(h, ctx)) for h in heads], ctx)

    @pl.when(i > 0)
    def _():
        scores(0, 0)

        def pair(t, carry):
            c = 2 * t
            scores(c + 1, 1)
            accumulate([s_ref[h, 0] for h in heads], keys(c))
            scores(c + 2, 0)
            accumulate([s_ref[h, 1] for h in heads], keys(c + 1))
            return carry

        lax.fori_loop(0, (n_kchunks - 1) // 2, pair, 0)
        if n_kchunks % 2 == 0:
            scores(n_kchunks - 1, 1)
            accumulate([s_ref[h, 0] for h in heads], keys(n_kchunks - 2))
            accumulate([s_ref[h, 1] for h in heads], keys(n_kchunks - 1))
        else:
            accumulate([s_ref[h, 0] for h in heads], keys(n_kchunks - 1))

    lam_init = lv_ref[4:5, 0:1]
    lam = (jnp.exp(jnp.sum(lv_ref[0:1, :] * lv_ref[1:2, :], axis=1, keepdims=True))
           - jnp.exp(jnp.sum(lv_ref[2:3, :] * lv_ref[3:4, :], axis=1, keepdims=True)) + lam_init)
    for h in heads:
        a1 = acc_ref[h, 0:DIFF_TQ, :]
        a2 = acc_ref[h, DIFF_TQ:, :]
        o = a1[:, :DIFF_HD] / a1[:, DIFF_HD:] - lam * (a2[:, :DIFF_HD] / a2[:, DIFF_HD:])
        y = o * lax.rsqrt(jnp.mean(o * o, axis=1, keepdims=True) + EPS) * sw_ref[...]
        o_ref[:, h * DIFF_HD:(h + 1) * DIFF_HD] = y * (1.0 - lam_init)


def _diff_pallas(p, q_norm, k_norm, lq1, lk1, lq2, lk2, subln, lam_init):
    rows = p.shape[0]
    assert rows % DIFF_TK == 0 and rows % DIFF_TQ == 0 and CTX_LEN == DIFF_TQ
    q, k, v = _diff_prep(p, q_norm, k_norm)
    pad = lambda t: jnp.pad(t, (0, DIFF_HD - DIFF_HEAD_DIM))
    lvec = jnp.stack([pad(lq1), pad(lk1), pad(lq2), pad(lk2), jnp.full((DIFF_HD,), lam_init, F32),
                      jnp.zeros((DIFF_HD,), F32), jnp.zeros((DIFF_HD,), F32), jnp.zeros((DIFF_HD,), F32)])
    return pl.pallas_call(
        functools.partial(_diff_kernel, n_kchunks=rows // DIFF_TK),
        grid=(DIFF_HEADS // DIFF_HPS, rows // DIFF_TQ),
        in_specs=[pl.BlockSpec((DIFF_TQ, DIFF_HPS * DIFF_HD), lambda h, i: (i, h)),
                  pl.BlockSpec((rows, DIFF_HPS * DIFF_HD), lambda h, i: (0, h)),
                  pl.BlockSpec((rows, 2 * DIFF_HPS * DIFF_HD), lambda h, i: (0, h)),
                  pl.BlockSpec((8, DIFF_HD), lambda h, i: (0, 0)),
                  pl.BlockSpec((1, DIFF_HD), lambda h, i: (0, 0))],
        out_specs=pl.BlockSpec((DIFF_TQ, DIFF_HPS * DIFF_HD), lambda h, i: (i, h)),
        out_shape=jax.ShapeDtypeStruct((rows, GROUP_WIDTH), F32),
        scratch_shapes=[pltpu.VMEM((DIFF_HPS, 2 * DIFF_TQ, DIFF_HD), F32),
                        pltpu.VMEM((DIFF_HPS, 2 * DIFF_TQ, 2 * DIFF_HD), F32),
                        pltpu.VMEM((DIFF_HPS, 2, 2 * DIFF_TQ, DIFF_TK), F32)],
        compiler_params=pltpu.CompilerParams(
            dimension_semantics=("arbitrary", "arbitrary"), vmem_limit_bytes=VMEM_LIMIT_BYTES),
        name="diff_attention",
    )(q, k, v, lvec, subln.reshape(1, DIFF_HD))


GDN_TB = 256
GDN_HALO = 8
GDN_GATES = 2 * GDN_HEADS


def _softplus(x):
    return jnp.maximum(x, 0.0) + jnp.log(1.0 + jnp.exp(-jnp.abs(x)))


def _gdn_prep_kernel(*refs):
    (qp, qc, qn, kp, kc, kn, vp, vc, vn, ab_ref, w_ref, alog_ref, dtb_ref,
     qo_ref, ko_ref, vo_ref, go_ref, pad_ref) = refs
    i = pl.program_id(0)
    last = pl.num_programs(0) - 1
    prev_ok = (i >= 2).astype(F32)
    next_ok = jnp.logical_and(i >= 1, i < last).astype(F32)
    half = GDN_CONV // 2

    def conv_silu(prev_ref, cur_ref, next_ref, sec):
        pad_ref[0:GDN_HALO, :] = prev_ref[...] * prev_ok
        pad_ref[GDN_HALO:GDN_HALO + GDN_TB, :] = cur_ref[...]
        pad_ref[GDN_HALO + GDN_TB:, :] = next_ref[...] * next_ok
        acc = jnp.zeros((GDN_TB, GROUP_WIDTH), F32)
        for j in range(GDN_CONV):
            w = w_ref[j:j + 1, sec * GROUP_WIDTH:(sec + 1) * GROUP_WIDTH]
            acc = acc + pad_ref[GDN_HALO - half + j:GDN_HALO - half + j + GDN_TB, :] * w
        return acc * jax.nn.sigmoid(acc)

    def l2n(x):
        parts = []
        for h in range(GDN_HEADS):
            xh = x[:, h * GDN_HEAD_DIM:(h + 1) * GDN_HEAD_DIM]
            parts.append(xh * lax.rsqrt(jnp.sum(xh * xh, axis=1, keepdims=True) + EPS))
        return jnp.concatenate(parts, axis=1)

    qo_ref[...] = l2n(conv_silu(qp, qc, qn, 0)) * GDN_HEAD_DIM ** -0.5
    ko_ref[...] = l2n(conv_silu(kp, kc, kn, 1))
    vo_ref[...] = conv_silu(vp, vc, vn, 2)
    x = ab_ref[...]
    lane = lax.broadcasted_iota(jnp.int32, x.shape, 1)
    go_ref[...] = jnp.where(lane < GDN_GATES, -jnp.exp(alog_ref[...]) * _softplus(x + dtb_ref[...]),
                            jax.nn.sigmoid(x))


def _gdn_prep(p, conv_w, a_log, dt_bias):
    rows = p.shape[0]
    assert rows % GDN_TB == 0 and CTX_LEN == GDN_TB
    n_halo = rows // GDN_HALO
    per = GDN_TB // GDN_HALO
    c0 = P_GDN_QKV // GROUP_WIDTH
    specs = []
    for sec in range(3):
        specs += [
            pl.BlockSpec((GDN_HALO, GROUP_WIDTH), lambda i, sec=sec: (jnp.maximum(i * per - 1, 0), c0 + sec)),
            pl.BlockSpec((GDN_TB, GROUP_WIDTH), lambda i, sec=sec: (i, c0 + sec)),
            pl.BlockSpec((GDN_HALO, GROUP_WIDTH),
                         lambda i, sec=sec: (jnp.minimum((i + 1) * per, n_halo - 1), c0 + sec)),
        ]
    vec = pl.BlockSpec((1, 128), lambda i: (0, 0))
    specs += [pl.BlockSpec((GDN_TB, 128), lambda i: (i, P_GDN_AB // 128)),
              pl.BlockSpec((8, 3 * GROUP_WIDTH), lambda i: (0, 0)), vec, vec]
    blk = pl.BlockSpec((GDN_TB, GROUP_WIDTH), lambda i: (i, 0))
    out = jax.ShapeDtypeStruct((rows, GROUP_WIDTH), F32)
    pad8 = lambda t: jnp.pad(t.reshape(1, GDN_GATES), ((0, 0), (0, 128 - GDN_GATES)))
    return pl.pallas_call(
        _gdn_prep_kernel,
        grid=(rows // GDN_TB,),
        in_specs=specs,
        out_specs=[blk, blk, blk, pl.BlockSpec((GDN_TB, 128), lambda i: (i, 0))],
        out_shape=[out, out, out, jax.ShapeDtypeStruct((rows, 128), F32)],
        scratch_shapes=[pltpu.VMEM((GDN_TB + 2 * GDN_HALO, GROUP_WIDTH), F32)],
        compiler_params=pltpu.CompilerParams(
            dimension_semantics=("arbitrary",), vmem_limit_bytes=VMEM_LIMIT_BYTES),
        name="gdn_prep",
    )(*([p] * 10), jnp.pad(conv_w, ((0, 8 - GDN_CONV), (0, 0))), pad8(a_log), pad8(dt_bias))


def _mm3(a, b):
    ah = a.astype(BF16)
    al = (a - ah.astype(F32)).astype(BF16)
    bh = b.astype(BF16)
    bl = (b - bh.astype(F32)).astype(BF16)
    dot = lambda x, y: jnp.dot(x, y, preferred_element_type=F32)
    return dot(ah, bh) + dot(ah, bl) + dot(al, bh)


def _gdn_kernel(q_ref, k_ref, v_ref, g_ref, o_ref, s_ref, *, reverse):
    j = pl.program_id(0)
    tb, ck, hd = GDN_TB, GDN_CHUNK, GDN_HEAD_DIM
    d = 1 if reverse else 0

    @pl.when(j == 0)
    def _():
        s_ref[...] = jnp.zeros_like(s_ref)

    ri = lax.broadcasted_iota(jnp.int32, (tb, tb), 0)
    ci = lax.broadcasted_iota(jnp.int32, (tb, tb), 1)
    same = lambda n: (ri // n) == (ci // n)
    same_chunk = same(ck)
    before = (ci > ri) if reverse else (ci < ri)
    strict = jnp.logical_and(same_chunk, before)
    incl = jnp.logical_and(same_chunk, jnp.logical_or(before, ri == ci))
    eye = (ri == ci).astype(F32)

    g = g_ref[...]
    g1 = g.astype(BF16)
    r1 = g - g1.astype(F32)
    g2 = r1.astype(BF16)
    g3 = (r1 - g2.astype(F32)).astype(BF16)
    dot = lambda x, y: jnp.dot(x, y, preferred_element_type=F32)
    cum_m = incl.astype(BF16)
    tot_m = same_chunk.astype(BF16)
    gcum = dot(cum_m, g1) + dot(cum_m, g2) + dot(cum_m, g3)
    gtot = dot(tot_m, g1) + dot(tot_m, g2) + dot(tot_m, g3)
    gcum_t = gcum.T

    order = range(tb // ck - 1, -1, -1) if reverse else range(tb // ck)
    heads = range(GDN_HEADS)
    hs = lambda h: slice(h * hd, (h + 1) * hd)
    lanes = [d * GDN_HEADS + h for h in heads]
    gcol = [gcum[:, l:l + 1] for l in lanes]
    glast = [gtot[:, l:l + 1] for l in lanes]
    beta = [g[:, GDN_GATES + l:GDN_GATES + l + 1] for l in lanes]
    decay = [jnp.where(incl, jnp.exp(jnp.where(incl, gcol[h] - gcum_t[lanes[h]:lanes[h] + 1, :], 0.0)), 0.0)
             for h in heads]
    kb = [k_ref[:, hs(h)] * beta[h] for h in heads]
    k_b = [k_ref[:, hs(h)].astype(BF16) for h in heads]
    a_mat = [jnp.where(strict, _dot_nt(kb[h].astype(BF16), k_b[h]) * decay[h], 0.0) for h in heads]
    attn = [(_dot_nt(q_ref[:, hs(h)].astype(BF16), k_b[h]) * decay[h]).astype(BF16) for h in heads]

    x = [eye - jnp.where(same(2), a_mat[h], 0.0) for h in heads]
    for half_blk in (2, 4, 8, 16, 32):
        level = jnp.logical_and(same(2 * half_blk), jnp.logical_not(same(half_blk)))
        x_b = [x[h].astype(BF16) for h in heads]
        xl = [dot(x_b[h], jnp.where(level, a_mat[h], 0.0).astype(BF16)).astype(BF16) for h in heads]
        x = [x[h] - dot(xl[h], x_b[h]) for h in heads]
    resid = [eye - x[h] - _mm3(a_mat[h], x[h]) for h in heads]
    t_mat = [(x[h] + dot(x[h].astype(BF16), resid[h].astype(BF16))).astype(BF16) for h in heads]

    eg = [jnp.exp(gcol[h]) for h in heads]
    wu = [dot(t_mat[h], jnp.concatenate([kb[h] * eg[h], v_ref[:, hs(h)] * beta[h]], axis=1).astype(BF16))
          for h in heads]
    w_b = [wu[h][:, :hd].astype(BF16) for h in heads]
    qd_b = [(q_ref[:, hs(h)] * eg[h]).astype(BF16) for h in heads]
    k_carry = [(k_ref[:, hs(h)] * jnp.exp(glast[h] - gcol[h])).astype(BF16) for h in heads]

    s = [s_ref[h] for h in heads]
    v_new = [[None] * (tb // ck) for _ in heads]
    o_inter = [[None] * (tb // ck) for _ in heads]
    for c in order:
        rows = slice(c * ck, (c + 1) * ck)
        for h in heads:
            s_b = s[h].astype(BF16)
            v_new[h][c] = wu[h][rows, hd:] - dot(w_b[h][rows], s_b)
            o_inter[h][c] = dot(qd_b[h][rows], s_b)
            s[h] = s[h] * jnp.exp(glast[h][c * ck:c * ck + 1, :]) + lax.dot_general(
                k_carry[h][rows], v_new[h][c].astype(BF16), (((0,), (0,)), ((), ())),
                preferred_element_type=F32)
    for h in heads:
        s_ref[h] = s[h]
        v_all = jnp.concatenate(v_new[h], axis=0).astype(BF16)
        o_ref[:, hs(h)] = jnp.concatenate(o_inter[h], axis=0) + dot(attn[h], v_all)


def _gdn_scan(q, k, v, gates, reverse):
    rows = q.shape[0]
    n_blocks = rows // GDN_TB
    blk_of = (lambda j: jnp.where(j == 0, 0, n_blocks - j)) if reverse else (lambda j: j)
    blk = pl.BlockSpec((GDN_TB, GROUP_WIDTH), lambda j: (blk_of(j), 0))
    return pl.pallas_call(
        functools.partial(_gdn_kernel, reverse=reverse),
        grid=(n_blocks,),
        in_specs=[blk, blk, blk, pl.BlockSpec((GDN_TB, 128), lambda j: (blk_of(j), 0))],
        out_specs=blk,
        out_shape=jax.ShapeDtypeStruct((rows, GROUP_WIDTH), F32),
        scratch_shapes=[pltpu.VMEM((GDN_HEADS, GDN_HEAD_DIM, GDN_HEAD_DIM), F32)],
        compiler_params=pltpu.CompilerParams(
            dimension_semantics=("arbitrary",), vmem_limit_bytes=VMEM_LIMIT_BYTES),
        name="gdn_scan_bwd" if reverse else "gdn_scan_fwd",
    )(q, k, v, gates)


def _gdn_finish_kernel(of_ref, ob_ref, z_ref, w_ref, y_ref):
    o = of_ref[...] + ob_ref[...]
    z = z_ref[...]
    parts = []
    for h in range(GDN_HEADS):
        oh = o[:, h * GDN_HEAD_DIM:(h + 1) * GDN_HEAD_DIM]
        parts.append(oh * lax.rsqrt(jnp.mean(oh * oh, axis=1, keepdims=True) + EPS) * w_ref[...])
    y_ref[...] = jnp.concatenate(parts, axis=1) * (z * jax.nn.sigmoid(z))


def _gdn_pallas(p, conv_w, a_log, dt_bias, norm_w):
    rows = p.shape[0]
    q, k, v, gates = _gdn_prep(p, conv_w, a_log, dt_bias)
    o_f = _gdn_scan(q, k, v, gates, False)
    o_b = _gdn_scan(q, k, v, gates, True)
    blk = pl.BlockSpec((GDN_TB, GROUP_WIDTH), lambda i: (i, 0))
    return pl.pallas_call(
        _gdn_finish_kernel,
        grid=(rows // GDN_TB,),
        in_specs=[blk, blk, pl.BlockSpec((GDN_TB, GROUP_WIDTH), lambda i: (i, P_GDN_Z // GROUP_WIDTH)),
                  pl.BlockSpec((1, GDN_HEAD_DIM), lambda i: (0, 0))],
        out_specs=blk,
        out_shape=jax.ShapeDtypeStruct((rows, GROUP_WIDTH), F32),
        compiler_params=pltpu.CompilerParams(
            dimension_semantics=("arbitrary",), vmem_limit_bytes=VMEM_LIMIT_BYTES),
        name="gdn_finish",
    )(o_f, o_b, p, norm_w.reshape(1, GDN_HEAD_DIM))


def _reorder_w_in(w):
    front = w[..., 0:4096].astype(BF16)
    gdn_ab = w[..., 4096:4112].astype(BF16)
    na = w[..., 4112:5648].astype(BF16)
    pad = jnp.zeros(w.shape[:-1] + (P_WIDTH - P_GDN_AB - 16,), BF16)
    return jnp.concatenate([front, na, gdn_ab, pad], axis=-1)


def kernel(x, c, ctx, c_ctx, w_ada, b_ada, norm_ffn1, norm_mix, norm_ffn2, ffn1_w_in, ffn1_w_out,
           ffn2_w_in, ffn2_w_out, w_in, w_out, s5_lambda_re, s5_lambda_im, s5_log_step, s5_b_re,
           s5_b_im, s5_c_re, s5_c_im, s5_d, s5_w_glu, s5_b_glu, diff_q_norm, diff_k_norm,
           diff_lambda_q1, diff_lambda_k1, diff_lambda_q2, diff_lambda_k2, diff_subln, gdn_conv,
           gdn_a_log, gdn_dt_bias, gdn_norm, na_q_norm, na_k_norm, na_rpb):
    mod = _modulation(c, c_ctx, w_ada, b_ada)
    s = jnp.concatenate([ctx[0], x[0]], axis=0)
    ffn1_in, ffn1_out = ffn1_w_in.astype(BF16), ffn1_w_out.astype(BF16)
    ffn2_in, ffn2_out = ffn2_w_in.astype(BF16), ffn2_w_out.astype(BF16)
    w_in_bf, w_out_bf = _reorder_w_in(w_in), w_out.astype(BF16)
    for l in range(DEPTH):
        mod_l = mod[l]
        s = _ffn(s, mod_l, norm_ffn1[l], ffn1_in, ffn1_out, l, 0)
        p = _inproj(s, mod_l, norm_mix[l], w_in_bf, l)
        lam_init = 0.8 - 0.6 * math.exp(-0.3 * l)
        ya = _s5_pallas(p, s5_lambda_re[l], s5_lambda_im[l], s5_log_step[l], s5_b_re[l], s5_b_im[l],
                        s5_c_re[l], s5_c_im[l], s5_d[l], s5_w_glu[l], s5_b_glu[l])
        yb = _diff_pallas(p, diff_q_norm[l], diff_k_norm[l], diff_lambda_q1[l], diff_lambda_k1[l],
                          diff_lambda_q2[l], diff_lambda_k2[l], diff_subln[l], lam_init)
        yc = _gdn_pallas(p, gdn_conv[l], gdn_a_log[l], gdn_dt_bias[l], gdn_norm[l])
        yd = _na_pallas(p, na_q_norm[l], na_k_norm[l], na_rpb[l])
        s = _outproj(s, mod_l, [ya, yb, yc, yd], w_out_bf, l)
        s = _ffn(s, mod_l, norm_ffn2[l], ffn2_in, ffn2_out, l, 6)
    return s[None, CTX_LEN:]
```

```python
import functools
import math

import jax
import jax.numpy as jnp
from jax import lax
from jax.experimental import pallas as pl
from jax.experimental.pallas import tpu as pltpu

D_MODEL = 2048
SEQ = 8192
DEPTH = 4
GRID_W = 64
CTX_LEN = 256
ROWS = CTX_LEN + SEQ
GROUP_WIDTH = 512
D_FF = 5632
N_MOD = 9
EPS = 1e-6

S5_CH = 16
S5_GROUPS = GROUP_WIDTH // S5_CH
S5_STATE = 64
DIFF_HEADS = 4
DIFF_HEAD_DIM = 64
ROPE_BASE = 10000.0
Q_BLOCK = 128
GDN_HEADS = 4
GDN_HEAD_DIM = 128
GDN_CONV = 5
GDN_CHUNK = 64
NA_HEADS = 8
NA_HEAD_DIM = 64
WIN_H = 8
WIN_W = 16
NA_KEY_COLS = 2 * WIN_W

P_S5 = 0
P_DIFF = 512
P_GDN_QKV = 2048
P_GDN_Z = 3584
P_NA = 4096
P_GDN_AB = 5632
P_WIDTH = 5760

VMEM_LIMIT_BYTES = 56 * 1024 * 1024

F32 = jnp.float32
BF16 = jnp.bfloat16


MOD_TN = 1024


def _mod_kernel(ct_ref, w_ref, b_ref, o_ref):
    c = ct_ref[...]
    s = c * jax.nn.sigmoid(c)
    v0 = jnp.broadcast_to(s[:, 0:1], (D_MODEL, 128))
    v1 = jnp.broadcast_to(s[:, 1:2], (D_MODEL, 128))
    for j in range(MOD_TN // 128):
        w = w_ref[:, j * 128:(j + 1) * 128]
        b = b_ref[:, j * 128:(j + 1) * 128]
        o_ref[0:1, j * 128:(j + 1) * 128] = jnp.sum(w * v0, axis=0, keepdims=True) + b
        o_ref[1:2, j * 128:(j + 1) * 128] = jnp.sum(w * v1, axis=0, keepdims=True) + b


def _modulation(c, c_ctx, w_ada, b_ada):
    n = N_MOD * D_MODEL
    ct = jnp.stack([c.reshape(D_MODEL), c_ctx.reshape(D_MODEL)], axis=1)
    out = pl.pallas_call(
        _mod_kernel,
        grid=(DEPTH, n // MOD_TN),
        in_specs=[
            pl.BlockSpec((D_MODEL, 2), lambda l, j: (0, 0)),
            pl.BlockSpec((None, D_MODEL, MOD_TN), lambda l, j: (l, 0, j)),
            pl.BlockSpec((None, 1, MOD_TN), lambda l, j: (l, 0, j)),
        ],
        out_specs=pl.BlockSpec((None, 2, MOD_TN), lambda l, j: (l, 0, j)),
        out_shape=jax.ShapeDtypeStruct((DEPTH, 2, n), F32),
        compiler_params=pltpu.CompilerParams(
            dimension_semantics=("arbitrary", "arbitrary"), vmem_limit_bytes=VMEM_LIMIT_BYTES),
        name="adaln_mod",
    )(ct, w_ada, b_ada.reshape(DEPTH, 1, n))
    return out.reshape(DEPTH, 2 * N_MOD, D_MODEL)


def _is_ctx_rows(tile_rows):
    rows = pl.program_id(0) * tile_rows + lax.broadcasted_iota(jnp.int32, (tile_rows, 1), 0)
    return rows < CTX_LEN


def _mod_row(mod_ref, is_ctx, k):
    return jnp.where(is_ctx, mod_ref[N_MOD + k:N_MOD + k + 1, :], mod_ref[k:k + 1, :])


NORM_ROWS = 16
NORM_UNROLL = 8


def _mod_chunk_row(mod_ref, first_row, k):
    off = jnp.where(first_row < CTX_LEN, N_MOD, 0)
    return mod_ref[pl.ds(off + k, 1), :]


def _adaln_to_scratch(x_ref, gamma_ref, mod_ref, h_ref, gs_ref, tile_rows, base):
    row0 = pl.program_id(0) * tile_rows
    for seg in range(2):
        gain = gamma_ref[...] * (1.0 + mod_ref[seg * N_MOD + base + 1:seg * N_MOD + base + 2, :])
        gs_ref[seg] = jnp.broadcast_to(gain, (NORM_ROWS, D_MODEL))
        gs_ref[2 + seg] = jnp.broadcast_to(mod_ref[seg * N_MOD + base:seg * N_MOD + base + 1, :],
                                           (NORM_ROWS, D_MODEL))

    def body(r, carry):
        lo = pl.multiple_of(r * NORM_ROWS, NORM_ROWS)
        seg = (row0 + lo < CTX_LEN).astype(jnp.int32)
        x = x_ref[pl.ds(lo, NORM_ROWS), :]
        ms = jnp.mean(x * x, axis=-1, keepdims=True)
        h = x * lax.rsqrt(ms + EPS) * gs_ref[seg] + gs_ref[2 + seg]
        h_ref[pl.ds(lo, NORM_ROWS), :] = h.astype(BF16)
        return carry

    lax.fori_loop(0, tile_rows // NORM_ROWS, body, 0, unroll=NORM_UNROLL)


FFN_TM = 1408
FFN_RC = 352
FFN_TF = 512


def _ffn_kernel(x_ref, mod_ref, gamma_ref, wg_ref, wu_ref, wo_ref, o_ref, h_ref, gs_ref, *, base):
    f = pl.program_id(1)
    last = pl.num_programs(1) - 1
    row0 = pl.program_id(0) * FFN_TM

    @pl.when(f == 0)
    def _():
        _adaln_to_scratch(x_ref, gamma_ref, mod_ref, h_ref, gs_ref, FFN_TM, base)
        o_ref[...] = jnp.zeros_like(o_ref)

    for r in range(FFN_TM // FFN_RC):
        rows = pl.ds(r * FFN_RC, FFN_RC)
        h = h_ref[rows, :]
        g = jnp.dot(h, wg_ref[...], preferred_element_type=F32)
        u = jnp.dot(h, wu_ref[...], preferred_element_type=F32)
        a = (g * jax.nn.sigmoid(g) * u).astype(BF16)
        o_ref[rows, :] += jnp.dot(a, wo_ref[...], preferred_element_type=F32)

    @pl.when(f == last)
    def _():
        def fin(r, carry):
            lo = pl.multiple_of(r * NORM_ROWS, NORM_ROWS)
            gate = _mod_chunk_row(mod_ref, row0 + lo, base + 2)
            sl = pl.ds(lo, NORM_ROWS)
            o_ref[sl, :] = x_ref[sl, :] + 0.5 * gate * o_ref[sl, :]
            return carry

        lax.fori_loop(0, FFN_TM // NORM_ROWS, fin, 0, unroll=4)


def _ffn(s, mod_l, gamma, w_in_bf, w_out_bf, layer, base):
    nf = D_FF // FFN_TF
    return pl.pallas_call(
        functools.partial(_ffn_kernel, base=base),
        grid=(ROWS // FFN_TM, nf),
        in_specs=[
            pl.BlockSpec((FFN_TM, D_MODEL), lambda i, f: (i, 0)),
            pl.BlockSpec((2 * N_MOD, D_MODEL), lambda i, f: (0, 0)),
            pl.BlockSpec((1, D_MODEL), lambda i, f: (0, 0)),
            pl.BlockSpec((None, D_MODEL, FFN_TF), lambda i, f: (layer, 0, f)),
            pl.BlockSpec((None, D_MODEL, FFN_TF), lambda i, f: (layer, 0, nf + f)),
            pl.BlockSpec((None, FFN_TF, D_MODEL), lambda i, f: (layer, f, 0)),
        ],
        out_specs=pl.BlockSpec((FFN_TM, D_MODEL), lambda i, f: (i, 0), pipeline_mode=pl.Buffered(1)),
        out_shape=jax.ShapeDtypeStruct((ROWS, D_MODEL), F32),
        scratch_shapes=[pltpu.VMEM((FFN_TM, D_MODEL), BF16), pltpu.VMEM((4, NORM_ROWS, D_MODEL), F32)],
        compiler_params=pltpu.CompilerParams(
            dimension_semantics=("arbitrary", "arbitrary"), vmem_limit_bytes=VMEM_LIMIT_BYTES),
        name="ffn_swiglu",
    )(s, mod_l, gamma.reshape(1, D_MODEL), w_in_bf, w_in_bf, w_out_bf)


INP_TM = 1408
INP_RC = 352
INP_TN = 1152


def _inproj_kernel(x_ref, mod_ref, gamma_ref, w_ref, o_ref, h_ref, gs_ref):
    @pl.when(pl.program_id(1) == 0)
    def _():
        _adaln_to_scratch(x_ref, gamma_ref, mod_ref, h_ref, gs_ref, INP_TM, 3)

    for r in range(INP_TM // INP_RC):
        rows = pl.ds(r * INP_RC, INP_RC)
        o_ref[rows, :] = jnp.dot(h_ref[rows, :], w_ref[...], preferred_element_type=F32)


def _inproj(s, mod_l, gamma, w_bf, layer):
    return pl.pallas_call(
        _inproj_kernel,
        grid=(ROWS // INP_TM, P_WIDTH // INP_TN),
        in_specs=[
            pl.BlockSpec((INP_TM, D_MODEL), lambda i, n: (i, 0)),
            pl.BlockSpec((2 * N_MOD, D_MODEL), lambda i, n: (0, 0)),
            pl.BlockSpec((1, D_MODEL), lambda i, n: (0, 0)),
            pl.BlockSpec((None, D_MODEL, INP_TN), lambda i, n: (layer, 0, n)),
        ],
        out_specs=pl.BlockSpec((INP_TM, INP_TN), lambda i, n: (i, n)),
        out_shape=jax.ShapeDtypeStruct((ROWS, P_WIDTH), F32),
        scratch_shapes=[pltpu.VMEM((INP_TM, D_MODEL), BF16), pltpu.VMEM((4, NORM_ROWS, D_MODEL), F32)],
        compiler_params=pltpu.CompilerParams(
            dimension_semantics=("arbitrary", "arbitrary"), vmem_limit_bytes=VMEM_LIMIT_BYTES),
        name="in_proj",
    )(s, mod_l, gamma.reshape(1, D_MODEL), w_bf)


OUT_TM = 384


def _outproj_kernel(x_ref, mod_ref, ya_ref, yb_ref, yc_ref, yd_ref, w_ref, o_ref):
    is_ctx = _is_ctx_rows(OUT_TM)
    acc = jnp.zeros((OUT_TM, D_MODEL), F32)
    for k, y_ref in enumerate((ya_ref, yb_ref, yc_ref, yd_ref)):
        acc += jnp.dot(y_ref[...].astype(BF16), w_ref[k * GROUP_WIDTH:(k + 1) * GROUP_WIDTH, :],
                       preferred_element_type=F32)
    o_ref[...] = x_ref[...] + _mod_row(mod_ref, is_ctx, 5) * acc


def _outproj(s, mod_l, ys, w_bf, layer):
    yspec = pl.BlockSpec((OUT_TM, GROUP_WIDTH), lambda i: (i, 0))
    return pl.pallas_call(
        _outproj_kernel,
        grid=(ROWS // OUT_TM,),
        in_specs=[
            pl.BlockSpec((OUT_TM, D_MODEL), lambda i: (i, 0)),
            pl.BlockSpec((2 * N_MOD, D_MODEL), lambda i: (0, 0)),
            yspec, yspec, yspec, yspec,
            pl.BlockSpec((None, D_MODEL, D_MODEL), lambda i: (layer, 0, 0)),
        ],
        out_specs=pl.BlockSpec((OUT_TM, D_MODEL), lambda i: (i, 0)),
        out_shape=jax.ShapeDtypeStruct((ROWS, D_MODEL), F32),
        compiler_params=pltpu.CompilerParams(
            dimension_semantics=("arbitrary",), vmem_limit_bytes=VMEM_LIMIT_BYTES),
        name="out_proj",
    )(s, mod_l, *ys, w_bf)


S5_TC = 256


def _s5_prepare(lam_re, lam_im, log_step, b_re, b_im, c_re, c_im):
    dt = jnp.exp(log_step)[..., None]
    mag = jnp.exp(lam_re * dt)
    ar = mag * jnp.cos(lam_im * dt)
    ai = mag * jnp.sin(lam_im * dt)
    den = lam_re * lam_re + lam_im * lam_im
    fr = ((ar - 1.0) * lam_re + ai * lam_im) / den
    fi = (ai * lam_re - (ar - 1.0) * lam_im) / den
    bbr = fr[..., None] * b_re - fi[..., None] * b_im
    bbi = fr[..., None] * b_im + fi[..., None] * b_re
    in_oct = jnp.eye(4, dtype=F32)[jnp.arange(S5_GROUPS) % 4]
    half = 4 * S5_STATE

    def w_in_half(bb):
        return jnp.einsum('dgpn,gj->dgnjp', bb, in_oct).reshape(2, GROUP_WIDTH, half)

    def w_out_half(cc):
        return jnp.einsum('dgcp,gj->djpgc', cc, in_oct).reshape(2, half, GROUP_WIDTH)

    w_in = jnp.concatenate([w_in_half(bbr), w_in_half(bbi)], axis=2)
    w_out = jnp.concatenate([w_out_half(c_re), w_out_half(-c_im)], axis=1)
    return (ar.reshape(2, 8, half), ai.reshape(2, 8, half), w_in.astype(BF16), w_out.astype(BF16))


def _s5_scan_kernel(uf_ref, ub_ref, win_ref, wout_ref, ar_ref, ai_ref, yf_ref, yb_ref,
                    buf_ref, bub_ref, h_ref):
    tc = S5_TC
    lane_blk = lambda b: slice(b * 128, (b + 1) * 128)

    @pl.when(pl.program_id(0) == 0)
    def _():
        h_ref[...] = jnp.zeros_like(h_ref)

    lo_half = lax.broadcasted_iota(jnp.int32, (tc, 128), 1) < 64
    octants = [(b, 2 * b + e, lo_half if e == 0 else jnp.logical_not(lo_half)) for b in range(4) for e in range(2)]

    def project_in(u_ref, d, bu_ref):
        for b, q, mask in octants:
            lhs = jnp.where(mask, u_ref[:, lane_blk(b)], 0.0).astype(BF16)
            res = jnp.dot(lhs, win_ref[d, lane_blk(b), :], preferred_element_type=F32)
            for j in range(4):
                bu_ref[j, pl.ds(q, tc, stride=8), :] = res[:, lane_blk(j)]

    project_in(uf_ref, 0, buf_ref)
    project_in(ub_ref, 1, bub_ref)

    arf, aif, arb, aib = ar_ref[0], ai_ref[0], ar_ref[1], ai_ref[1]

    def load(ref, rows):
        return (jnp.concatenate([ref[0, rows, :], ref[1, rows, :]], axis=1),
                jnp.concatenate([ref[2, rows, :], ref[3, rows, :]], axis=1))

    def store(ref, rows, re, im):
        ref[0, rows, :] = re[:, :128]
        ref[1, rows, :] = re[:, 128:]
        ref[2, rows, :] = im[:, :128]
        ref[3, rows, :] = im[:, 128:]

    def step(t, carry):
        hfr, hfi, hbr, hbi = carry
        rf = pl.ds(pl.multiple_of(t * 8, 8), 8)
        rb = pl.ds(pl.multiple_of((tc - 1 - t) * 8, 8), 8)
        bfr, bfi = load(buf_ref, rf)
        bbr, bbi = load(bub_ref, rb)
        nfr = arf * hfr - aif * hfi + bfr
        nfi = arf * hfi + aif * hfr + bfi
        nbr = arb * hbr - aib * hbi + bbr
        nbi = arb * hbi + aib * hbr + bbi
        store(buf_ref, rf, nfr, nfi)
        store(bub_ref, rb, nbr, nbi)
        return nfr, nfi, nbr, nbi

    carry = lax.fori_loop(0, tc, step, (h_ref[0], h_ref[1], h_ref[2], h_ref[3]), unroll=8)
    for i in range(4):
        h_ref[i] = carry[i]

    def project_out(bu_ref, d, y_ref):
        for b in range(4):
            res = []
            for q in (2 * b, 2 * b + 1):
                states = jnp.concatenate([bu_ref[j, pl.ds(q, tc, stride=8), :] for j in range(4)], axis=1)
                res.append(jnp.dot(states.astype(BF16), wout_ref[d, :, lane_blk(b)], preferred_element_type=F32))
            y_ref[:, lane_blk(b)] = jnp.where(lo_half, res[0], res[1])

    project_out(buf_ref, 0, yf_ref)
    project_out(bub_ref, 1, yb_ref)


def _s5_scan_call(p, a_r, a_i, w_in, w_out):
    rows = p.shape[0]
    n_chunks = rows // S5_TC
    bwd = lambda j: jnp.where(j == 0, 0, n_chunks - j)
    full = lambda shape: pl.BlockSpec(shape, lambda j: (0,) * len(shape))
    y = jax.ShapeDtypeStruct((rows, GROUP_WIDTH), F32)
    return pl.pallas_call(
        _s5_scan_kernel,
        grid=(n_chunks,),
        in_specs=[
            pl.BlockSpec((S5_TC, GROUP_WIDTH), lambda j: (j, 0)),
            pl.BlockSpec((S5_TC, GROUP_WIDTH), lambda j: (bwd(j), 0)),
            full((2, GROUP_WIDTH, GROUP_WIDTH)), full((2, GROUP_WIDTH, GROUP_WIDTH)),
            full((2, 8, 256)), full((2, 8, 256)),
        ],
        out_specs=[pl.BlockSpec((S5_TC, GROUP_WIDTH), lambda j: (j, 0)),
                   pl.BlockSpec((S5_TC, GROUP_WIDTH), lambda j: (bwd(j), 0))],
        out_shape=[y, y],
        scratch_shapes=[
            pltpu.VMEM((4, 8 * S5_TC, 128), F32),
            pltpu.VMEM((4, 8 * S5_TC, 128), F32),
            pltpu.VMEM((4, 8, 256), F32),
        ],
        compiler_params=pltpu.CompilerParams(
            dimension_semantics=("arbitrary",), vmem_limit_bytes=VMEM_LIMIT_BYTES),
        name="s5_scan",
    )(p, p, w_in, w_out, a_r, a_i)


S5_FIN_TM = 384


def _s5_finish_kernel(yf_ref, yb_ref, u_ref, d_ref, w_ref, b_ref, o_ref):
    y = yf_ref[...] + yb_ref[...] + d_ref[...] * u_ref[...]
    h = jnp.dot(jax.nn.gelu(y).astype(BF16), w_ref[...], preferred_element_type=F32) + b_ref[...]
    o_ref[...] = h[:, :GROUP_WIDTH] * jax.nn.sigmoid(h[:, GROUP_WIDTH:])


def _s5_finish(yf, yb, p, d_skip, w_glu_bf, b_glu):
    rows = p.shape[0]
    row_blk = pl.BlockSpec((S5_FIN_TM, GROUP_WIDTH), lambda i: (i, 0))
    return pl.pallas_call(
        _s5_finish_kernel,
        grid=(rows // S5_FIN_TM,),
        in_specs=[row_blk, row_blk, row_blk,
                  pl.BlockSpec((1, GROUP_WIDTH), lambda i: (0, 0)),
                  pl.BlockSpec((GROUP_WIDTH, 2 * GROUP_WIDTH), lambda i: (0, 0)),
                  pl.BlockSpec((1, 2 * GROUP_WIDTH), lambda i: (0, 0))],
        out_specs=row_blk,
        out_shape=jax.ShapeDtypeStruct((rows, GROUP_WIDTH), F32),
        compiler_params=pltpu.CompilerParams(
            dimension_semantics=("arbitrary",), vmem_limit_bytes=VMEM_LIMIT_BYTES),
        name="s5_finish",
    )(yf, yb, p, d_skip.reshape(1, GROUP_WIDTH), w_glu_bf, b_glu.reshape(1, 2 * GROUP_WIDTH))


def _s5_pallas(p, lam_re, lam_im, log_step, b_re, b_im, c_re, c_im, d_skip, w_glu, b_glu):
    a_r, a_i, w_in, w_out = _s5_prepare(lam_re, lam_im, log_step, b_re, b_im, c_re, c_im)
    yf, yb = _s5_scan_call(p, a_r, a_i, w_in, w_out)
    return _s5_finish(yf, yb, p, d_skip, w_glu.astype(BF16), b_glu)


PREP_TM = 256
PREP_RC = 128


def _segment_ones(width, seg):
    i = jnp.arange(width) // seg
    return (i[:, None] == i[None, :]).astype(BF16)


def _seg_sumsq(x, e):
    x2 = x * x
    hi = x2.astype(BF16)
    lo = (x2 - hi.astype(F32)).astype(BF16)
    return jnp.dot(hi, e, preferred_element_type=F32) + jnp.dot(lo, e, preferred_element_type=F32)


NA_QROWS = 4
NA_KROWS = NA_QROWS + WIN_H
NA_NQ = NA_QROWS * GRID_W
NA_NK = NA_KROWS * GRID_W
MASKED = -1e30


def _na_prep_kernel(q_ref, k_ref, v_ref, e_ref, qw_ref, kw_ref, qo_ref, ko_ref, vo_ref):
    def body(c, carry):
        sl = pl.ds(pl.multiple_of(c * PREP_RC, PREP_RC), PREP_RC)
        q = q_ref[sl, :]
        k = k_ref[sl, :]
        qn = q * lax.rsqrt(_seg_sumsq(q, e_ref[...]) * (1.0 / NA_HEAD_DIM) + EPS) * qw_ref[...]
        kn = k * lax.rsqrt(_seg_sumsq(k, e_ref[...]) * (1.0 / NA_HEAD_DIM) + EPS) * kw_ref[...]
        qo_ref[sl, :] = (qn * NA_HEAD_DIM ** -0.5).astype(BF16)
        ko_ref[sl, :] = kn.astype(BF16)
        vo_ref[sl, :] = v_ref[sl, :].astype(BF16)
        return carry

    lax.fori_loop(0, PREP_TM // PREP_RC, body, 0)


def _na_prep(p, q_norm, k_norm):
    rows = p.shape[0]
    assert rows % PREP_TM == 0
    c0 = P_NA // GROUP_WIDTH
    col = lambda j: pl.BlockSpec((PREP_TM, GROUP_WIDTH), lambda i: (i, c0 + j))
    vec = pl.BlockSpec((1, GROUP_WIDTH), lambda i: (0, 0))
    out = jax.ShapeDtypeStruct((rows, GROUP_WIDTH), BF16)
    blk = pl.BlockSpec((PREP_TM, GROUP_WIDTH), lambda i: (i, 0))
    return pl.pallas_call(
        _na_prep_kernel,
        grid=(rows // PREP_TM,),
        in_specs=[col(0), col(1), col(2),
                  pl.BlockSpec((GROUP_WIDTH, GROUP_WIDTH), lambda i: (0, 0)), vec, vec],
        out_specs=[blk, blk, blk],
        out_shape=[out, out, out],
        compiler_params=pltpu.CompilerParams(
            dimension_semantics=("arbitrary",), vmem_limit_bytes=VMEM_LIMIT_BYTES),
        name="na_prep",
    )(p, p, p, _segment_ones(GROUP_WIDTH, NA_HEAD_DIM),
      jnp.tile(q_norm, NA_HEADS).reshape(1, GROUP_WIDTH), jnp.tile(k_norm, NA_HEADS).reshape(1, GROUP_WIDTH))


def _na_bias_table(rpb, n_grid_rows):
    import numpy as np
    c = np.arange(GRID_W)
    col0 = np.clip(c - WIN_W // 2, 0, GRID_W - WIN_W)
    kc = np.arange(GRID_W)
    col_ok = (kc[None, :] >= col0[:, None]) & (kc[None, :] < col0[:, None] + WIN_W)
    dcol = kc[None, :] - c[:, None] + WIN_W - 1
    shift = ((dcol[None] == np.arange(2 * WIN_W - 1)[:, None, None]) & col_ok[None]).astype(np.float32)
    jr = np.arange(NA_QROWS)[:, None]
    kr = np.arange(NA_KROWS)[None, :]
    ri = np.stack([kr - jr + WIN_H - 1, kr - jr + WIN_H - 1 - WIN_H // 2, kr - jr + WIN_H - 1 - WIN_H])
    ok = np.stack([(kr < WIN_H) & (jr >= 0), (kr - jr >= 0) & (kr - jr < WIN_H), (kr >= NA_QROWS) & (jr >= 0)])
    valid = ok[:, :, None, :, None] & col_ok[None, None, :, None, :]
    neg = np.where(valid, 0.0, MASKED).astype(np.float32).reshape(3, 1, NA_NQ, NA_NK)
    hi = lax.Precision.HIGHEST
    band = jnp.einsum('hrj,jck->hrck', rpb, jnp.asarray(shift), precision=hi)
    zero = jnp.zeros_like(band[:, :1])
    pairs = jnp.concatenate([jnp.concatenate([zero, band], axis=1), jnp.concatenate([band, zero], axis=1)],
                            axis=-1)
    sel = (ri[:, :, 0::2, None] + 1 == np.arange(2 * WIN_H)).astype(np.float32)
    tab = jnp.einsum('vjmr,hrcy->vhjcmy', jnp.asarray(sel), pairs, precision=hi)
    return tab.reshape(3, NA_HEADS, NA_NQ, NA_NK) + jnp.asarray(neg)


def _softmax_pv(scores, values):
    m = functools.reduce(jnp.maximum, [jnp.max(s, axis=1, keepdims=True) for s in scores])
    ps = [jnp.exp(s - m) for s in scores]
    denom = functools.reduce(lambda a, b: a + b, [jnp.sum(p, axis=1, keepdims=True) for p in ps])
    acc = functools.reduce(lambda a, b: a + b,
                           [jnp.dot(p.astype(BF16), v, preferred_element_type=F32) for p, v in zip(ps, values)])
    return acc / denom


def _dot_nt(a, b):
    return lax.dot_general(a, b, (((1,), (1,)), ((), ())), preferred_element_type=F32)


def _na_kernel(q_ref, k_ref, v_ref, tab_ref, o_ref, *, n_blocks):
    i = pl.program_id(1)
    lo_half = lax.broadcasted_iota(jnp.int32, (NA_NQ, 128), 1) < NA_HEAD_DIM
    q = q_ref[...]
    zero = jnp.zeros_like(q)
    q_heads = (jnp.where(lo_half, q, zero), jnp.where(lo_half, zero, q))
    kc = k_ref[0:CTX_LEN, :]
    vc = v_ref[0:CTX_LEN, :]

    @pl.when(i == 0)
    def _():
        outs = [_softmax_pv([_dot_nt(qh, kc)], [vc]) for qh in q_heads]
        o_ref[...] = jnp.where(lo_half, outs[0], outs[1])

    @pl.when(i > 0)
    def _():
        ib = i - 1
        kr0 = jnp.clip(NA_QROWS * ib - WIN_H // 2, 0, NA_QROWS * n_blocks - NA_KROWS)
        start = pl.multiple_of(CTX_LEN + kr0 * GRID_W, GRID_W)
        kwin = k_ref[pl.ds(start, NA_NK), :]
        vwin = v_ref[pl.ds(start, NA_NK), :]
        variant = jnp.where(ib == 0, 0, jnp.where(ib == n_blocks - 1, 2, 1))
        outs = []
        for e, qh in enumerate(q_heads):
            s_loc = _dot_nt(qh, kwin) + tab_ref[variant, e]
            outs.append(_softmax_pv([s_loc, _dot_nt(qh, kc)], [vwin, vc]))
        o_ref[...] = jnp.where(lo_half, outs[0], outs[1])


def _na_pallas(p, q_norm, k_norm, rpb):
    rows = p.shape[0]
    n_grid_rows = (rows - CTX_LEN) // GRID_W
    n_blocks = n_grid_rows // NA_QROWS
    q, k, v = _na_prep(p, q_norm, k_norm)
    tab = _na_bias_table(rpb, n_grid_rows).reshape(3, NA_HEADS // 2, 2, NA_NQ, NA_NK)
    kv = pl.BlockSpec((rows, 128), lambda h, i: (0, h))
    return pl.pallas_call(
        functools.partial(_na_kernel, n_blocks=n_blocks),
        grid=(NA_HEADS // 2, n_blocks + 1),
        in_specs=[pl.BlockSpec((NA_NQ, 128), lambda h, i: (i, h)), kv, kv,
                  pl.BlockSpec((3, None, 2, NA_NQ, NA_NK), lambda h, i: (0, h, 0, 0, 0))],
        out_specs=pl.BlockSpec((NA_NQ, 128), lambda h, i: (i, h)),
        out_shape=jax.ShapeDtypeStruct((rows, GROUP_WIDTH), F32),
        compiler_params=pltpu.CompilerParams(
            dimension_semantics=("arbitrary", "arbitrary"), vmem_limit_bytes=VMEM_LIMIT_BYTES),
        name="na_attention",
    )(q, k, v, tab)


DIFF_TQ = 256
DIFF_TK = 1408
DIFF_HD = 2 * DIFF_HEAD_DIM
DIFF_HPS = 2


def _rope_tables_rows(rows):
    pos = jnp.arange(rows - CTX_LEN)
    row = (pos // GRID_W).astype(F32)
    col = (pos % GRID_W).astype(F32)
    n_freq = DIFF_HEAD_DIM // 4
    inv = ROPE_BASE ** (-jnp.arange(n_freq, dtype=F32) / n_freq)
    ang = jnp.concatenate([row[:, None] * inv, row[:, None] * inv, col[:, None] * inv, col[:, None] * inv], -1)
    sign = jnp.tile(jnp.repeat(jnp.array([-1.0, 1.0, -1.0, 1.0], F32), n_freq), 2)
    cos = jnp.concatenate([jnp.ones((CTX_LEN, DIFF_HD), F32), jnp.tile(jnp.cos(ang), (1, 2))], axis=0)
    sin = jnp.concatenate([jnp.zeros((CTX_LEN, DIFF_HD), F32), jnp.tile(jnp.sin(ang), (1, 2)) * sign], axis=0)
    return cos, sin


def _diff_prep_kernel(q_ref, k_ref, v_ref, e_ref, qw_ref, kw_ref, cos_ref, sin_ref, qo_ref, ko_ref, vo_ref):
    quarter = lax.broadcasted_iota(jnp.int32, (PREP_RC, GROUP_WIDTH), 1) // (DIFF_HEAD_DIM // 4)
    first_of_pair = quarter % 2 == 0

    def rope(x, cos, sin):
        partner = jnp.where(first_of_pair, pltpu.roll(x, GROUP_WIDTH - DIFF_HEAD_DIM // 4, 1),
                            pltpu.roll(x, DIFF_HEAD_DIM // 4, 1))
        return x * cos + partner * sin

    def body(c, carry):
        sl = pl.ds(pl.multiple_of(c * PREP_RC, PREP_RC), PREP_RC)
        cos = jnp.concatenate([cos_ref[sl, :]] * DIFF_HEADS, axis=1)
        sin = jnp.concatenate([sin_ref[sl, :]] * DIFF_HEADS, axis=1)
        q = q_ref[sl, :]
        k = k_ref[sl, :]
        qn = q * lax.rsqrt(_seg_sumsq(q, e_ref[...]) * (1.0 / DIFF_HEAD_DIM) + EPS) * qw_ref[...]
        kn = k * lax.rsqrt(_seg_sumsq(k, e_ref[...]) * (1.0 / DIFF_HEAD_DIM) + EPS) * kw_ref[...]
        qo_ref[sl, :] = (rope(qn, cos, sin) * DIFF_HEAD_DIM ** -0.5).astype(BF16)
        ko_ref[sl, :] = rope(kn, cos, sin).astype(BF16)
        v = v_ref[sl, :].astype(BF16)
        ones = jnp.ones((PREP_RC, DIFF_HD), BF16)
        for h in range(DIFF_HEADS):
            vo_ref[sl, 2 * h * DIFF_HD:(2 * h + 1) * DIFF_HD] = v[:, h * DIFF_HD:(h + 1) * DIFF_HD]
            vo_ref[sl, (2 * h + 1) * DIFF_HD:(2 * h + 2) * DIFF_HD] = ones
        return carry

    lax.fori_loop(0, PREP_TM // PREP_RC, body, 0)


def _diff_prep(p, q_norm, k_norm):
    rows = p.shape[0]
    assert rows % PREP_TM == 0
    c0 = P_DIFF // GROUP_WIDTH
    col = lambda j: pl.BlockSpec((PREP_TM, GROUP_WIDTH), lambda i: (i, c0 + j))
    vec = pl.BlockSpec((1, GROUP_WIDTH), lambda i: (0, 0))
    tab = pl.BlockSpec((PREP_TM, DIFF_HD), lambda i: (i, 0))
    blk = pl.BlockSpec((PREP_TM, GROUP_WIDTH), lambda i: (i, 0))
    cos, sin = _rope_tables_rows(rows)
    return pl.pallas_call(
        _diff_prep_kernel,
        grid=(rows // PREP_TM,),
        in_specs=[col(0), col(1), col(2),
                  pl.BlockSpec((GROUP_WIDTH, GROUP_WIDTH), lambda i: (0, 0)), vec, vec, tab, tab],
        out_specs=[blk, blk, pl.BlockSpec((PREP_TM, 2 * GROUP_WIDTH), lambda i: (i, 0))],
        out_shape=[jax.ShapeDtypeStruct((rows, GROUP_WIDTH), BF16), jax.ShapeDtypeStruct((rows, GROUP_WIDTH), BF16),
                   jax.ShapeDtypeStruct((rows, 2 * GROUP_WIDTH), BF16)],
        compiler_params=pltpu.CompilerParams(
            dimension_semantics=("arbitrary",), vmem_limit_bytes=VMEM_LIMIT_BYTES),
        name="diff_prep",
    )(p, p, p, _segment_ones(GROUP_WIDTH, DIFF_HEAD_DIM),
      jnp.tile(q_norm, 2 * DIFF_HEADS).reshape(1, GROUP_WIDTH),
      jnp.tile(k_norm, 2 * DIFF_HEADS).reshape(1, GROUP_WIDTH), cos, sin)


def _diff_kernel(q_ref, k_ref, v_ref, lv_ref, sw_ref, o_ref, m_ref, acc_ref, s_ref, *, n_kchunks):
    i = pl.program_id(1)
    heads = range(DIFF_HPS)
    lo_half = lax.broadcasted_iota(jnp.int32, (DIFF_TQ, DIFF_HD), 1) < DIFF_HEAD_DIM
    q2 = []
    for h in heads:
        q = q_ref[:, h * DIFF_HD:(h + 1) * DIFF_HD]
        zero = jnp.zeros_like(q)
        q2.append(jnp.concatenate([jnp.where(lo_half, q, zero), jnp.where(lo_half, zero, q)], axis=0))

    m_ref[...] = jnp.full(m_ref.shape, MASKED, F32)
    acc_ref[...] = jnp.zeros_like(acc_ref)

    def keys(c):
        return pl.ds(pl.multiple_of(c * DIFF_TK, DIFF_TK), DIFF_TK)

    def k_of(h, rows):
        return k_ref[rows, h * DIFF_HD:(h + 1) * DIFF_HD]

    def v_of(h, rows):
        return v_ref[rows, 2 * h * DIFF_HD:2 * (h + 1) * DIFF_HD]

    def scores(c, slot):
        for h in heads:
            s_ref[h, slot] = _dot_nt(q2[h], k_of(h, keys(c)))

    def accumulate(s, rows):
        m_old = [m_ref[h] for h in heads]
        m_new = [jnp.maximum(m_old[h], jnp.max(s[h], axis=1, keepdims=True)) for h in heads]
        p = [jnp.exp(s[h] - m_new[h][:, 0:1]).astype(BF16) for h in heads]
        pv = [jnp.dot(p[h], v_of(h, rows), preferred_element_type=F32) for h in heads]
        for h in heads:
            alpha = jnp.exp(m_old[h] - m_new[h])
            acc_ref[h] = jnp.concatenate([alpha, alpha], axis=1) * acc_ref[h] + pv[h]
            m_ref[h] = m_new[h]

    @pl.when(i == 0)
    def _():
        ctx = slice(0, CTX_LEN)
        accumulate([_dot_nt(q2[h], k_of(h, ctx)) for h in heads], ctx)

    @pl.when(i > 0)
    def _():
        scores(0, 0)

        def pair(t, carry):
            c = 2 * t
            scores(c + 1, 1)
            accumulate([s_ref[h, 0] for h in heads], keys(c))
            scores(c + 2, 0)
            accumulate([s_ref[h, 1] for h in heads], keys(c + 1))
            return carry

        lax.fori_loop(0, (n_kchunks - 1) // 2, pair, 0)
        if n_kchunks % 2 == 0:
            scores(n_kchunks - 1, 1)
            accumulate([s_ref[h, 0] for h in heads], keys(n_kchunks - 2))
            accumulate([s_ref[h, 1] for h in heads], keys(n_kchunks - 1))
        else:
            accumulate([s_ref[h, 0] for h in heads], keys(n_kchunks - 1))

    lam_init = lv_ref[4:5, 0:1]
    lam = (jnp.exp(jnp.sum(lv_ref[0:1, :] * lv_ref[1:2, :], axis=1, keepdims=True))
           - jnp.exp(jnp.sum(lv_ref[2:3, :] * lv_ref[3:4, :], axis=1, keepdims=True)) + lam_init)
    for h in heads:
        a1 = acc_ref[h, 0:DIFF_TQ, :]
        a2 = acc_ref[h, DIFF_TQ:, :]
        o = a1[:, :DIFF_HD] / a1[:, DIFF_HD:] - lam * (a2[:, :DIFF_HD] / a2[:, DIFF_HD:])
        y = o * lax.rsqrt(jnp.mean(o * o, axis=1, keepdims=True) + EPS) * sw_ref[...]
        o_ref[:, h * DIFF_HD:(h + 1) * DIFF_HD] = y * (1.0 - lam_init)


def _diff_pallas(p, q_norm, k_norm, lq1, lk1, lq2, lk2, subln, lam_init):
    rows = p.shape[0]
    assert rows % DIFF_TK == 0 and rows % DIFF_TQ == 0 and CTX_LEN == DIFF_TQ
    q, k, v = _diff_prep(p, q_norm, k_norm)
    pad = lambda t: jnp.pad(t, (0, DIFF_HD - DIFF_HEAD_DIM))
    lvec = jnp.stack([pad(lq1), pad(lk1), pad(lq2), pad(lk2), jnp.full((DIFF_HD,), lam_init, F32),
                      jnp.zeros((DIFF_HD,), F32), jnp.zeros((DIFF_HD,), F32), jnp.zeros((DIFF_HD,), F32)])
    return pl.pallas_call(
        functools.partial(_diff_kernel, n_kchunks=rows // DIFF_TK),
        grid=(DIFF_HEADS // DIFF_HPS, rows // DIFF_TQ),
        in_specs=[pl.BlockSpec((DIFF_TQ, DIFF_HPS * DIFF_HD), lambda h, i: (i, h)),
                  pl.BlockSpec((rows, DIFF_HPS * DIFF_HD), lambda h, i: (0, h)),
                  pl.BlockSpec((rows, 2 * DIFF_HPS * DIFF_HD), lambda h, i: (0, h)),
                  pl.BlockSpec((8, DIFF_HD), lambda h, i: (0, 0)),
                  pl.BlockSpec((1, DIFF_HD), lambda h, i: (0, 0))],
        out_specs=pl.BlockSpec((DIFF_TQ, DIFF_HPS * DIFF_HD), lambda h, i: (i, h)),
        out_shape=jax.ShapeDtypeStruct((rows, GROUP_WIDTH), F32),
        scratch_shapes=[pltpu.VMEM((DIFF_HPS, 2 * DIFF_TQ, DIFF_HD), F32),
                        pltpu.VMEM((DIFF_HPS, 2 * DIFF_TQ, 2 * DIFF_HD), F32),
                        pltpu.VMEM((DIFF_HPS, 2, 2 * DIFF_TQ, DIFF_TK), F32)],
        compiler_params=pltpu.CompilerParams(
            dimension_semantics=("arbitrary", "arbitrary"), vmem_limit_bytes=VMEM_LIMIT_BYTES),
        name="diff_attention",
    )(q, k, v, lvec, subln.reshape(1, DIFF_HD))


GDN_TB = 256
GDN_HALO = 8
GDN_GATES = 2 * GDN_HEADS


def _softplus(x):
    return jnp.maximum(x, 0.0) + jnp.log(1.0 + jnp.exp(-jnp.abs(x)))


def _gdn_prep_kernel(*refs):
    (qp, qc, qn, kp, kc, kn, vp, vc, vn, ab_ref, w_ref, alog_ref, dtb_ref,
     qo_ref, ko_ref, vo_ref, go_ref, pad_ref) = refs
    i = pl.program_id(0)
    last = pl.num_programs(0) - 1
    prev_ok = (i >= 2).astype(F32)
    next_ok = jnp.logical_and(i >= 1, i < last).astype(F32)
    half = GDN_CONV // 2

    def conv_silu(prev_ref, cur_ref, next_ref, sec):
        pad_ref[0:GDN_HALO, :] = prev_ref[...] * prev_ok
        pad_ref[GDN_HALO:GDN_HALO + GDN_TB, :] = cur_ref[...]
        pad_ref[GDN_HALO + GDN_TB:, :] = next_ref[...] * next_ok
        acc = jnp.zeros((GDN_TB, GROUP_WIDTH), F32)
        for j in range(GDN_CONV):
            w = w_ref[j:j + 1, sec * GROUP_WIDTH:(sec + 1) * GROUP_WIDTH]
            acc = acc + pad_ref[GDN_HALO - half + j:GDN_HALO - half + j + GDN_TB, :] * w
        return acc * jax.nn.sigmoid(acc)

    def l2n(x):
        parts = []
        for h in range(GDN_HEADS):
            xh = x[:, h * GDN_HEAD_DIM:(h + 1) * GDN_HEAD_DIM]
            parts.append(xh * lax.rsqrt(jnp.sum(xh * xh, axis=1, keepdims=True) + EPS))
        return jnp.concatenate(parts, axis=1)

    qo_ref[...] = l2n(conv_silu(qp, qc, qn, 0)) * GDN_HEAD_DIM ** -0.5
    ko_ref[...] = l2n(conv_silu(kp, kc, kn, 1))
    vo_ref[...] = conv_silu(vp, vc, vn, 2)
    x = ab_ref[...]
    lane = lax.broadcasted_iota(jnp.int32, x.shape, 1)
    go_ref[...] = jnp.where(lane < GDN_GATES, -jnp.exp(alog_ref[...]) * _softplus(x + dtb_ref[...]),
                            jax.nn.sigmoid(x))


def _gdn_prep(p, conv_w, a_log, dt_bias):
    rows = p.shape[0]
    assert rows % GDN_TB == 0 and CTX_LEN == GDN_TB
    n_halo = rows // GDN_HALO
    per = GDN_TB // GDN_HALO
    c0 = P_GDN_QKV // GROUP_WIDTH
    specs = []
    for sec in range(3):
        specs += [
            pl.BlockSpec((GDN_HALO, GROUP_WIDTH), lambda i, sec=sec: (jnp.maximum(i * per - 1, 0), c0 + sec)),
            pl.BlockSpec((GDN_TB, GROUP_WIDTH), lambda i, sec=sec: (i, c0 + sec)),
            pl.BlockSpec((GDN_HALO, GROUP_WIDTH),
                         lambda i, sec=sec: (jnp.minimum((i + 1) * per, n_halo - 1), c0 + sec)),
        ]
    vec = pl.BlockSpec((1, 128), lambda i: (0, 0))
    specs += [pl.BlockSpec((GDN_TB, 128), lambda i: (i, P_GDN_AB // 128)),
              pl.BlockSpec((8, 3 * GROUP_WIDTH), lambda i: (0, 0)), vec, vec]
    blk = pl.BlockSpec((GDN_TB, GROUP_WIDTH), lambda i: (i, 0))
    out = jax.ShapeDtypeStruct((rows, GROUP_WIDTH), F32)
    pad8 = lambda t: jnp.pad(t.reshape(1, GDN_GATES), ((0, 0), (0, 128 - GDN_GATES)))
    return pl.pallas_call(
        _gdn_prep_kernel,
        grid=(rows // GDN_TB,),
        in_specs=specs,
        out_specs=[blk, blk, blk, pl.BlockSpec((GDN_TB, 128), lambda i: (i, 0))],
        out_shape=[out, out, out, jax.ShapeDtypeStruct((rows, 128), F32)],
        scratch_shapes=[pltpu.VMEM((GDN_TB + 2 * GDN_HALO, GROUP_WIDTH), F32)],
        compiler_params=pltpu.CompilerParams(
            dimension_semantics=("arbitrary",), vmem_limit_bytes=VMEM_LIMIT_BYTES),
        name="gdn_prep",
    )(*([p] * 10), jnp.pad(conv_w, ((0, 8 - GDN_CONV), (0, 0))), pad8(a_log), pad8(dt_bias))


def _mm3(a, b):
    ah = a.astype(BF16)
    al = (a - ah.astype(F32)).astype(BF16)
    bh = b.astype(BF16)
    bl = (b - bh.astype(F32)).astype(BF16)
    dot = lambda x, y: jnp.dot(x, y, preferred_element_type=F32)
    return dot(ah, bh) + dot(ah, bl) + dot(al, bh)


def _gdn_kernel(q_ref, k_ref, v_ref, g_ref, o_ref, s_ref, *, reverse):
    j = pl.program_id(0)
    tb, ck, hd = GDN_TB, GDN_CHUNK, GDN_HEAD_DIM
    d = 1 if reverse else 0

    @pl.when(j == 0)
    def _():
        s_ref[...] = jnp.zeros_like(s_ref)

    ri = lax.broadcasted_iota(jnp.int32, (tb, tb), 0)
    ci = lax.broadcasted_iota(jnp.int32, (tb, tb), 1)
    same = lambda n: (ri // n) == (ci // n)
    same_chunk = same(ck)
    before = (ci > ri) if reverse else (ci < ri)
    strict = jnp.logical_and(same_chunk, before)
    incl = jnp.logical_and(same_chunk, jnp.logical_or(before, ri == ci))
    eye = (ri == ci).astype(F32)

    g = g_ref[...]
    g1 = g.astype(BF16)
    r1 = g - g1.astype(F32)
    g2 = r1.astype(BF16)
    g3 = (r1 - g2.astype(F32)).astype(BF16)
    dot = lambda x, y: jnp.dot(x, y, preferred_element_type=F32)
    cum_m = incl.astype(BF16)
    tot_m = same_chunk.astype(BF16)
    gcum = dot(cum_m, g1) + dot(cum_m, g2) + dot(cum_m, g3)
    gtot = dot(tot_m, g1) + dot(tot_m, g2) + dot(tot_m, g3)
    gcum_t = gcum.T

    order = range(tb // ck - 1, -1, -1) if reverse else range(tb // ck)
    heads = range(GDN_HEADS)
    hs = lambda h: slice(h * hd, (h + 1) * hd)
    lanes = [d * GDN_HEADS + h for h in heads]
    gcol = [gcum[:, l:l + 1] for l in lanes]
    glast = [gtot[:, l:l + 1] for l in lanes]
    beta = [g[:, GDN_GATES + l:GDN_GATES + l + 1] for l in lanes]
    decay = [jnp.where(incl, jnp.exp(jnp.where(incl, gcol[h] - gcum_t[lanes[h]:lanes[h] + 1, :], 0.0)), 0.0)
             for h in heads]
    kb = [k_ref[:, hs(h)] * beta[h] for h in heads]
    k_b = [k_ref[:, hs(h)].astype(BF16) for h in heads]
    a_mat = [jnp.where(strict, _dot_nt(kb[h].astype(BF16), k_b[h]) * decay[h], 0.0) for h in heads]
    attn = [(_dot_nt(q_ref[:, hs(h)].astype(BF16), k_b[h]) * decay[h]).astype(BF16) for h in heads]

    x = [eye - jnp.where(same(2), a_mat[h], 0.0) for h in heads]
    for half_blk in (2, 4, 8, 16, 32):
        level = jnp.logical_and(same(2 * half_blk), jnp.logical_not(same(half_blk)))
        x_b = [x[h].astype(BF16) for h in heads]
        xl = [dot(x_b[h], jnp.where(level, a_mat[h], 0.0).astype(BF16)).astype(BF16) for h in heads]
        x = [x[h] - dot(xl[h], x_b[h]) for h in heads]
    resid = [eye - x[h] - _mm3(a_mat[h], x[h]) for h in heads]
    t_mat = [(x[h] + dot(x[h].astype(BF16), resid[h].astype(BF16))).astype(BF16) for h in heads]

    eg = [jnp.exp(gcol[h]) for h in heads]
    wu = [dot(t_mat[h], jnp.concatenate([kb[h] * eg[h], v_ref[:, hs(h)] * beta[h]], axis=1).astype(BF16))
          for h in heads]
    w_b = [wu[h][:, :hd].astype(BF16) for h in heads]
    qd_b = [(q_ref[:, hs(h)] * eg[h]).astype(BF16) for h in heads]
    k_carry = [(k_ref[:, hs(h)] * jnp.exp(glast[h] - gcol[h])).astype(BF16) for h in heads]

    s = [s_ref[h] for h in heads]
    v_new = [[None] * (tb // ck) for _ in heads]
    o_inter = [[None] * (tb // ck) for _ in heads]
    for c in order:
        rows = slice(c * ck, (c + 1) * ck)
        for h in heads:
            s_b = s[h].astype(BF16)
            v_new[h][c] = wu[h][rows, hd:] - dot(w_b[h][rows], s_b)
            o_inter[h][c] = dot(qd_b[h][rows], s_b)
            s[h] = s[h] * jnp.exp(glast[h][c * ck:c * ck + 1, :]) + lax.dot_general(
                k_carry[h][rows], v_new[h][c].astype(BF16), (((0,), (0,)), ((), ())),
                preferred_element_type=F32)
    for h in heads:
        s_ref[h] = s[h]
        v_all = jnp.concatenate(v_new[h], axis=0).astype(BF16)
        o_ref[:, hs(h)] = jnp.concatenate(o_inter[h], axis=0) + dot(attn[h], v_all)


def _gdn_scan(q, k, v, gates, reverse):
    rows = q.shape[0]
    n_blocks = rows // GDN_TB
    blk_of = (lambda j: jnp.where(j == 0, 0, n_blocks - j)) if reverse else (lambda j: j)
    blk = pl.BlockSpec((GDN_TB, GROUP_WIDTH), lambda j: (blk_of(j), 0))
    return pl.pallas_call(
        functools.partial(_gdn_kernel, reverse=reverse),
        grid=(n_blocks,),
        in_specs=[blk, blk, blk, pl.BlockSpec((GDN_TB, 128), lambda j: (blk_of(j), 0))],
        out_specs=blk,
        out_shape=jax.ShapeDtypeStruct((rows, GROUP_WIDTH), F32),
        scratch_shapes=[pltpu.VMEM((GDN_HEADS, GDN_HEAD_DIM, GDN_HEAD_DIM), F32)],
        compiler_params=pltpu.CompilerParams(
            dimension_semantics=("arbitrary",), vmem_limit_bytes=VMEM_LIMIT_BYTES),
        name="gdn_scan_bwd" if reverse else "gdn_scan_fwd",
    )(q, k, v, gates)


def _gdn_finish_kernel(of_ref, ob_ref, z_ref, w_ref, y_ref):
    o = of_ref[...] + ob_ref[...]
    z = z_ref[...]
    parts = []
    for h in range(GDN_HEADS):
        oh = o[:, h * GDN_HEAD_DIM:(h + 1) * GDN_HEAD_DIM]
        parts.append(oh * lax.rsqrt(jnp.mean(oh * oh, axis=1, keepdims=True) + EPS) * w_ref[...])
    y_ref[...] = jnp.concatenate(parts, axis=1) * (z * jax.nn.sigmoid(z))


def _gdn_pallas(p, conv_w, a_log, dt_bias, norm_w):
    rows = p.shape[0]
    q, k, v, gates = _gdn_prep(p, conv_w, a_log, dt_bias)
    o_f = _gdn_scan(q, k, v, gates, False)
    o_b = _gdn_scan(q, k, v, gates, True)
    blk = pl.BlockSpec((GDN_TB, GROUP_WIDTH), lambda i: (i, 0))
    return pl.pallas_call(
        _gdn_finish_kernel,
        grid=(rows // GDN_TB,),
        in_specs=[blk, blk, pl.BlockSpec((GDN_TB, GROUP_WIDTH), lambda i: (i, P_GDN_Z // GROUP_WIDTH)),
                  pl.BlockSpec((1, GDN_HEAD_DIM), lambda i: (0, 0))],
        out_specs=blk,
        out_shape=jax.ShapeDtypeStruct((rows, GROUP_WIDTH), F32),
        compiler_params=pltpu.CompilerParams(
            dimension_semantics=("arbitrary",), vmem_limit_bytes=VMEM_LIMIT_BYTES),
        name="gdn_finish",
    )(o_f, o_b, p, norm_w.reshape(1, GDN_HEAD_DIM))


def _reorder_w_in(w):
    front = w[..., 0:4096].astype(BF16)
    gdn_ab = w[..., 4096:4112].astype(BF16)
    na = w[..., 4112:5648].astype(BF16)
    pad = jnp.zeros(w.shape[:-1] + (P_WIDTH - P_GDN_AB - 16,), BF16)
    return jnp.concatenate([front, na, gdn_ab, pad], axis=-1)


def kernel(x, c, ctx, c_ctx, w_ada, b_ada, norm_ffn1, norm_mix, norm_ffn2, ffn1_w_in, ffn1_w_out,
           ffn2_w_in, ffn2_w_out, w_in, w_out, s5_lambda_re, s5_lambda_im, s5_log_step, s5_b_re,
           s5_b_im, s5_c_re, s5_c_im, s5_d, s5_w_glu, s5_b_glu, diff_q_norm, diff_k_norm,
           diff_lambda_q1, diff_lambda_k1, diff_lambda_q2, diff_lambda_k2, diff_subln, gdn_conv,
           gdn_a_log, gdn_dt_bias, gdn_norm, na_q_norm, na_k_norm, na_rpb):
    mod = _modulation(c, c_ctx, w_ada, b_ada)
    s = jnp.concatenate([ctx[0], x[0]], axis=0)
    ffn1_in, ffn1_out = ffn1_w_in.astype(BF16), ffn1_w_out.astype(BF16)
    ffn2_in, ffn2_out = ffn2_w_in.astype(BF16), ffn2_w_out.astype(BF16)
    w_in_bf, w_out_bf = _reorder_w_in(w_in), w_out.astype(BF16)
    for l in range(DEPTH):
        mod_l = mod[l]
        s = _ffn(s, mod_l, norm_ffn1[l], ffn1_in, ffn1_out, l, 0)
        p = _inproj(s, mod_l, norm_mix[l], w_in_bf, l)
        lam_init = 0.8 - 0.6 * math.exp(-0.3 * l)
        ya = _s5_pallas(p, s5_lambda_re[l], s5_lambda_im[l], s5_log_step[l], s5_b_re[l], s5_b_im[l],
                        s5_c_re[l], s5_c_im[l], s5_d[l], s5_w_glu[l], s5_b_glu[l])
        yb = _diff_pallas(p, diff_q_norm[l], diff_k_norm[l], diff_lambda_q1[l], diff_lambda_k1[l],
                          diff_lambda_q2[l], diff_lambda_k2[l], diff_subln[l], lam_init)
        yc = _gdn_pallas(p, gdn_conv[l], gdn_a_log[l], gdn_dt_bias[l], gdn_norm[l])
        yd = _na_pallas(p, na_q_norm[l], na_k_norm[l], na_rpb[l])
        s = _outproj(s, mod_l, [ya, yb, yc, yd], w_out_bf, l)
        s = _ffn(s, mod_l, norm_ffn2[l], ffn2_in, ffn2_out, l, 6)
    return s[None, CTX_LEN:]
```

```python
import functools
import math

import jax
import jax.numpy as jnp
from jax import lax
from jax.experimental import pallas as pl
from jax.experimental.pallas import tpu as pltpu

D_MODEL = 2048
SEQ = 8192
DEPTH = 4
GRID_W = 64
CTX_LEN = 256
ROWS = CTX_LEN + SEQ
GROUP_WIDTH = 512
D_FF = 5632
N_MOD = 9
EPS = 1e-6

S5_CH = 16
S5_GROUPS = GROUP_WIDTH // S5_CH
S5_STATE = 64
DIFF_HEADS = 4
DIFF_HEAD_DIM = 64
ROPE_BASE = 10000.0
Q_BLOCK = 128
GDN_HEADS = 4
GDN_HEAD_DIM = 128
GDN_CONV = 5
GDN_CHUNK = 64
NA_HEADS = 8
NA_HEAD_DIM = 64
WIN_H = 8
WIN_W = 16
NA_KEY_COLS = 2 * WIN_W

P_S5 = 0
P_DIFF = 512
P_GDN_QKV = 2048
P_GDN_Z = 3584
P_NA = 4096
P_GDN_AB = 5632
P_WIDTH = 5760

VMEM_LIMIT_BYTES = 56 * 1024 * 1024

F32 = jnp.float32
BF16 = jnp.bfloat16


MOD_TN = 1024


def _mod_kernel(ct_ref, w_ref, b_ref, o_ref):
    c = ct_ref[...]
    s = c * jax.nn.sigmoid(c)
    v0 = jnp.broadcast_to(s[:, 0:1], (D_MODEL, 128))
    v1 = jnp.broadcast_to(s[:, 1:2], (D_MODEL, 128))
    for j in range(MOD_TN // 128):
        w = w_ref[:, j * 128:(j + 1) * 128]
        b = b_ref[:, j * 128:(j + 1) * 128]
        o_ref[0:1, j * 128:(j + 1) * 128] = jnp.sum(w * v0, axis=0, keepdims=True) + b
        o_ref[1:2, j * 128:(j + 1) * 128] = jnp.sum(w * v1, axis=0, keepdims=True) + b


def _modulation(c, c_ctx, w_ada, b_ada):
    n = N_MOD * D_MODEL
    ct = jnp.stack([c.reshape(D_MODEL), c_ctx.reshape(D_MODEL)], axis=1)
    out = pl.pallas_call(
        _mod_kernel,
        grid=(DEPTH, n // MOD_TN),
        in_specs=[
            pl.BlockSpec((D_MODEL, 2), lambda l, j: (0, 0)),
            pl.BlockSpec((None, D_MODEL, MOD_TN), lambda l, j: (l, 0, j)),
            pl.BlockSpec((None, 1, MOD_TN), lambda l, j: (l, 0, j)),
        ],
        out_specs=pl.BlockSpec((None, 2, MOD_TN), lambda l, j: (l, 0, j)),
        out_shape=jax.ShapeDtypeStruct((DEPTH, 2, n), F32),
        compiler_params=pltpu.CompilerParams(
            dimension_semantics=("arbitrary", "arbitrary"), vmem_limit_bytes=VMEM_LIMIT_BYTES),
        name="adaln_mod",
    )(ct, w_ada, b_ada.reshape(DEPTH, 1, n))
    return out.reshape(DEPTH, 2 * N_MOD, D_MODEL)


def _is_ctx_rows(tile_rows):
    rows = pl.program_id(0) * tile_rows + lax.broadcasted_iota(jnp.int32, (tile_rows, 1), 0)
    return rows < CTX_LEN


def _mod_row(mod_ref, is_ctx, k):
    return jnp.where(is_ctx, mod_ref[N_MOD + k:N_MOD + k + 1, :], mod_ref[k:k + 1, :])


NORM_ROWS = 16


def _mod_chunk_row(mod_ref, first_row, k):
    off = jnp.where(first_row < CTX_LEN, N_MOD, 0)
    return mod_ref[pl.ds(off + k, 1), :]


def _fill_gain_shift(gamma_ref, mod_ref, gs_ref, base):
    for seg in range(2):
        gain = gamma_ref[...] * (1.0 + mod_ref[seg * N_MOD + base + 1:seg * N_MOD + base + 2, :])
        gs_ref[seg] = jnp.broadcast_to(gain, (NORM_ROWS, D_MODEL))
        gs_ref[2 + seg] = jnp.broadcast_to(mod_ref[seg * N_MOD + base:seg * N_MOD + base + 1, :],
                                           (NORM_ROWS, D_MODEL))


def _adaln_rows(x_ref, h_ref, gs_ref, tile_row0, lo, n_rows):
    for t in range(n_rows // NORM_ROWS):
        sl = slice(lo + t * NORM_ROWS, lo + (t + 1) * NORM_ROWS)
        seg = (tile_row0 + sl.start < CTX_LEN).astype(jnp.int32)
        x = x_ref[sl, :]
        ms = jnp.mean(x * x, axis=-1, keepdims=True)
        h = x * lax.rsqrt(ms + EPS) * gs_ref[seg] + gs_ref[2 + seg]
        h_ref[sl, :] = h.astype(BF16)


FFN_TM = 1408
FFN_RC = 352
FFN_TF = 512


def _ffn_kernel(x_ref, mod_ref, gamma_ref, wg_ref, wu_ref, wo_ref, o_ref, h_ref, gs_ref, *, base):
    f = pl.program_id(1)
    last = pl.num_programs(1) - 1
    row0 = pl.program_id(0) * FFN_TM
    n_chunks = FFN_TM // FFN_RC

    def down(r):
        rows = pl.ds(r * FFN_RC, FFN_RC)
        h = h_ref[rows, :]
        g = jnp.dot(h, wg_ref[...], preferred_element_type=F32)
        u = jnp.dot(h, wu_ref[...], preferred_element_type=F32)
        a = (g * jax.nn.sigmoid(g) * u).astype(BF16)
        return rows, jnp.dot(a, wo_ref[...], preferred_element_type=F32)

    @pl.when(f == 0)
    def _():
        _fill_gain_shift(gamma_ref, mod_ref, gs_ref, base)
        _adaln_rows(x_ref, h_ref, gs_ref, row0, 0, FFN_RC)
        for r in range(n_chunks):
            if r + 1 < n_chunks:
                _adaln_rows(x_ref, h_ref, gs_ref, row0, (r + 1) * FFN_RC, FFN_RC)
            rows, d = down(r)
            o_ref[rows, :] = d

    @pl.when(jnp.logical_and(f > 0, f < last))
    def _():
        for r in range(n_chunks):
            rows, d = down(r)
            o_ref[rows, :] += d

    @pl.when(f == last)
    def _():
        for r in range(n_chunks):
            _, d = down(r)
            for t in range(FFN_RC // NORM_ROWS):
                lo = r * FFN_RC + t * NORM_ROWS
                sl = slice(lo, lo + NORM_ROWS)
                gate = _mod_chunk_row(mod_ref, row0 + lo, base + 2)
                o_ref[sl, :] = x_ref[sl, :] + 0.5 * gate * (o_ref[sl, :] + d[t * NORM_ROWS:(t + 1) * NORM_ROWS])


def _ffn(s, mod_l, gamma, w_in_bf, w_out_bf, layer, base):
    nf = D_FF // FFN_TF
    return pl.pallas_call(
        functools.partial(_ffn_kernel, base=base),
        grid=(ROWS // FFN_TM, nf),
        in_specs=[
            pl.BlockSpec((FFN_TM, D_MODEL), lambda i, f: (i, 0)),
            pl.BlockSpec((2 * N_MOD, D_MODEL), lambda i, f: (0, 0)),
            pl.BlockSpec((1, D_MODEL), lambda i, f: (0, 0)),
            pl.BlockSpec((None, D_MODEL, FFN_TF), lambda i, f: (layer, 0, f)),
            pl.BlockSpec((None, D_MODEL, FFN_TF), lambda i, f: (layer, 0, nf + f)),
            pl.BlockSpec((None, FFN_TF, D_MODEL), lambda i, f: (layer, f, 0)),
        ],
        out_specs=pl.BlockSpec((FFN_TM, D_MODEL), lambda i, f: (i, 0), pipeline_mode=pl.Buffered(1)),
        out_shape=jax.ShapeDtypeStruct((ROWS, D_MODEL), F32),
        scratch_shapes=[pltpu.VMEM((FFN_TM, D_MODEL), BF16), pltpu.VMEM((4, NORM_ROWS, D_MODEL), F32)],
        compiler_params=pltpu.CompilerParams(
            dimension_semantics=("arbitrary", "arbitrary"), vmem_limit_bytes=VMEM_LIMIT_BYTES),
        name="ffn_swiglu",
    )(s, mod_l, gamma.reshape(1, D_MODEL), w_in_bf, w_in_bf, w_out_bf)


INP_TM = 1408
INP_RC = 352
INP_TN = 1152


def _inproj_kernel(x_ref, mod_ref, gamma_ref, w_ref, o_ref, h_ref, gs_ref):
    n = pl.program_id(1)
    row0 = pl.program_id(0) * INP_TM
    n_chunks = INP_TM // INP_RC

    def project(r):
        rows = pl.ds(r * INP_RC, INP_RC)
        o_ref[rows, :] = jnp.dot(h_ref[rows, :], w_ref[...], preferred_element_type=F32)

    @pl.when(n == 0)
    def _():
        _fill_gain_shift(gamma_ref, mod_ref, gs_ref, 3)
        _adaln_rows(x_ref, h_ref, gs_ref, row0, 0, INP_RC)
        for r in range(n_chunks):
            if r + 1 < n_chunks:
                _adaln_rows(x_ref, h_ref, gs_ref, row0, (r + 1) * INP_RC, INP_RC)
            project(r)

    @pl.when(n > 0)
    def _():
        for r in range(n_chunks):
            project(r)


def _inproj(s, mod_l, gamma, w_bf, layer):
    return pl.pallas_call(
        _inproj_kernel,
        grid=(ROWS // INP_TM, P_WIDTH // INP_TN),
        in_specs=[
            pl.BlockSpec((INP_TM, D_MODEL), lambda i, n: (i, 0)),
            pl.BlockSpec((2 * N_MOD, D_MODEL), lambda i, n: (0, 0)),
            pl.BlockSpec((1, D_MODEL), lambda i, n: (0, 0)),
            pl.BlockSpec((None, D_MODEL, INP_TN), lambda i, n: (layer, 0, n)),
        ],
        out_specs=pl.BlockSpec((INP_TM, INP_TN), lambda i, n: (i, n)),
        out_shape=jax.ShapeDtypeStruct((ROWS, P_WIDTH), F32),
        scratch_shapes=[pltpu.VMEM((INP_TM, D_MODEL), BF16), pltpu.VMEM((4, NORM_ROWS, D_MODEL), F32)],
        compiler_params=pltpu.CompilerParams(
            dimension_semantics=("arbitrary", "arbitrary"), vmem_limit_bytes=VMEM_LIMIT_BYTES),
        name="in_proj",
    )(s, mod_l, gamma.reshape(1, D_MODEL), w_bf)


OUT_TM = 384


def _outproj_kernel(x_ref, mod_ref, ya_ref, yb_ref, yc_ref, yd_ref, w_ref, o_ref):
    is_ctx = _is_ctx_rows(OUT_TM)
    acc = jnp.zeros((OUT_TM, D_MODEL), F32)
    for k, y_ref in enumerate((ya_ref, yb_ref, yc_ref, yd_ref)):
        acc += jnp.dot(y_ref[...].astype(BF16), w_ref[k * GROUP_WIDTH:(k + 1) * GROUP_WIDTH, :],
                       preferred_element_type=F32)
    o_ref[...] = x_ref[...] + _mod_row(mod_ref, is_ctx, 5) * acc


def _outproj(s, mod_l, ys, w_bf, layer):
    yspec = pl.BlockSpec((OUT_TM, GROUP_WIDTH), lambda i: (i, 0))
    return pl.pallas_call(
        _outproj_kernel,
        grid=(ROWS // OUT_TM,),
        in_specs=[
            pl.BlockSpec((OUT_TM, D_MODEL), lambda i: (i, 0)),
            pl.BlockSpec((2 * N_MOD, D_MODEL), lambda i: (0, 0)),
            yspec, yspec, yspec, yspec,
            pl.BlockSpec((None, D_MODEL, D_MODEL), lambda i: (layer, 0, 0)),
        ],
        out_specs=pl.BlockSpec((OUT_TM, D_MODEL), lambda i: (i, 0)),
        out_shape=jax.ShapeDtypeStruct((ROWS, D_MODEL), F32),
        compiler_params=pltpu.CompilerParams(
            dimension_semantics=("arbitrary",), vmem_limit_bytes=VMEM_LIMIT_BYTES),
        name="out_proj",
    )(s, mod_l, *ys, w_bf)


S5_TC = 256


def _s5_prepare(lam_re, lam_im, log_step, b_re, b_im, c_re, c_im):
    dt = jnp.exp(log_step)[..., None]
    mag = jnp.exp(lam_re * dt)
    ar = mag * jnp.cos(lam_im * dt)
    ai = mag * jnp.sin(lam_im * dt)
    den = lam_re * lam_re + lam_im * lam_im
    fr = ((ar - 1.0) * lam_re + ai * lam_im) / den
    fi = (ai * lam_re - (ar - 1.0) * lam_im) / den
    bbr = fr[..., None] * b_re - fi[..., None] * b_im
    bbi = fr[..., None] * b_im + fi[..., None] * b_re
    in_oct = jnp.eye(4, dtype=F32)[jnp.arange(S5_GROUPS) % 4]
    half = 4 * S5_STATE

    def w_in_half(bb):
        return jnp.einsum('dgpn,gj->dgnjp', bb, in_oct).reshape(2, GROUP_WIDTH, half)

    def w_out_half(cc):
        return jnp.einsum('dgcp,gj->djpgc', cc, in_oct).reshape(2, half, GROUP_WIDTH)

    w_in = jnp.concatenate([w_in_half(bbr), w_in_half(bbi)], axis=2)
    w_out = jnp.concatenate([w_out_half(c_re), w_out_half(-c_im)], axis=1)
    return (ar.reshape(2, 8, half), ai.reshape(2, 8, half), w_in.astype(BF16), w_out.astype(BF16))


def _s5_scan_kernel(uf_ref, ub_ref, win_ref, wout_ref, ar_ref, ai_ref, yf_ref, yb_ref,
                    buf_ref, bub_ref, h_ref):
    tc = S5_TC
    lane_blk = lambda b: slice(b * 128, (b + 1) * 128)

    @pl.when(pl.program_id(0) == 0)
    def _():
        h_ref[...] = jnp.zeros_like(h_ref)

    lo_half = lax.broadcasted_iota(jnp.int32, (tc, 128), 1) < 64
    octants = [(b, 2 * b + e, lo_half if e == 0 else jnp.logical_not(lo_half)) for b in range(4) for e in range(2)]

    def project_in(u_ref, d, bu_ref):
        for b, q, mask in octants:
            lhs = jnp.where(mask, u_ref[:, lane_blk(b)], 0.0).astype(BF16)
            res = jnp.dot(lhs, win_ref[d, lane_blk(b), :], preferred_element_type=F32)
            for j in range(4):
                bu_ref[j, pl.ds(q, tc, stride=8), :] = res[:, lane_blk(j)]

    project_in(uf_ref, 0, buf_ref)
    project_in(ub_ref, 1, bub_ref)

    arf, aif, arb, aib = ar_ref[0], ai_ref[0], ar_ref[1], ai_ref[1]

    def load(ref, rows):
        return (jnp.concatenate([ref[0, rows, :], ref[1, rows, :]], axis=1),
                jnp.concatenate([ref[2, rows, :], ref[3, rows, :]], axis=1))

    def store(ref, rows, re, im):
        ref[0, rows, :] = re[:, :128]
        ref[1, rows, :] = re[:, 128:]
        ref[2, rows, :] = im[:, :128]
        ref[3, rows, :] = im[:, 128:]

    def step(t, carry):
        hfr, hfi, hbr, hbi = carry
        rf = pl.ds(pl.multiple_of(t * 8, 8), 8)
        rb = pl.ds(pl.multiple_of((tc - 1 - t) * 8, 8), 8)
        bfr, bfi = load(buf_ref, rf)
        bbr, bbi = load(bub_ref, rb)
        nfr = arf * hfr - aif * hfi + bfr
        nfi = arf * hfi + aif * hfr + bfi
        nbr = arb * hbr - aib * hbi + bbr
        nbi = arb * hbi + aib * hbr + bbi
        store(buf_ref, rf, nfr, nfi)
        store(bub_ref, rb, nbr, nbi)
        return nfr, nfi, nbr, nbi

    carry = lax.fori_loop(0, tc, step, (h_ref[0], h_ref[1], h_ref[2], h_ref[3]), unroll=8)
    for i in range(4):
        h_ref[i] = carry[i]

    def project_out(bu_ref, d, y_ref):
        for b in range(4):
            res = []
            for q in (2 * b, 2 * b + 1):
                states = jnp.concatenate([bu_ref[j, pl.ds(q, tc, stride=8), :] for j in range(4)], axis=1)
                res.append(jnp.dot(states.astype(BF16), wout_ref[d, :, lane_blk(b)], preferred_element_type=F32))
            y_ref[:, lane_blk(b)] = jnp.where(lo_half, res[0], res[1])

    project_out(buf_ref, 0, yf_ref)
    project_out(bub_ref, 1, yb_ref)


def _s5_scan_call(p, a_r, a_i, w_in, w_out):
    rows = p.shape[0]
    n_chunks = rows // S5_TC
    bwd = lambda j: jnp.where(j == 0, 0, n_chunks - j)
    full = lambda shape: pl.BlockSpec(shape, lambda j: (0,) * len(shape))
    y = jax.ShapeDtypeStruct((rows, GROUP_WIDTH), F32)
    return pl.pallas_call(
        _s5_scan_kernel,
        grid=(n_chunks,),
        in_specs=[
            pl.BlockSpec((S5_TC, GROUP_WIDTH), lambda j: (j, 0)),
            pl.BlockSpec((S5_TC, GROUP_WIDTH), lambda j: (bwd(j), 0)),
            full((2, GROUP_WIDTH, GROUP_WIDTH)), full((2, GROUP_WIDTH, GROUP_WIDTH)),
            full((2, 8, 256)), full((2, 8, 256)),
        ],
        out_specs=[pl.BlockSpec((S5_TC, GROUP_WIDTH), lambda j: (j, 0)),
                   pl.BlockSpec((S5_TC, GROUP_WIDTH), lambda j: (bwd(j), 0))],
        out_shape=[y, y],
        scratch_shapes=[
            pltpu.VMEM((4, 8 * S5_TC, 128), F32),
            pltpu.VMEM((4, 8 * S5_TC, 128), F32),
            pltpu.VMEM((4, 8, 256), F32),
        ],
        compiler_params=pltpu.CompilerParams(
            dimension_semantics=("arbitrary",), vmem_limit_bytes=VMEM_LIMIT_BYTES),
        name="s5_scan",
    )(p, p, w_in, w_out, a_r, a_i)


S5_FIN_TM = 384


def _s5_finish_kernel(yf_ref, yb_ref, u_ref, d_ref, w_ref, b_ref, o_ref):
    y = yf_ref[...] + yb_ref[...] + d_ref[...] * u_ref[...]
    h = jnp.dot(jax.nn.gelu(y).astype(BF16), w_ref[...], preferred_element_type=F32) + b_ref[...]
    o_ref[...] = h[:, :GROUP_WIDTH] * jax.nn.sigmoid(h[:, GROUP_WIDTH:])


def _s5_finish(yf, yb, p, d_skip, w_glu_bf, b_glu):
    rows = p.shape[0]
    row_blk = pl.BlockSpec((S5_FIN_TM, GROUP_WIDTH), lambda i: (i, 0))
    return pl.pallas_call(
        _s5_finish_kernel,
        grid=(rows // S5_FIN_TM,),
        in_specs=[row_blk, row_blk, row_blk,
                  pl.BlockSpec((1, GROUP_WIDTH), lambda i: (0, 0)),
                  pl.BlockSpec((GROUP_WIDTH, 2 * GROUP_WIDTH), lambda i: (0, 0)),
                  pl.BlockSpec((1, 2 * GROUP_WIDTH), lambda i: (0, 0))],
        out_specs=row_blk,
        out_shape=jax.ShapeDtypeStruct((rows, GROUP_WIDTH), F32),
        compiler_params=pltpu.CompilerParams(
            dimension_semantics=("arbitrary",), vmem_limit_bytes=VMEM_LIMIT_BYTES),
        name="s5_finish",
    )(yf, yb, p, d_skip.reshape(1, GROUP_WIDTH), w_glu_bf, b_glu.reshape(1, 2 * GROUP_WIDTH))


def _s5_pallas(p, lam_re, lam_im, log_step, b_re, b_im, c_re, c_im, d_skip, w_glu, b_glu):
    a_r, a_i, w_in, w_out = _s5_prepare(lam_re, lam_im, log_step, b_re, b_im, c_re, c_im)
    yf, yb = _s5_scan_call(p, a_r, a_i, w_in, w_out)
    return _s5_finish(yf, yb, p, d_skip, w_glu.astype(BF16), b_glu)


PREP_TM = 256
PREP_RC = 128


def _segment_ones(width, seg):
    i = jnp.arange(width) // seg
    return (i[:, None] == i[None, :]).astype(BF16)


def _seg_sumsq(x, e):
    x2 = x * x
    hi = x2.astype(BF16)
    lo = (x2 - hi.astype(F32)).astype(BF16)
    return jnp.dot(hi, e, preferred_element_type=F32) + jnp.dot(lo, e, preferred_element_type=F32)


NA_QROWS = 4
NA_KROWS = NA_QROWS + WIN_H
NA_NQ = NA_QROWS * GRID_W
NA_NK = NA_KROWS * GRID_W
MASKED = -1e30


def _na_prep_kernel(q_ref, k_ref, v_ref, e_ref, qw_ref, kw_ref, qo_ref, ko_ref, vo_ref):
    def body(c, carry):
        sl = pl.ds(pl.multiple_of(c * PREP_RC, PREP_RC), PREP_RC)
        q = q_ref[sl, :]
        k = k_ref[sl, :]
        qn = q * lax.rsqrt(_seg_sumsq(q, e_ref[...]) * (1.0 / NA_HEAD_DIM) + EPS) * qw_ref[...]
        kn = k * lax.rsqrt(_seg_sumsq(k, e_ref[...]) * (1.0 / NA_HEAD_DIM) + EPS) * kw_ref[...]
        qo_ref[sl, :] = (qn * NA_HEAD_DIM ** -0.5).astype(BF16)
        ko_ref[sl, :] = kn.astype(BF16)
        vo_ref[sl, :] = v_ref[sl, :].astype(BF16)
        return carry

    lax.fori_loop(0, PREP_TM // PREP_RC, body, 0)


def _na_prep(p, q_norm, k_norm):
    rows = p.shape[0]
    assert rows % PREP_TM == 0
    c0 = P_NA // GROUP_WIDTH
    col = lambda j: pl.BlockSpec((PREP_TM, GROUP_WIDTH), lambda i: (i, c0 + j))
    vec = pl.BlockSpec((1, GROUP_WIDTH), lambda i: (0, 0))
    out = jax.ShapeDtypeStruct((rows, GROUP_WIDTH), BF16)
    blk = pl.BlockSpec((PREP_TM, GROUP_WIDTH), lambda i: (i, 0))
    return pl.pallas_call(
        _na_prep_kernel,
        grid=(rows // PREP_TM,),
        in_specs=[col(0), col(1), col(2),
                  pl.BlockSpec((GROUP_WIDTH, GROUP_WIDTH), lambda i: (0, 0)), vec, vec],
        out_specs=[blk, blk, blk],
        out_shape=[out, out, out],
        compiler_params=pltpu.CompilerParams(
            dimension_semantics=("arbitrary",), vmem_limit_bytes=VMEM_LIMIT_BYTES),
        name="na_prep",
    )(p, p, p, _segment_ones(GROUP_WIDTH, NA_HEAD_DIM),
      jnp.tile(q_norm, NA_HEADS).reshape(1, GROUP_WIDTH), jnp.tile(k_norm, NA_HEADS).reshape(1, GROUP_WIDTH))


def _na_bias_table(rpb, n_grid_rows):
    import numpy as np
    c = np.arange(GRID_W)
    col0 = np.clip(c - WIN_W // 2, 0, GRID_W - WIN_W)
    kc = np.arange(GRID_W)
    col_ok = (kc[None, :] >= col0[:, None]) & (kc[None, :] < col0[:, None] + WIN_W)
    dcol = kc[None, :] - c[:, None] + WIN_W - 1
    shift = ((dcol[None] == np.arange(2 * WIN_W - 1)[:, None, None]) & col_ok[None]).astype(np.float32)
    jr = np.arange(NA_QROWS)[:, None]
    kr = np.arange(NA_KROWS)[None, :]
    ri = np.stack([kr - jr + WIN_H - 1, kr - jr + WIN_H - 1 - WIN_H // 2, kr - jr + WIN_H - 1 - WIN_H])
    ok = np.stack([(kr < WIN_H) & (jr >= 0), (kr - jr >= 0) & (kr - jr < WIN_H), (kr >= NA_QROWS) & (jr >= 0)])
    valid = ok[:, :, None, :, None] & col_ok[None, None, :, None, :]
    neg = np.where(valid, 0.0, MASKED).astype(np.float32).reshape(3, 1, NA_NQ, NA_NK)
    hi = lax.Precision.HIGHEST
    band = jnp.einsum('hrj,jck->hrck', rpb, jnp.asarray(shift), precision=hi)
    zero = jnp.zeros_like(band[:, :1])
    pairs = jnp.concatenate([jnp.concatenate([zero, band], axis=1), jnp.concatenate([band, zero], axis=1)],
                            axis=-1)
    sel = (ri[:, :, 0::2, None] + 1 == np.arange(2 * WIN_H)).astype(np.float32)
    tab = jnp.einsum('vjmr,hrcy->vhjcmy', jnp.asarray(sel), pairs, precision=hi)
    return tab.reshape(3, NA_HEADS, NA_NQ, NA_NK) + jnp.asarray(neg)


def _softmax_pv(scores, values):
    heads = range(len(scores))
    m = [functools.reduce(jnp.maximum, [jnp.max(s, axis=1, keepdims=True) for s in scores[h]]) for h in heads]
    ps = [[jnp.exp(s - m[h]) for s in scores[h]] for h in heads]
    denom = [functools.reduce(lambda a, b: a + b, [jnp.sum(p, axis=1, keepdims=True) for p in ps[h]])
             for h in heads]
    acc = [functools.reduce(lambda a, b: a + b,
                            [jnp.dot(p.astype(BF16), v, preferred_element_type=F32)
                             for p, v in zip(ps[h], values[h])]) for h in heads]
    return [acc[h] / denom[h] for h in heads]


def _dot_nt(a, b):
    return lax.dot_general(a, b, (((1,), (1,)), ((), ())), preferred_element_type=F32)


def _na_kernel(q_ref, k_ref, v_ref, tab_ref, o_ref, *, n_blocks):
    i = pl.program_id(1)
    lo_half = lax.broadcasted_iota(jnp.int32, (NA_NQ, 128), 1) < NA_HEAD_DIM
    q = q_ref[...]
    zero = jnp.zeros_like(q)
    q_heads = (jnp.where(lo_half, q, zero), jnp.where(lo_half, zero, q))
    kc = k_ref[0:CTX_LEN, :]
    vc = v_ref[0:CTX_LEN, :]

    @pl.when(i == 0)
    def _():
        outs = _softmax_pv([[_dot_nt(qh, kc)] for qh in q_heads], [[vc], [vc]])
        o_ref[...] = jnp.where(lo_half, outs[0], outs[1])

    @pl.when(i > 0)
    def _():
        ib = i - 1
        kr0 = jnp.clip(NA_QROWS * ib - WIN_H // 2, 0, NA_QROWS * n_blocks - NA_KROWS)
        start = pl.multiple_of(CTX_LEN + kr0 * GRID_W, GRID_W)
        kwin = k_ref[pl.ds(start, NA_NK), :]
        vwin = v_ref[pl.ds(start, NA_NK), :]
        variant = jnp.where(ib == 0, 0, jnp.where(ib == n_blocks - 1, 2, 1))
        outs = []
        for e, qh in enumerate(q_heads):
            s_loc = _dot_nt(qh, kwin) + tab_ref[variant, e]
            outs += _softmax_pv([[s_loc, _dot_nt(qh, kc)]], [[vwin, vc]])
        o_ref[...] = jnp.where(lo_half, outs[0], outs[1])


def _na_pallas(p, q_norm, k_norm, rpb):
    rows = p.shape[0]
    n_grid_rows = (rows - CTX_LEN) // GRID_W
    n_blocks = n_grid_rows // NA_QROWS
    q, k, v = _na_prep(p, q_norm, k_norm)
    tab = _na_bias_table(rpb, n_grid_rows).reshape(3, NA_HEADS // 2, 2, NA_NQ, NA_NK)
    kv = pl.BlockSpec((rows, 128), lambda h, i: (0, h))
    return pl.pallas_call(
        functools.partial(_na_kernel, n_blocks=n_blocks),
        grid=(NA_HEADS // 2, n_blocks + 1),
        in_specs=[pl.BlockSpec((NA_NQ, 128), lambda h, i: (i, h)), kv, kv,
                  pl.BlockSpec((3, None, 2, NA_NQ, NA_NK), lambda h, i: (0, h, 0, 0, 0))],
        out_specs=pl.BlockSpec((NA_NQ, 128), lambda h, i: (i, h)),
        out_shape=jax.ShapeDtypeStruct((rows, GROUP_WIDTH), F32),
        compiler_params=pltpu.CompilerParams(
            dimension_semantics=("arbitrary", "arbitrary"), vmem_limit_bytes=VMEM_LIMIT_BYTES),
        name="na_attention",
    )(q, k, v, tab)


DIFF_TQ = 256
DIFF_TK = 1408
DIFF_HD = 2 * DIFF_HEAD_DIM
DIFF_HPS = 2


def _rope_tables_rows(rows):
    pos = jnp.arange(rows - CTX_LEN)
    row = (pos // GRID_W).astype(F32)
    col = (pos % GRID_W).astype(F32)
    n_freq = DIFF_HEAD_DIM // 4
    inv = ROPE_BASE ** (-jnp.arange(n_freq, dtype=F32) / n_freq)
    ang = jnp.concatenate([row[:, None] * inv, row[:, None] * inv, col[:, None] * inv, col[:, None] * inv], -1)
    sign = jnp.tile(jnp.repeat(jnp.array([-1.0, 1.0, -1.0, 1.0], F32), n_freq), 2)
    cos = jnp.concatenate([jnp.ones((CTX_LEN, DIFF_HD), F32), jnp.tile(jnp.cos(ang), (1, 2))], axis=0)
    sin = jnp.concatenate([jnp.zeros((CTX_LEN, DIFF_HD), F32), jnp.tile(jnp.sin(ang), (1, 2)) * sign], axis=0)
    return cos, sin


def _diff_prep_kernel(q_ref, k_ref, v_ref, e_ref, qw_ref, kw_ref, cos_ref, sin_ref, qo_ref, ko_ref, vo_ref):
    quarter = lax.broadcasted_iota(jnp.int32, (PREP_RC, GROUP_WIDTH), 1) // (DIFF_HEAD_DIM // 4)
    first_of_pair = quarter % 2 == 0

    def rope(x, cos, sin):
        partner = jnp.where(first_of_pair, pltpu.roll(x, GROUP_WIDTH - DIFF_HEAD_DIM // 4, 1),
                            pltpu.roll(x, DIFF_HEAD_DIM // 4, 1))
        return x * cos + partner * sin

    def body(c, carry):
        sl = pl.ds(pl.multiple_of(c * PREP_RC, PREP_RC), PREP_RC)
        cos = jnp.concatenate([cos_ref[sl, :]] * DIFF_HEADS, axis=1)
        sin = jnp.concatenate([sin_ref[sl, :]] * DIFF_HEADS, axis=1)
        q = q_ref[sl, :]
        k = k_ref[sl, :]
        qn = q * lax.rsqrt(_seg_sumsq(q, e_ref[...]) * (1.0 / DIFF_HEAD_DIM) + EPS) * qw_ref[...]
        kn = k * lax.rsqrt(_seg_sumsq(k, e_ref[...]) * (1.0 / DIFF_HEAD_DIM) + EPS) * kw_ref[...]
        qo_ref[sl, :] = (rope(qn, cos, sin) * DIFF_HEAD_DIM ** -0.5).astype(BF16)
        ko_ref[sl, :] = rope(kn, cos, sin).astype(BF16)
        v = v_ref[sl, :].astype(BF16)
        ones = jnp.ones((PREP_RC, DIFF_HD), BF16)
        for h in range(DIFF_HEADS):
            vo_ref[sl, 2 * h * DIFF_HD:(2 * h + 1) * DIFF_HD] = v[:, h * DIFF_HD:(h + 1) * DIFF_HD]
            vo_ref[sl, (2 * h + 1) * DIFF_HD:(2 * h + 2) * DIFF_HD] = ones
        return carry

    lax.fori_loop(0, PREP_TM // PREP_RC, body, 0)


def _diff_prep(p, q_norm, k_norm):
    rows = p.shape[0]
    assert rows % PREP_TM == 0
    c0 = P_DIFF // GROUP_WIDTH
    col = lambda j: pl.BlockSpec((PREP_TM, GROUP_WIDTH), lambda i: (i, c0 + j))
    vec = pl.BlockSpec((1, GROUP_WIDTH), lambda i: (0, 0))
    tab = pl.BlockSpec((PREP_TM, DIFF_HD), lambda i: (i, 0))
    blk = pl.BlockSpec((PREP_TM, GROUP_WIDTH), lambda i: (i, 0))
    cos, sin = _rope_tables_rows(rows)
    return pl.pallas_call(
        _diff_prep_kernel,
        grid=(rows // PREP_TM,),
        in_specs=[col(0), col(1), col(2),
                  pl.BlockSpec((GROUP_WIDTH, GROUP_WIDTH), lambda i: (0, 0)), vec, vec, tab, tab],
        out_specs=[blk, blk, pl.BlockSpec((PREP_TM, 2 * GROUP_WIDTH), lambda i: (i, 0))],
        out_shape=[jax.ShapeDtypeStruct((rows, GROUP_WIDTH), BF16), jax.ShapeDtypeStruct((rows, GROUP_WIDTH), BF16),
                   jax.ShapeDtypeStruct((rows, 2 * GROUP_WIDTH), BF16)],
        compiler_params=pltpu.CompilerParams(
            dimension_semantics=("arbitrary",), vmem_limit_bytes=VMEM_LIMIT_BYTES),
        name="diff_prep",
    )(p, p, p, _segment_ones(GROUP_WIDTH, DIFF_HEAD_DIM),
      jnp.tile(q_norm, 2 * DIFF_HEADS).reshape(1, GROUP_WIDTH),
      jnp.tile(k_norm, 2 * DIFF_HEADS).reshape(1, GROUP_WIDTH), cos, sin)


def _diff_kernel(q_ref, k_ref, v_ref, lv_ref, sw_ref, o_ref, m_ref, acc_ref, s_ref, *, n_kchunks):
    i = pl.program_id(1)
    heads = range(DIFF_HPS)
    lo_half = lax.broadcasted_iota(jnp.int32, (DIFF_TQ, DIFF_HD), 1) < DIFF_HEAD_DIM
    q2 = []
    for h in heads:
        q = q_ref[:, h * DIFF_HD:(h + 1) * DIFF_HD]
        zero = jnp.zeros_like(q)
        q2.append(jnp.concatenate([jnp.where(lo_half, q, zero), jnp.where(lo_half, zero, q)], axis=0))

    m_ref[...] = jnp.full(m_ref.shape, MASKED, F32)
    acc_ref[...] = jnp.zeros_like(acc_ref)

    def keys(c):
        return pl.ds(pl.multiple_of(c * DIFF_TK, DIFF_TK), DIFF_TK)

    def k_of(h, rows):
        return k_ref[rows, h * DIFF_HD:(h + 1) * DIFF_HD]

    def v_of(h, rows):
        return v_ref[rows, 2 * h * DIFF_HD:2 * (h + 1) * DIFF_HD]

    def scores(c, slot):
        for h in heads:
            s_ref[h, slot] = _dot_nt(q2[h], k_of(h, keys(c)))

    def accumulate(s, rows):
        m_old = [m_ref[h] for h in heads]
        m_new = [jnp.maximum(m_old[h], jnp.max(s[h], axis=1, keepdims=True)) for h in heads]
        p = [jnp.exp(s[h] - m_new[h][:, 0:1]).astype(BF16) for h in heads]
        pv = [jnp.dot(p[h], v_of(h, rows), preferred_element_type=F32) for h in heads]
        for h in heads:
            alpha = jnp.exp(m_old[h] - m_new[h])
            acc_ref[h] = jnp.concatenate([alpha, alpha], axis=1) * acc_ref[h] + pv[h]
            m_ref[h] = m_new[h]

    @pl.when(i == 0)
    def _():
        ctx = slice(0, CTX_LEN)
        accumulate([_dot_nt(q2[h], k_of(h, ctx)) for h in heads], ctx)

    @pl.when(i > 0)
    def _():
        scores(0, 0)

        def pair(t, carry):
            c = 2 * t
            scores(c + 1, 1)
            accumulate([s_ref[h, 0] for h in heads], keys(c))
            scores(c + 2, 0)
            accumulate([s_ref[h, 1] for h in heads], keys(c + 1))
            return carry

        lax.fori_loop(0, (n_kchunks - 1) // 2, pair, 0)
        if n_kchunks % 2 == 0:
            scores(n_kchunks - 1, 1)
            accumulate([s_ref[h, 0] for h in heads], keys(n_kchunks - 2))
            accumulate([s_ref[h, 1] for h in heads], keys(n_kchunks - 1))
        else:
            accumulate([s_ref[h, 0] for h in heads], keys(n_kchunks - 1))

    lam_init = lv_ref[4:5, 0:1]
    lam = (jnp.exp(jnp.sum(lv_ref[0:1, :] * lv_ref[1:2, :], axis=1, keepdims=True))
           - jnp.exp(jnp.sum(lv_ref[2:3, :] * lv_ref[3:4, :], axis=1, keepdims=True)) + lam_init)
    for h in heads:
        a1 = acc_ref[h, 0:DIFF_TQ, :]
        a2 = acc_ref[h, DIFF_TQ:, :]
        o = a1[:, :DIFF_HD] / a1[:, DIFF_HD:] - lam * (a2[:, :DIFF_HD] / a2[:, DIFF_HD:])
        y = o * lax.rsqrt(jnp.mean(o * o, axis=1, keepdims=True) + EPS) * sw_ref[...]
        o_ref[:, h * DIFF_HD:(h + 1) * DIFF_HD] = y * (1.0 - lam_init)


def _diff_pallas(p, q_norm, k_norm, lq1, lk1, lq2, lk2, subln, lam_init):
    rows = p.shape[0]
    assert rows % DIFF_TK == 0 and rows % DIFF_TQ == 0 and CTX_LEN == DIFF_TQ
    q, k, v = _diff_prep(p, q_norm, k_norm)
    pad = lambda t: jnp.pad(t, (0, DIFF_HD - DIFF_HEAD_DIM))
    lvec = jnp.stack([pad(lq1), pad(lk1), pad(lq2), pad(lk2), jnp.full((DIFF_HD,), lam_init, F32),
                      jnp.zeros((DIFF_HD,), F32), jnp.zeros((DIFF_HD,), F32), jnp.zeros((DIFF_HD,), F32)])
    return pl.pallas_call(
        functools.partial(_diff_kernel, n_kchunks=rows // DIFF_TK),
        grid=(DIFF_HEADS // DIFF_HPS, rows // DIFF_TQ),
        in_specs=[pl.BlockSpec((DIFF_TQ, DIFF_HPS * DIFF_HD), lambda h, i: (i, h)),
                  pl.BlockSpec((rows, DIFF_HPS * DIFF_HD), lambda h, i: (0, h)),
                  pl.BlockSpec((rows, 2 * DIFF_HPS * DIFF_HD), lambda h, i: (0, h)),
                  pl.BlockSpec((8, DIFF_HD), lambda h, i: (0, 0)),
                  pl.BlockSpec((1, DIFF_HD), lambda h, i: (0, 0))],
        out_specs=pl.BlockSpec((DIFF_TQ, DIFF_HPS * DIFF_HD), lambda h, i: (i, h)),
        out_shape=jax.ShapeDtypeStruct((rows, GROUP_WIDTH), F32),
        scratch_shapes=[pltpu.VMEM((DIFF_HPS, 2 * DIFF_TQ, DIFF_HD), F32),
                        pltpu.VMEM((DIFF_HPS, 2 * DIFF_TQ, 2 * DIFF_HD), F32),
                        pltpu.VMEM((DIFF_HPS, 2, 2 * DIFF_TQ, DIFF_TK), F32)],
        compiler_params=pltpu.CompilerParams(
            dimension_semantics=("arbitrary", "arbitrary"), vmem_limit_bytes=VMEM_LIMIT_BYTES),
        name="diff_attention",
    )(q, k, v, lvec, subln.reshape(1, DIFF_HD))


GDN_TB = 256
GDN_HALO = 8
GDN_GATES = 2 * GDN_HEADS


def _softplus(x):
    return jnp.maximum(x, 0.0) + jnp.log(1.0 + jnp.exp(-jnp.abs(x)))


def _gdn_prep_kernel(*refs):
    (qp, qc, qn, kp, kc, kn, vp, vc, vn, ab_ref, w_ref, alog_ref, dtb_ref,
     qo_ref, ko_ref, vo_ref, go_ref, pad_ref) = refs
    i = pl.program_id(0)
    last = pl.num_programs(0) - 1
    prev_ok = (i >= 2).astype(F32)
    next_ok = jnp.logical_and(i >= 1, i < last).astype(F32)
    half = GDN_CONV // 2

    def conv_silu(prev_ref, cur_ref, next_ref, sec):
        pad_ref[0:GDN_HALO, :] = prev_ref[...] * prev_ok
        pad_ref[GDN_HALO:GDN_HALO + GDN_TB, :] = cur_ref[...]
        pad_ref[GDN_HALO + GDN_TB:, :] = next_ref[...] * next_ok
        acc = jnp.zeros((GDN_TB, GROUP_WIDTH), F32)
        for j in range(GDN_CONV):
            w = w_ref[j:j + 1, sec * GROUP_WIDTH:(sec + 1) * GROUP_WIDTH]
            acc = acc + pad_ref[GDN_HALO - half + j:GDN_HALO - half + j + GDN_TB, :] * w
        return acc * jax.nn.sigmoid(acc)

    def l2n(x):
        parts = []
        for h in range(GDN_HEADS):
            xh = x[:, h * GDN_HEAD_DIM:(h + 1) * GDN_HEAD_DIM]
            parts.append(xh * lax.rsqrt(jnp.sum(xh * xh, axis=1, keepdims=True) + EPS))
        return jnp.concatenate(parts, axis=1)

    qo_ref[...] = l2n(conv_silu(qp, qc, qn, 0)) * GDN_HEAD_DIM ** -0.5
    ko_ref[...] = l2n(conv_silu(kp, kc, kn, 1))
    vo_ref[...] = conv_silu(vp, vc, vn, 2)
    x = ab_ref[...]
    lane = lax.broadcasted_iota(jnp.int32, x.shape, 1)
    go_ref[...] = jnp.where(lane < GDN_GATES, -jnp.exp(alog_ref[...]) * _softplus(x + dtb_ref[...]),
                            jax.nn.sigmoid(x))


def _gdn_prep(p, conv_w, a_log, dt_bias):
    rows = p.shape[0]
    assert rows % GDN_TB == 0 and CTX_LEN == GDN_TB
    n_halo = rows // GDN_HALO
    per = GDN_TB // GDN_HALO
    c0 = P_GDN_QKV // GROUP_WIDTH
    specs = []
    for sec in range(3):
        specs += [
            pl.BlockSpec((GDN_HALO, GROUP_WIDTH), lambda i, sec=sec: (jnp.maximum(i * per - 1, 0), c0 + sec)),
            pl.BlockSpec((GDN_TB, GROUP_WIDTH), lambda i, sec=sec: (i, c0 + sec)),
            pl.BlockSpec((GDN_HALO, GROUP_WIDTH),
                         lambda i, sec=sec: (jnp.minimum((i + 1) * per, n_halo - 1), c0 + sec)),
        ]
    vec = pl.BlockSpec((1, 128), lambda i: (0, 0))
    specs += [pl.BlockSpec((GDN_TB, 128), lambda i: (i, P_GDN_AB // 128)),
              pl.BlockSpec((8, 3 * GROUP_WIDTH), lambda i: (0, 0)), vec, vec]
    blk = pl.BlockSpec((GDN_TB, GROUP_WIDTH), lambda i: (i, 0))
    out = jax.ShapeDtypeStruct((rows, GROUP_WIDTH), F32)
    pad8 = lambda t: jnp.pad(t.reshape(1, GDN_GATES), ((0, 0), (0, 128 - GDN_GATES)))
    return pl.pallas_call(
        _gdn_prep_kernel,
        grid=(rows // GDN_TB,),
        in_specs=specs,
        out_specs=[blk, blk, blk, pl.BlockSpec((GDN_TB, 128), lambda i: (i, 0))],
        out_shape=[out, out, out, jax.ShapeDtypeStruct((rows, 128), F32)],
        scratch_shapes=[pltpu.VMEM((GDN_TB + 2 * GDN_HALO, GROUP_WIDTH), F32)],
        compiler_params=pltpu.CompilerParams(
            dimension_semantics=("arbitrary",), vmem_limit_bytes=VMEM_LIMIT_BYTES),
        name="gdn_prep",
    )(*([p] * 10), jnp.pad(conv_w, ((0, 8 - GDN_CONV), (0, 0))), pad8(a_log), pad8(dt_bias))


def _mm3(a, b):
    ah = a.astype(BF16)
    al = (a - ah.astype(F32)).astype(BF16)
    bh = b.astype(BF16)
    bl = (b - bh.astype(F32)).astype(BF16)
    dot = lambda x, y: jnp.dot(x, y, preferred_element_type=F32)
    return dot(ah, bh) + dot(ah, bl) + dot(al, bh)


def _gdn_kernel(qf_ref, kf_ref, vf_ref, gf_ref, qb_ref, kb_ref, vb_ref, gb_ref, of_ref, ob_ref, s_ref):
    j = pl.program_id(0)
    tb, ck, hd = GDN_TB, GDN_CHUNK, GDN_HEAD_DIM
    n_ck = tb // ck

    @pl.when(j == 0)
    def _():
        s_ref[...] = jnp.zeros_like(s_ref)

    ri = lax.broadcasted_iota(jnp.int32, (tb, tb), 0)
    ci = lax.broadcasted_iota(jnp.int32, (tb, tb), 1)
    same = lambda n: (ri // n) == (ci // n)
    same_chunk = same(ck)
    eye = (ri == ci).astype(F32)
    dot = lambda x, y: jnp.dot(x, y, preferred_element_type=F32)
    tot_m = same_chunk.astype(BF16)

    def direction(reverse, g_ref):
        before = (ci > ri) if reverse else (ci < ri)
        strict = jnp.logical_and(same_chunk, before)
        incl = jnp.logical_and(same_chunk, jnp.logical_or(before, ri == ci))
        g = g_ref[...]
        g1 = g.astype(BF16)
        r1 = g - g1.astype(F32)
        g2 = r1.astype(BF16)
        g3 = (r1 - g2.astype(F32)).astype(BF16)
        cum_m = incl.astype(BF16)
        gcum = dot(cum_m, g1) + dot(cum_m, g2) + dot(cum_m, g3)
        gtot = dot(tot_m, g1) + dot(tot_m, g2) + dot(tot_m, g3)
        return dict(strict=strict, incl=incl, g=g, gcum=gcum, gtot=gtot, gcum_t=gcum.T,
                    order=list(range(n_ck - 1, -1, -1)) if reverse else list(range(n_ck)))

    dirs = [direction(False, gf_ref), direction(True, gb_ref)]
    refs = [(qf_ref, kf_ref, vf_ref, of_ref), (qb_ref, kb_ref, vb_ref, ob_ref)]
    chains = [(d, h) for d in range(2) for h in range(GDN_HEADS)]
    heads = range(len(chains))
    hs = [slice(h * hd, (h + 1) * hd) for _, h in chains]
    q_ref = [refs[d][0] for d, _ in chains]
    k_ref = [refs[d][1] for d, _ in chains]
    v_ref = [refs[d][2] for d, _ in chains]
    incl = [dirs[d]['incl'] for d, _ in chains]
    strict = [dirs[d]['strict'] for d, _ in chains]
    lanes = [d * GDN_HEADS + h for d, h in chains]
    gcol = [dirs[d]['gcum'][:, l:l + 1] for (d, _), l in zip(chains, lanes)]
    grow = [dirs[d]['gcum_t'][l:l + 1, :] for (d, _), l in zip(chains, lanes)]
    glast = [dirs[d]['gtot'][:, l:l + 1] for (d, _), l in zip(chains, lanes)]
    beta = [dirs[d]['g'][:, GDN_GATES + l:GDN_GATES + l + 1] for (d, _), l in zip(chains, lanes)]
    decay = [jnp.where(incl[h], jnp.exp(jnp.where(incl[h], gcol[h] - grow[h], 0.0)), 0.0) for h in heads]
    kb = [k_ref[h][:, hs[h]] * beta[h] for h in heads]
    k_b = [k_ref[h][:, hs[h]].astype(BF16) for h in heads]
    a_mat = [jnp.where(strict[h], _dot_nt(kb[h].astype(BF16), k_b[h]) * decay[h], 0.0) for h in heads]
    attn = [(_dot_nt(q_ref[h][:, hs[h]].astype(BF16), k_b[h]) * decay[h]).astype(BF16) for h in heads]

    x = [eye - jnp.where(same(2), a_mat[h], 0.0) for h in heads]
    for half_blk in (2, 4, 8, 16, 32):
        level = jnp.logical_and(same(2 * half_blk), jnp.logical_not(same(half_blk)))
        x_b = [x[h].astype(BF16) for h in heads]
        xl = [dot(x_b[h], jnp.where(level, a_mat[h], 0.0).astype(BF16)).astype(BF16) for h in heads]
        x = [x[h] - dot(xl[h], x_b[h]) for h in heads]
    resid = [eye - x[h] - _mm3(a_mat[h], x[h]) for h in heads]
    t_mat = [(x[h] + dot(x[h].astype(BF16), resid[h].astype(BF16))).astype(BF16) for h in heads]

    eg = [jnp.exp(gcol[h]) for h in heads]
    wu = [dot(t_mat[h], jnp.concatenate([kb[h] * eg[h], v_ref[h][:, hs[h]] * beta[h]], axis=1).astype(BF16))
          for h in heads]
    w_b = [wu[h][:, :hd].astype(BF16) for h in heads]
    qd_b = [(q_ref[h][:, hs[h]] * eg[h]).astype(BF16) for h in heads]
    k_carry = [(k_ref[h][:, hs[h]] * jnp.exp(glast[h] - gcol[h])).astype(BF16) for h in heads]

    s = [s_ref[h] for h in heads]
    v_new = [[None] * n_ck for _ in heads]
    o_inter = [[None] * n_ck for _ in heads]
    for step in range(n_ck):
        for h in heads:
            c = dirs[chains[h][0]]['order'][step]
            rows = slice(c * ck, (c + 1) * ck)
            s_b = s[h].astype(BF16)
            v_new[h][c] = wu[h][rows, hd:] - dot(w_b[h][rows], s_b)
            o_inter[h][c] = dot(qd_b[h][rows], s_b)
            s[h] = s[h] * jnp.exp(glast[h][c * ck:c * ck + 1, :]) + lax.dot_general(
                k_carry[h][rows], v_new[h][c].astype(BF16), (((0,), (0,)), ((), ())),
                preferred_element_type=F32)
    for h in heads:
        s_ref[h] = s[h]
        v_all = jnp.concatenate(v_new[h], axis=0).astype(BF16)
        refs[chains[h][0]][3][:, hs[h]] = jnp.concatenate(o_inter[h], axis=0) + dot(attn[h], v_all)


def _gdn_scan(q, k, v, gates):
    rows = q.shape[0]
    n_blocks = rows // GDN_TB
    bwd_of = lambda j: jnp.where(j == 0, 0, n_blocks - j)
    blk_f = pl.BlockSpec((GDN_TB, GROUP_WIDTH), lambda j: (j, 0))
    blk_b = pl.BlockSpec((GDN_TB, GROUP_WIDTH), lambda j: (bwd_of(j), 0))
    out = jax.ShapeDtypeStruct((rows, GROUP_WIDTH), F32)
    return pl.pallas_call(
        _gdn_kernel,
        grid=(n_blocks,),
        in_specs=[blk_f, blk_f, blk_f, pl.BlockSpec((GDN_TB, 128), lambda j: (j, 0)),
                  blk_b, blk_b, blk_b, pl.BlockSpec((GDN_TB, 128), lambda j: (bwd_of(j), 0))],
        out_specs=[blk_f, blk_b],
        out_shape=[out, out],
        scratch_shapes=[pltpu.VMEM((2 * GDN_HEADS, GDN_HEAD_DIM, GDN_HEAD_DIM), F32)],
        compiler_params=pltpu.CompilerParams(
            dimension_semantics=("arbitrary",), vmem_limit_bytes=VMEM_LIMIT_BYTES),
        name="gdn_scan",
    )(q, k, v, gates, q, k, v, gates)


def _gdn_finish_kernel(of_ref, ob_ref, z_ref, w_ref, y_ref):
    o = of_ref[...] + ob_ref[...]
    z = z_ref[...]
    parts = []
    for h in range(GDN_HEADS):
        oh = o[:, h * GDN_HEAD_DIM:(h + 1) * GDN_HEAD_DIM]
        parts.append(oh * lax.rsqrt(jnp.mean(oh * oh, axis=1, keepdims=True) + EPS) * w_ref[...])
    y_ref[...] = jnp.concatenate(parts, axis=1) * (z * jax.nn.sigmoid(z))


def _gdn_pallas(p, conv_w, a_log, dt_bias, norm_w):
    rows = p.shape[0]
    q, k, v, gates = _gdn_prep(p, conv_w, a_log, dt_bias)
    o_f, o_b = _gdn_scan(q, k, v, gates)
    blk = pl.BlockSpec((GDN_TB, GROUP_WIDTH), lambda i: (i, 0))
    return pl.pallas_call(
        _gdn_finish_kernel,
        grid=(rows // GDN_TB,),
        in_specs=[blk, blk, pl.BlockSpec((GDN_TB, GROUP_WIDTH), lambda i: (i, P_GDN_Z // GROUP_WIDTH)),
                  pl.BlockSpec((1, GDN_HEAD_DIM), lambda i: (0, 0))],
        out_specs=blk,
        out_shape=jax.ShapeDtypeStruct((rows, GROUP_WIDTH), F32),
        compiler_params=pltpu.CompilerParams(
            dimension_semantics=("arbitrary",), vmem_limit_bytes=VMEM_LIMIT_BYTES),
        name="gdn_finish",
    )(o_f, o_b, p, norm_w.reshape(1, GDN_HEAD_DIM))


def _reorder_w_in(w):
    front = w[..., 0:4096].astype(BF16)
    gdn_ab = w[..., 4096:4112].astype(BF16)
    na = w[..., 4112:5648].astype(BF16)
    pad = jnp.zeros(w.shape[:-1] + (P_WIDTH - P_GDN_AB - 16,), BF16)
    return jnp.concatenate([front, na, gdn_ab, pad], axis=-1)


def kernel(x, c, ctx, c_ctx, w_ada, b_ada, norm_ffn1, norm_mix, norm_ffn2, ffn1_w_in, ffn1_w_out,
           ffn2_w_in, ffn2_w_out, w_in, w_out, s5_lambda_re, s5_lambda_im, s5_log_step, s5_b_re,
           s5_b_im, s5_c_re, s5_c_im, s5_d, s5_w_glu, s5_b_glu, diff_q_norm, diff_k_norm,
           diff_lambda_q1, diff_lambda_k1, diff_lambda_q2, diff_lambda_k2, diff_subln, gdn_conv,
           gdn_a_log, gdn_dt_bias, gdn_norm, na_q_norm, na_k_norm, na_rpb):
    mod = _modulation(c, c_ctx, w_ada, b_ada)
    s = jnp.concatenate([ctx[0], x[0]], axis=0)
    ffn1_in, ffn1_out = ffn1_w_in.astype(BF16), ffn1_w_out.astype(BF16)
    ffn2_in, ffn2_out = ffn2_w_in.astype(BF16), ffn2_w_out.astype(BF16)
    w_in_bf, w_out_bf = _reorder_w_in(w_in), w_out.astype(BF16)
    for l in range(DEPTH):
        mod_l = mod[l]
        s = _ffn(s, mod_l, norm_ffn1[l], ffn1_in, ffn1_out, l, 0)
        p = _inproj(s, mod_l, norm_mix[l], w_in_bf, l)
        lam_init = 0.8 - 0.6 * math.exp(-0.3 * l)
        ya = _s5_pallas(p, s5_lambda_re[l], s5_lambda_im[l], s5_log_step[l], s5_b_re[l], s5_b_im[l],
                        s5_c_re[l], s5_c_im[l], s5_d[l], s5_w_glu[l], s5_b_glu[l])
        yb = _diff_pallas(p, diff_q_norm[l], diff_k_norm[l], diff_lambda_q1[l], diff_lambda_k1[l],
                          diff_lambda_q2[l], diff_lambda_k2[l], diff_subln[l], lam_init)
        yc = _gdn_pallas(p, gdn_conv[l], gdn_a_log[l], gdn_dt_bias[l], gdn_norm[l])
        yd = _na_pallas(p, na_q_norm[l], na_k_norm[l], na_rpb[l])
        s = _outproj(s, mod_l, [ya, yb, yc, yd], w_out_bf, l)
        s = _ffn(s, mod_l, norm_ffn2[l], ffn2_in, ffn2_out, l, 6)
    return s[None, CTX_LEN:]
```

```python
import functools
import math

import jax
import jax.numpy as jnp
from jax import lax
from jax.experimental import pallas as pl
from jax.experimental.pallas import tpu as pltpu

D_MODEL = 2048
SEQ = 8192
DEPTH = 4
GRID_W = 64
CTX_LEN = 256
ROWS = CTX_LEN + SEQ
GROUP_WIDTH = 512
D_FF = 5632
N_MOD = 9
EPS = 1e-6

S5_CH = 16
S5_GROUPS = GROUP_WIDTH // S5_CH
S5_STATE = 64
DIFF_HEADS = 4
DIFF_HEAD_DIM = 64
ROPE_BASE = 10000.0
Q_BLOCK = 128
GDN_HEADS = 4
GDN_HEAD_DIM = 128
GDN_CONV = 5
GDN_CHUNK = 64
NA_HEADS = 8
NA_HEAD_DIM = 64
WIN_H = 8
WIN_W = 16
NA_KEY_COLS = 2 * WIN_W

P_S5 = 0
P_DIFF = 512
P_GDN_QKV = 2048
P_GDN_Z = 3584
P_NA = 4096
P_GDN_AB = 5632
P_WIDTH = 5760

VMEM_LIMIT_BYTES = 56 * 1024 * 1024

F32 = jnp.float32
BF16 = jnp.bfloat16


MOD_TN = 1024


def _mod_kernel(ct_ref, w_ref, b_ref, o_ref):
    c = ct_ref[...]
    s = c * jax.nn.sigmoid(c)
    v0 = jnp.broadcast_to(s[:, 0:1], (D_MODEL, 128))
    v1 = jnp.broadcast_to(s[:, 1:2], (D_MODEL, 128))
    for j in range(MOD_TN // 128):
        w = w_ref[:, j * 128:(j + 1) * 128]
        b = b_ref[:, j * 128:(j + 1) * 128]
        o_ref[0:1, j * 128:(j + 1) * 128] = jnp.sum(w * v0, axis=0, keepdims=True) + b
        o_ref[1:2, j * 128:(j + 1) * 128] = jnp.sum(w * v1, axis=0, keepdims=True) + b


def _modulation(c, c_ctx, w_ada, b_ada):
    n = N_MOD * D_MODEL
    ct = jnp.stack([c.reshape(D_MODEL), c_ctx.reshape(D_MODEL)], axis=1)
    out = pl.pallas_call(
        _mod_kernel,
        grid=(DEPTH, n // MOD_TN),
        in_specs=[
            pl.BlockSpec((D_MODEL, 2), lambda l, j: (0, 0)),
            pl.BlockSpec((None, D_MODEL, MOD_TN), lambda l, j: (l, 0, j)),
            pl.BlockSpec((None, 1, MOD_TN), lambda l, j: (l, 0, j)),
        ],
        out_specs=pl.BlockSpec((None, 2, MOD_TN), lambda l, j: (l, 0, j)),
        out_shape=jax.ShapeDtypeStruct((DEPTH, 2, n), F32),
        compiler_params=pltpu.CompilerParams(
            dimension_semantics=("arbitrary", "arbitrary"), vmem_limit_bytes=VMEM_LIMIT_BYTES),
        name="adaln_mod",
    )(ct, w_ada, b_ada.reshape(DEPTH, 1, n))
    return out.reshape(DEPTH, 2 * N_MOD, D_MODEL)


def _is_ctx_rows(tile_rows):
    rows = pl.program_id(0) * tile_rows + lax.broadcasted_iota(jnp.int32, (tile_rows, 1), 0)
    return rows < CTX_LEN


def _mod_row(mod_ref, is_ctx, k):
    return jnp.where(is_ctx, mod_ref[N_MOD + k:N_MOD + k + 1, :], mod_ref[k:k + 1, :])


NORM_ROWS = 16


def _mod_chunk_row(mod_ref, first_row, k):
    off = jnp.where(first_row < CTX_LEN, N_MOD, 0)
    return mod_ref[pl.ds(off + k, 1), :]


def _fill_gain_shift(gamma_ref, mod_ref, gs_ref, base):
    for seg in range(2):
        gain = gamma_ref[...] * (1.0 + mod_ref[seg * N_MOD + base + 1:seg * N_MOD + base + 2, :])
        gs_ref[seg] = jnp.broadcast_to(gain, (NORM_ROWS, D_MODEL))
        gs_ref[2 + seg] = jnp.broadcast_to(mod_ref[seg * N_MOD + base:seg * N_MOD + base + 1, :],
                                           (NORM_ROWS, D_MODEL))


def _adaln_rows(x_ref, h_ref, gs_ref, tile_row0, lo, n_rows):
    for t in range(n_rows // NORM_ROWS):
        sl = slice(lo + t * NORM_ROWS, lo + (t + 1) * NORM_ROWS)
        seg = (tile_row0 + sl.start < CTX_LEN).astype(jnp.int32)
        x = x_ref[sl, :]
        ms = jnp.mean(x * x, axis=-1, keepdims=True)
        h = x * lax.rsqrt(ms + EPS) * gs_ref[seg] + gs_ref[2 + seg]
        h_ref[sl, :] = h.astype(BF16)


FFN_TM = 1408
FFN_RC = 352
FFN_TF = 512


def _ffn_kernel(x_ref, mod_ref, gamma_ref, wg_ref, wu_ref, wo_ref, o_ref, h_ref, gs_ref, *, base):
    f = pl.program_id(1)
    last = pl.num_programs(1) - 1
    row0 = pl.program_id(0) * FFN_TM
    n_chunks = FFN_TM // FFN_RC

    def down(r):
        rows = pl.ds(r * FFN_RC, FFN_RC)
        h = h_ref[rows, :]
        g = jnp.dot(h, wg_ref[...], preferred_element_type=F32)
        u = jnp.dot(h, wu_ref[...], preferred_element_type=F32)
        a = (g * jax.nn.sigmoid(g) * u).astype(BF16)
        return rows, jnp.dot(a, wo_ref[...], preferred_element_type=F32)

    @pl.when(f == 0)
    def _():
        _fill_gain_shift(gamma_ref, mod_ref, gs_ref, base)
        _adaln_rows(x_ref, h_ref, gs_ref, row0, 0, FFN_RC)
        for r in range(n_chunks):
            if r + 1 < n_chunks:
                _adaln_rows(x_ref, h_ref, gs_ref, row0, (r + 1) * FFN_RC, FFN_RC)
            rows, d = down(r)
            o_ref[rows, :] = d

    @pl.when(jnp.logical_and(f > 0, f < last))
    def _():
        for r in range(n_chunks):
            rows, d = down(r)
            o_ref[rows, :] += d

    @pl.when(f == last)
    def _():
        for r in range(n_chunks):
            _, d = down(r)
            for t in range(FFN_RC // NORM_ROWS):
                lo = r * FFN_RC + t * NORM_ROWS
                sl = slice(lo, lo + NORM_ROWS)
                gate = _mod_chunk_row(mod_ref, row0 + lo, base + 2)
                o_ref[sl, :] = x_ref[sl, :] + 0.5 * gate * (o_ref[sl, :] + d[t * NORM_ROWS:(t + 1) * NORM_ROWS])


def _ffn(s, mod_l, gamma, w_in_bf, w_out_bf, layer, base):
    nf = D_FF // FFN_TF
    return pl.pallas_call(
        functools.partial(_ffn_kernel, base=base),
        grid=(ROWS // FFN_TM, nf),
        in_specs=[
            pl.BlockSpec((FFN_TM, D_MODEL), lambda i, f: (i, 0)),
            pl.BlockSpec((2 * N_MOD, D_MODEL), lambda i, f: (0, 0)),
            pl.BlockSpec((1, D_MODEL), lambda i, f: (0, 0)),
            pl.BlockSpec((None, D_MODEL, FFN_TF), lambda i, f: (layer, 0, f)),
            pl.BlockSpec((None, D_MODEL, FFN_TF), lambda i, f: (layer, 0, nf + f)),
            pl.BlockSpec((None, FFN_TF, D_MODEL), lambda i, f: (layer, f, 0)),
        ],
        out_specs=pl.BlockSpec((FFN_TM, D_MODEL), lambda i, f: (i, 0), pipeline_mode=pl.Buffered(1)),
        out_shape=jax.ShapeDtypeStruct((ROWS, D_MODEL), F32),
        scratch_shapes=[pltpu.VMEM((FFN_TM, D_MODEL), BF16), pltpu.VMEM((4, NORM_ROWS, D_MODEL), F32)],
        compiler_params=pltpu.CompilerParams(
            dimension_semantics=("arbitrary", "arbitrary"), vmem_limit_bytes=VMEM_LIMIT_BYTES),
        name="ffn_swiglu",
    )(s, mod_l, gamma.reshape(1, D_MODEL), w_in_bf, w_in_bf, w_out_bf)


INP_TM = 1408
INP_RC = 352
INP_TN = 1152


def _inproj_kernel(x_ref, mod_ref, gamma_ref, w_ref, o_ref, h_ref, gs_ref):
    n = pl.program_id(1)
    row0 = pl.program_id(0) * INP_TM
    n_chunks = INP_TM // INP_RC

    def project(r):
        rows = pl.ds(r * INP_RC, INP_RC)
        o_ref[rows, :] = jnp.dot(h_ref[rows, :], w_ref[...], preferred_element_type=F32)

    @pl.when(n == 0)
    def _():
        _fill_gain_shift(gamma_ref, mod_ref, gs_ref, 3)
        _adaln_rows(x_ref, h_ref, gs_ref, row0, 0, INP_RC)
        for r in range(n_chunks):
            if r + 1 < n_chunks:
                _adaln_rows(x_ref, h_ref, gs_ref, row0, (r + 1) * INP_RC, INP_RC)
            project(r)

    @pl.when(n > 0)
    def _():
        for r in range(n_chunks):
            project(r)


def _inproj(s, mod_l, gamma, w_bf, layer):
    return pl.pallas_call(
        _inproj_kernel,
        grid=(ROWS // INP_TM, P_WIDTH // INP_TN),
        in_specs=[
            pl.BlockSpec((INP_TM, D_MODEL), lambda i, n: (i, 0)),
            pl.BlockSpec((2 * N_MOD, D_MODEL), lambda i, n: (0, 0)),
            pl.BlockSpec((1, D_MODEL), lambda i, n: (0, 0)),
            pl.BlockSpec((None, D_MODEL, INP_TN), lambda i, n: (layer, 0, n)),
        ],
        out_specs=pl.BlockSpec((INP_TM, INP_TN), lambda i, n: (i, n)),
        out_shape=jax.ShapeDtypeStruct((ROWS, P_WIDTH), F32),
        scratch_shapes=[pltpu.VMEM((INP_TM, D_MODEL), BF16), pltpu.VMEM((4, NORM_ROWS, D_MODEL), F32)],
        compiler_params=pltpu.CompilerParams(
            dimension_semantics=("arbitrary", "arbitrary"), vmem_limit_bytes=VMEM_LIMIT_BYTES),
        name="in_proj",
    )(s, mod_l, gamma.reshape(1, D_MODEL), w_bf)


OUT_TM = 384


def _outproj_kernel(x_ref, mod_ref, ya_ref, yb_ref, yc_ref, yd_ref, w_ref, o_ref):
    is_ctx = _is_ctx_rows(OUT_TM)
    acc = jnp.zeros((OUT_TM, D_MODEL), F32)
    for k, y_ref in enumerate((ya_ref, yb_ref, yc_ref, yd_ref)):
        acc += jnp.dot(y_ref[...].astype(BF16), w_ref[k * GROUP_WIDTH:(k + 1) * GROUP_WIDTH, :],
                       preferred_element_type=F32)
    o_ref[...] = x_ref[...] + _mod_row(mod_ref, is_ctx, 5) * acc


def _outproj(s, mod_l, ys, w_bf, layer):
    yspec = pl.BlockSpec((OUT_TM, GROUP_WIDTH), lambda i: (i, 0))
    return pl.pallas_call(
        _outproj_kernel,
        grid=(ROWS // OUT_TM,),
        in_specs=[
            pl.BlockSpec((OUT_TM, D_MODEL), lambda i: (i, 0)),
            pl.BlockSpec((2 * N_MOD, D_MODEL), lambda i: (0, 0)),
            yspec, yspec, yspec, yspec,
            pl.BlockSpec((None, D_MODEL, D_MODEL), lambda i: (layer, 0, 0)),
        ],
        out_specs=pl.BlockSpec((OUT_TM, D_MODEL), lambda i: (i, 0)),
        out_shape=jax.ShapeDtypeStruct((ROWS, D_MODEL), F32),
        compiler_params=pltpu.CompilerParams(
            dimension_semantics=("arbitrary",), vmem_limit_bytes=VMEM_LIMIT_BYTES),
        name="out_proj",
    )(s, mod_l, *ys, w_bf)


S5_TC = 256


def _s5_prepare(lam_re, lam_im, log_step, b_re, b_im, c_re, c_im):
    dt = jnp.exp(log_step)[..., None]
    mag = jnp.exp(lam_re * dt)
    ar = mag * jnp.cos(lam_im * dt)
    ai = mag * jnp.sin(lam_im * dt)
    den = lam_re * lam_re + lam_im * lam_im
    fr = ((ar - 1.0) * lam_re + ai * lam_im) / den
    fi = (ai * lam_re - (ar - 1.0) * lam_im) / den
    bbr = fr[..., None] * b_re - fi[..., None] * b_im
    bbi = fr[..., None] * b_im + fi[..., None] * b_re
    in_oct = jnp.eye(4, dtype=F32)[jnp.arange(S5_GROUPS) % 4]
    half = 4 * S5_STATE

    def w_in_half(bb):
        return jnp.einsum('dgpn,gj->dgnjp', bb, in_oct).reshape(2, GROUP_WIDTH, half)

    def w_out_half(cc):
        return jnp.einsum('dgcp,gj->djpgc', cc, in_oct).reshape(2, half, GROUP_WIDTH)

    w_in = jnp.concatenate([w_in_half(bbr), w_in_half(bbi)], axis=2)
    w_out = jnp.concatenate([w_out_half(c_re), w_out_half(-c_im)], axis=1)
    return (ar.reshape(2, 8, half), ai.reshape(2, 8, half), w_in.astype(BF16), w_out.astype(BF16))


def _s5_scan_kernel(uf_ref, ub_ref, win_ref, wout_ref, ar_ref, ai_ref, yf_ref, yb_ref, buf_ref, h_ref):
    j = pl.program_id(0)
    tc = S5_TC
    lane_blk = lambda b: slice(b * 128, (b + 1) * 128)

    @pl.when(j == 0)
    def _():
        buf_ref[...] = jnp.zeros_like(buf_ref)

    @pl.when(j <= 1)
    def _():
        h_ref[...] = jnp.zeros_like(h_ref)

    lo_half = lax.broadcasted_iota(jnp.int32, (tc, 128), 1) < 64
    u_refs = (uf_ref, ub_ref)
    y_refs = (yf_ref, yb_ref)
    for v in range(3):
        pl.when(lax.rem(j, 3) == v)(functools.partial(
            _s5_step, u_refs, y_refs, win_ref, wout_ref, ar_ref, ai_ref, buf_ref, h_ref, lo_half,
            v, (v + 2) % 3, (v + 1) % 3))


def _s5_step(u_refs, y_refs, win_ref, wout_ref, ar_ref, ai_ref, buf_ref, h_ref, lo_half,
             slot_in, slot_scan, slot_out):
    tc = S5_TC
    lane_blk = lambda b: slice(b * 128, (b + 1) * 128)

    def project_in(d, b, e):
        mask = lo_half if e == 0 else jnp.logical_not(lo_half)
        lhs = jnp.where(mask, u_refs[d][:, lane_blk(b)], 0.0).astype(BF16)
        res = jnp.dot(lhs, win_ref[d, lane_blk(b), :], preferred_element_type=F32)
        for s in range(4):
            buf_ref[slot_in, d, s, pl.ds(2 * b + e, tc, stride=8), :] = res[:, lane_blk(s)]

    def project_out(d, b):
        res = []
        for q in (2 * b, 2 * b + 1):
            states = jnp.concatenate([buf_ref[slot_out, d, s, pl.ds(q, tc, stride=8), :] for s in range(4)],
                                     axis=1)
            res.append(jnp.dot(states.astype(BF16), wout_ref[d, :, lane_blk(b)], preferred_element_type=F32))
        y_refs[d][:, lane_blk(b)] = jnp.where(lo_half, res[0], res[1])

    a_re = (ar_ref[0], ar_ref[1])
    a_im = (ai_ref[0], ai_ref[1])

    def scan_step(d, t, hr, hi):
        rows = slice(8 * t, 8 * t + 8)
        br = jnp.concatenate([buf_ref[slot_scan, d, 0, rows, :], buf_ref[slot_scan, d, 1, rows, :]], axis=1)
        bi = jnp.concatenate([buf_ref[slot_scan, d, 2, rows, :], buf_ref[slot_scan, d, 3, rows, :]], axis=1)
        nr = a_re[d] * hr - a_im[d] * hi + br
        ni = a_re[d] * hi + a_im[d] * hr + bi
        buf_ref[slot_scan, d, 0, rows, :] = nr[:, :128]
        buf_ref[slot_scan, d, 1, rows, :] = nr[:, 128:]
        buf_ref[slot_scan, d, 2, rows, :] = ni[:, :128]
        buf_ref[slot_scan, d, 3, rows, :] = ni[:, 128:]
        return nr, ni

    state = [(h_ref[0], h_ref[1]), (h_ref[2], h_ref[3])]
    n_phase = 16
    per_phase = tc // n_phase
    for ph in range(n_phase):
        d, b, e = ph // 8, (ph % 8) // 2, ph % 2
        project_in(d, b, e)
        if e == 1:
            project_out(d, b)
        for t in range(ph * per_phase, (ph + 1) * per_phase):
            state[0] = scan_step(0, t, *state[0])
            state[1] = scan_step(1, tc - 1 - t, *state[1])
    h_ref[0], h_ref[1] = state[0]
    h_ref[2], h_ref[3] = state[1]


def _s5_scan_call(p, a_r, a_i, w_in, w_out):
    rows = p.shape[0]
    n_chunks = rows // S5_TC
    fwd_in = lambda j: jnp.minimum(j, n_chunks - 1)
    fwd_out = lambda j: jnp.clip(j - 2, 0, n_chunks - 1)
    bwd = lambda c: jnp.where(c == 0, 0, n_chunks - c)
    full = lambda shape: pl.BlockSpec(shape, lambda j: (0,) * len(shape))
    y = jax.ShapeDtypeStruct((rows, GROUP_WIDTH), F32)
    return pl.pallas_call(
        _s5_scan_kernel,
        grid=(n_chunks + 2,),
        in_specs=[
            pl.BlockSpec((S5_TC, GROUP_WIDTH), lambda j: (fwd_in(j), 0)),
            pl.BlockSpec((S5_TC, GROUP_WIDTH), lambda j: (bwd(fwd_in(j)), 0)),
            full((2, GROUP_WIDTH, GROUP_WIDTH)), full((2, GROUP_WIDTH, GROUP_WIDTH)),
            full((2, 8, 256)), full((2, 8, 256)),
        ],
        out_specs=[pl.BlockSpec((S5_TC, GROUP_WIDTH), lambda j: (fwd_out(j), 0)),
                   pl.BlockSpec((S5_TC, GROUP_WIDTH), lambda j: (bwd(fwd_out(j)), 0))],
        out_shape=[y, y],
        scratch_shapes=[
            pltpu.VMEM((3, 2, 4, 8 * S5_TC, 128), F32),
            pltpu.VMEM((4, 8, 256), F32),
        ],
        compiler_params=pltpu.CompilerParams(
            dimension_semantics=("arbitrary",), vmem_limit_bytes=VMEM_LIMIT_BYTES),
        name="s5_scan",
    )(p, p, w_in, w_out, a_r, a_i)


S5_FIN_TM = 384


def _s5_finish_kernel(yf_ref, yb_ref, u_ref, d_ref, w_ref, b_ref, o_ref):
    y = yf_ref[...] + yb_ref[...] + d_ref[...] * u_ref[...]
    h = jnp.dot(jax.nn.gelu(y).astype(BF16), w_ref[...], preferred_element_type=F32) + b_ref[...]
    o_ref[...] = h[:, :GROUP_WIDTH] * jax.nn.sigmoid(h[:, GROUP_WIDTH:])


def _s5_finish(yf, yb, p, d_skip, w_glu_bf, b_glu):
    rows = p.shape[0]
    row_blk = pl.BlockSpec((S5_FIN_TM, GROUP_WIDTH), lambda i: (i, 0))
    return pl.pallas_call(
        _s5_finish_kernel,
        grid=(rows // S5_FIN_TM,),
        in_specs=[row_blk, row_blk, row_blk,
                  pl.BlockSpec((1, GROUP_WIDTH), lambda i: (0, 0)),
                  pl.BlockSpec((GROUP_WIDTH, 2 * GROUP_WIDTH), lambda i: (0, 0)),
                  pl.BlockSpec((1, 2 * GROUP_WIDTH), lambda i: (0, 0))],
        out_specs=row_blk,
        out_shape=jax.ShapeDtypeStruct((rows, GROUP_WIDTH), F32),
        compiler_params=pltpu.CompilerParams(
            dimension_semantics=("arbitrary",), vmem_limit_bytes=VMEM_LIMIT_BYTES),
        name="s5_finish",
    )(yf, yb, p, d_skip.reshape(1, GROUP_WIDTH), w_glu_bf, b_glu.reshape(1, 2 * GROUP_WIDTH))


def _s5_pallas(p, lam_re, lam_im, log_step, b_re, b_im, c_re, c_im, d_skip, w_glu, b_glu):
    a_r, a_i, w_in, w_out = _s5_prepare(lam_re, lam_im, log_step, b_re, b_im, c_re, c_im)
    yf, yb = _s5_scan_call(p, a_r, a_i, w_in, w_out)
    return _s5_finish(yf, yb, p, d_skip, w_glu.astype(BF16), b_glu)


PREP_TM = 256
PREP_RC = 128


def _segment_ones(width, seg):
    i = jnp.arange(width) // seg
    return (i[:, None] == i[None, :]).astype(BF16)


def _seg_sumsq(x, e):
    x2 = x * x
    hi = x2.astype(BF16)
    lo = (x2 - hi.astype(F32)).astype(BF16)
    return jnp.dot(hi, e, preferred_element_type=F32) + jnp.dot(lo, e, preferred_element_type=F32)


NA_QROWS = 4
NA_KROWS = NA_QROWS + WIN_H
NA_NQ = NA_QROWS * GRID_W
NA_NK = NA_KROWS * GRID_W
NA_PAIRS = 2
MASKED = -1e30


def _na_prep_kernel(q_ref, k_ref, v_ref, e_ref, qw_ref, kw_ref, qo_ref, ko_ref, vo_ref):
    def body(c, carry):
        sl = pl.ds(pl.multiple_of(c * PREP_RC, PREP_RC), PREP_RC)
        q = q_ref[sl, :]
        k = k_ref[sl, :]
        qn = q * lax.rsqrt(_seg_sumsq(q, e_ref[...]) * (1.0 / NA_HEAD_DIM) + EPS) * qw_ref[...]
        kn = k * lax.rsqrt(_seg_sumsq(k, e_ref[...]) * (1.0 / NA_HEAD_DIM) + EPS) * kw_ref[...]
        qo_ref[sl, :] = (qn * NA_HEAD_DIM ** -0.5).astype(BF16)
        ko_ref[sl, :] = kn.astype(BF16)
        vo_ref[sl, :] = v_ref[sl, :].astype(BF16)
        return carry

    lax.fori_loop(0, PREP_TM // PREP_RC, body, 0)


def _na_prep(p, q_norm, k_norm):
    rows = p.shape[0]
    assert rows % PREP_TM == 0
    c0 = P_NA // GROUP_WIDTH
    col = lambda j: pl.BlockSpec((PREP_TM, GROUP_WIDTH), lambda i: (i, c0 + j))
    vec = pl.BlockSpec((1, GROUP_WIDTH), lambda i: (0, 0))
    out = jax.ShapeDtypeStruct((rows, GROUP_WIDTH), BF16)
    blk = pl.BlockSpec((PREP_TM, GROUP_WIDTH), lambda i: (i, 0))
    return pl.pallas_call(
        _na_prep_kernel,
        grid=(rows // PREP_TM,),
        in_specs=[col(0), col(1), col(2),
                  pl.BlockSpec((GROUP_WIDTH, GROUP_WIDTH), lambda i: (0, 0)), vec, vec],
        out_specs=[blk, blk, blk],
        out_shape=[out, out, out],
        compiler_params=pltpu.CompilerParams(
            dimension_semantics=("arbitrary",), vmem_limit_bytes=VMEM_LIMIT_BYTES),
        name="na_prep",
    )(p, p, p, _segment_ones(GROUP_WIDTH, NA_HEAD_DIM),
      jnp.tile(q_norm, NA_HEADS).reshape(1, GROUP_WIDTH), jnp.tile(k_norm, NA_HEADS).reshape(1, GROUP_WIDTH))


def _na_bias_table(rpb, n_grid_rows):
    import numpy as np
    c = np.arange(GRID_W)
    col0 = np.clip(c - WIN_W // 2, 0, GRID_W - WIN_W)
    kc = np.arange(GRID_W)
    col_ok = (kc[None, :] >= col0[:, None]) & (kc[None, :] < col0[:, None] + WIN_W)
    dcol = kc[None, :] - c[:, None] + WIN_W - 1
    shift = ((dcol[None] == np.arange(2 * WIN_W - 1)[:, None, None]) & col_ok[None]).astype(np.float32)
    jr = np.arange(NA_QROWS)[:, None]
    kr = np.arange(NA_KROWS)[None, :]
    ri = np.stack([kr - jr + WIN_H - 1, kr - jr + WIN_H - 1 - WIN_H // 2, kr - jr + WIN_H - 1 - WIN_H])
    ok = np.stack([(kr < WIN_H) & (jr >= 0), (kr - jr >= 0) & (kr - jr < WIN_H), (kr >= NA_QROWS) & (jr >= 0)])
    valid = ok[:, :, None, :, None] & col_ok[None, None, :, None, :]
    neg = np.where(valid, 0.0, MASKED).astype(np.float32).reshape(3, 1, NA_NQ, NA_NK)
    hi = lax.Precision.HIGHEST
    band = jnp.einsum('hrj,jck->hrck', rpb, jnp.asarray(shift), precision=hi)
    zero = jnp.zeros_like(band[:, :1])
    pairs = jnp.concatenate([jnp.concatenate([zero, band], axis=1), jnp.concatenate([band, zero], axis=1)],
                            axis=-1)
    sel = (ri[:, :, 0::2, None] + 1 == np.arange(2 * WIN_H)).astype(np.float32)
    tab = jnp.einsum('vjmr,hrcy->vhjcmy', jnp.asarray(sel), pairs, precision=hi)
    return tab.reshape(3, NA_HEADS, NA_NQ, NA_NK) + jnp.asarray(neg)


def _softmax_pv(scores, values):
    heads = range(len(scores))
    m = [functools.reduce(jnp.maximum, [jnp.max(s, axis=1, keepdims=True) for s in scores[h]]) for h in heads]
    ps = [[jnp.exp(s - m[h]) for s in scores[h]] for h in heads]
    denom = [functools.reduce(lambda a, b: a + b, [jnp.sum(p, axis=1, keepdims=True) for p in ps[h]])
             for h in heads]
    acc = [functools.reduce(lambda a, b: a + b,
                            [jnp.dot(p.astype(BF16), v, preferred_element_type=F32)
                             for p, v in zip(ps[h], values[h])]) for h in heads]
    return [acc[h] / denom[h] for h in heads]


def _dot_nt(a, b):
    return lax.dot_general(a, b, (((1,), (1,)), ((), ())), preferred_element_type=F32)


def _na_kernel(q_ref, k_ref, v_ref, tab_ref, o_ref, *, n_blocks):
    i = pl.program_id(1)
    lo_half = lax.broadcasted_iota(jnp.int32, (NA_NQ, 128), 1) < NA_HEAD_DIM
    lanes = [slice(pr * 128, (pr + 1) * 128) for pr in range(NA_PAIRS)]

    def heads_of(pr):
        q = q_ref[:, lanes[pr]]
        zero = jnp.zeros_like(q)
        return (jnp.where(lo_half, q, zero), jnp.where(lo_half, zero, q))

    @pl.when(i == 0)
    def _():
        for pr in range(NA_PAIRS):
            kc, vc = k_ref[0:CTX_LEN, lanes[pr]], v_ref[0:CTX_LEN, lanes[pr]]
            outs = _softmax_pv([[_dot_nt(qh, kc)] for qh in heads_of(pr)], [[vc], [vc]])
            o_ref[:, lanes[pr]] = jnp.where(lo_half, outs[0], outs[1])

    @pl.when(i > 0)
    def _():
        ib = i - 1
        kr0 = jnp.clip(NA_QROWS * ib - WIN_H // 2, 0, NA_QROWS * n_blocks - NA_KROWS)
        win = pl.ds(pl.multiple_of(CTX_LEN + kr0 * GRID_W, GRID_W), NA_NK)
        variant = jnp.where(ib == 0, 0, jnp.where(ib == n_blocks - 1, 2, 1))
        for pr in range(NA_PAIRS):
            kc, vc = k_ref[0:CTX_LEN, lanes[pr]], v_ref[0:CTX_LEN, lanes[pr]]
            kwin, vwin = k_ref[win, lanes[pr]], v_ref[win, lanes[pr]]
            outs = []
            for e, qh in enumerate(heads_of(pr)):
                s_loc = _dot_nt(qh, kwin) + tab_ref[variant, pr, e]
                outs += _softmax_pv([[s_loc, _dot_nt(qh, kc)]], [[vwin, vc]])
            o_ref[:, lanes[pr]] = jnp.where(lo_half, outs[0], outs[1])


def _na_pallas(p, q_norm, k_norm, rpb):
    rows = p.shape[0]
    n_grid_rows = (rows - CTX_LEN) // GRID_W
    n_blocks = n_grid_rows // NA_QROWS
    q, k, v = _na_prep(p, q_norm, k_norm)
    width = 128 * NA_PAIRS
    tab = _na_bias_table(rpb, n_grid_rows).reshape(3, NA_HEADS // 2, 2, NA_NQ, NA_NK)
    kv = pl.BlockSpec((rows, width), lambda h, i: (0, h))
    return pl.pallas_call(
        functools.partial(_na_kernel, n_blocks=n_blocks),
        grid=(NA_HEADS // (2 * NA_PAIRS), n_blocks + 1),
        in_specs=[pl.BlockSpec((NA_NQ, width), lambda h, i: (i, h)), kv, kv,
                  pl.BlockSpec((3, NA_PAIRS, 2, NA_NQ, NA_NK), lambda h, i: (0, h, 0, 0, 0))],
        out_specs=pl.BlockSpec((NA_NQ, width), lambda h, i: (i, h)),
        out_shape=jax.ShapeDtypeStruct((rows, GROUP_WIDTH), F32),
        compiler_params=pltpu.CompilerParams(
            dimension_semantics=("arbitrary", "arbitrary"), vmem_limit_bytes=VMEM_LIMIT_BYTES),
        name="na_attention",
    )(q, k, v, tab)


DIFF_TQ = 256
DIFF_TK = 1408
DIFF_HD = 2 * DIFF_HEAD_DIM
DIFF_HPS = 2


def _rope_tables_rows(rows):
    pos = jnp.arange(rows - CTX_LEN)
    row = (pos // GRID_W).astype(F32)
    col = (pos % GRID_W).astype(F32)
    n_freq = DIFF_HEAD_DIM // 4
    inv = ROPE_BASE ** (-jnp.arange(n_freq, dtype=F32) / n_freq)
    ang = jnp.concatenate([row[:, None] * inv, row[:, None] * inv, col[:, None] * inv, col[:, None] * inv], -1)
    sign = jnp.tile(jnp.repeat(jnp.array([-1.0, 1.0, -1.0, 1.0], F32), n_freq), 2)
    cos = jnp.concatenate([jnp.ones((CTX_LEN, DIFF_HD), F32), jnp.tile(jnp.cos(ang), (1, 2))], axis=0)
    sin = jnp.concatenate([jnp.zeros((CTX_LEN, DIFF_HD), F32), jnp.tile(jnp.sin(ang), (1, 2)) * sign], axis=0)
    return cos, sin


def _diff_prep_kernel(q_ref, k_ref, v_ref, e_ref, qw_ref, kw_ref, cos_ref, sin_ref, qo_ref, ko_ref, vo_ref):
    quarter = lax.broadcasted_iota(jnp.int32, (PREP_RC, GROUP_WIDTH), 1) // (DIFF_HEAD_DIM // 4)
    first_of_pair = quarter % 2 == 0

    def rope(x, cos, sin):
        partner = jnp.where(first_of_pair, pltpu.roll(x, GROUP_WIDTH - DIFF_HEAD_DIM // 4, 1),
                            pltpu.roll(x, DIFF_HEAD_DIM // 4, 1))
        return x * cos + partner * sin

    def body(c, carry):
        sl = pl.ds(pl.multiple_of(c * PREP_RC, PREP_RC), PREP_RC)
        cos = jnp.concatenate([cos_ref[sl, :]] * DIFF_HEADS, axis=1)
        sin = jnp.concatenate([sin_ref[sl, :]] * DIFF_HEADS, axis=1)
        q = q_ref[sl, :]
        k = k_ref[sl, :]
        qn = q * lax.rsqrt(_seg_sumsq(q, e_ref[...]) * (1.0 / DIFF_HEAD_DIM) + EPS) * qw_ref[...]
        kn = k * lax.rsqrt(_seg_sumsq(k, e_ref[...]) * (1.0 / DIFF_HEAD_DIM) + EPS) * kw_ref[...]
        qo_ref[sl, :] = (rope(qn, cos, sin) * DIFF_HEAD_DIM ** -0.5).astype(BF16)
        ko_ref[sl, :] = rope(kn, cos, sin).astype(BF16)
        v = v_ref[sl, :].astype(BF16)
        ones = jnp.ones((PREP_RC, DIFF_HD), BF16)
        for h in range(DIFF_HEADS):
            vo_ref[sl, 2 * h * DIFF_HD:(2 * h + 1) * DIFF_HD] = v[:, h * DIFF_HD:(h + 1) * DIFF_HD]
            vo_ref[sl, (2 * h + 1) * DIFF_HD:(2 * h + 2) * DIFF_HD] = ones
        return carry

    lax.fori_loop(0, PREP_TM // PREP_RC, body, 0)


def _diff_prep(p, q_norm, k_norm):
    rows = p.shape[0]
    assert rows % PREP_TM == 0
    c0 = P_DIFF // GROUP_WIDTH
    col = lambda j: pl.BlockSpec((PREP_TM, GROUP_WIDTH), lambda i: (i, c0 + j))
    vec = pl.BlockSpec((1, GROUP_WIDTH), lambda i: (0, 0))
    tab = pl.BlockSpec((PREP_TM, DIFF_HD), lambda i: (i, 0))
    blk = pl.BlockSpec((PREP_TM, GROUP_WIDTH), lambda i: (i, 0))
    cos, sin = _rope_tables_rows(rows)
    return pl.pallas_call(
        _diff_prep_kernel,
        grid=(rows // PREP_TM,),
        in_specs=[col(0), col(1), col(2),
                  pl.BlockSpec((GROUP_WIDTH, GROUP_WIDTH), lambda i: (0, 0)), vec, vec, tab, tab],
        out_specs=[blk, blk, pl.BlockSpec((PREP_TM, 2 * GROUP_WIDTH), lambda i: (i, 0))],
        out_shape=[jax.ShapeDtypeStruct((rows, GROUP_WIDTH), BF16), jax.ShapeDtypeStruct((rows, GROUP_WIDTH), BF16),
                   jax.ShapeDtypeStruct((rows, 2 * GROUP_WIDTH), BF16)],
        compiler_params=pltpu.CompilerParams(
            dimension_semantics=("arbitrary",), vmem_limit_bytes=VMEM_LIMIT_BYTES),
        name="diff_prep",
    )(p, p, p, _segment_ones(GROUP_WIDTH, DIFF_HEAD_DIM),
      jnp.tile(q_norm, 2 * DIFF_HEADS).reshape(1, GROUP_WIDTH),
      jnp.tile(k_norm, 2 * DIFF_HEADS).reshape(1, GROUP_WIDTH), cos, sin)


def _diff_kernel(q_ref, k_ref, v_ref, lv_ref, sw_ref, o_ref, m_ref, acc_ref, s_ref, *, n_kchunks):
    i = pl.program_id(1)
    heads = range(DIFF_HPS)
    lo_half = lax.broadcasted_iota(jnp.int32, (DIFF_TQ, DIFF_HD), 1) < DIFF_HEAD_DIM
    q2 = []
    for h in heads:
        q = q_ref[:, h * DIFF_HD:(h + 1) * DIFF_HD]
        zero = jnp.zeros_like(q)
        q2.append(jnp.concatenate([jnp.where(lo_half, q, zero), jnp.where(lo_half, zero, q)], axis=0))

    m_ref[...] = jnp.full(m_ref.shape, MASKED, F32)
    acc_ref[...] = jnp.zeros_like(acc_ref)

    def keys(c):
        return pl.ds(pl.multiple_of(c * DIFF_TK, DIFF_TK), DIFF_TK)

    def k_of(h, rows):
        return k_ref[rows, h * DIFF_HD:(h + 1) * DIFF_HD]

    def v_of(h, rows):
        return v_ref[rows, 2 * h * DIFF_HD:2 * (h + 1) * DIFF_HD]

    def scores(c, slot):
        for h in heads:
            s_ref[h, slot] = _dot_nt(q2[h], k_of(h, keys(c)))

    def accumulate(s, rows):
        m_old = [m_ref[h] for h in heads]
        m_new = [jnp.maximum(m_old[h], jnp.max(s[h], axis=1, keepdims=True)) for h in heads]
        p = [jnp.exp(s[h] - m_new[h][:, 0:1]).astype(BF16) for h in heads]
        pv = [jnp.dot(p[h], v_of(h, rows), preferred_element_type=F32) for h in heads]
        for h in heads:
            alpha = jnp.exp(m_old[h] - m_new[h])
            acc_ref[h] = jnp.concatenate([alpha, alpha], axis=1) * acc_ref[h] + pv[h]
            m_ref[h] = m_new[h]

    @pl.when(i == 0)
    def _():
        ctx = slice(0, CTX_LEN)
        accumulate([_dot_nt(q2[h], k_of(h, ctx)) for h in heads], ctx)

    @pl.when(i > 0)
    def _():
        scores(0, 0)

        def pair(t, carry):
            c = 2 * t
            scores(c + 1, 1)
            accumulate([s_ref[h, 0] for h in heads], keys(c))
            scores(c + 2, 0)
            accumulate([s_ref[h, 1] for h in heads], keys(c + 1))
            return carry

        lax.fori_loop(0, (n_kchunks - 1) // 2, pair, 0)
        if n_kchunks % 2 == 0:
            scores(n_kchunks - 1, 1)
            accumulate([s_ref[h, 0] for h in heads], keys(n_kchunks - 2))
            accumulate([s_ref[h, 1] for h in heads], keys(n_kchunks - 1))
        else:
            accumulate([s_ref[h, 0] for h in heads], keys(n_kchunks - 1))

    lam_init = lv_ref[4:5, 0:1]
    lam = (jnp.exp(jnp.sum(lv_ref[0:1, :] * lv_ref[1:2, :], axis=1, keepdims=True))
           - jnp.exp(jnp.sum(lv_ref[2:3, :] * lv_ref[3:4, :], axis=1, keepdims=True)) + lam_init)
    for h in heads:
        a1 = acc_ref[h, 0:DIFF_TQ, :]
        a2 = acc_ref[h, DIFF_TQ:, :]
        o = a1[:, :DIFF_HD] / a1[:, DIFF_HD:] - lam * (a2[:, :DIFF_HD] / a2[:, DIFF_HD:])
        y = o * lax.rsqrt(jnp.mean(o * o, axis=1, keepdims=True) + EPS) * sw_ref[...]
        o_ref[:, h * DIFF_HD:(h + 1) * DIFF_HD] = y * (1.0 - lam_init)


def _diff_pallas(p, q_norm, k_norm, lq1, lk1, lq2, lk2, subln, lam_init):
    rows = p.shape[0]
    assert rows % DIFF_TK == 0 and rows % DIFF_TQ == 0 and CTX_LEN == DIFF_TQ
    q, k, v = _diff_prep(p, q_norm, k_norm)
    pad = lambda t: jnp.pad(t, (0, DIFF_HD - DIFF_HEAD_DIM))
    lvec = jnp.stack([pad(lq1), pad(lk1), pad(lq2), pad(lk2), jnp.full((DIFF_HD,), lam_init, F32),
                      jnp.zeros((DIFF_HD,), F32), jnp.zeros((DIFF_HD,), F32), jnp.zeros((DIFF_HD,), F32)])
    return pl.pallas_call(
        functools.partial(_diff_kernel, n_kchunks=rows // DIFF_TK),
        grid=(DIFF_HEADS // DIFF_HPS, rows // DIFF_TQ),
        in_specs=[pl.BlockSpec((DIFF_TQ, DIFF_HPS * DIFF_HD), lambda h, i: (i, h)),
                  pl.BlockSpec((rows, DIFF_HPS * DIFF_HD), lambda h, i: (0, h)),
                  pl.BlockSpec((rows, 2 * DIFF_HPS * DIFF_HD), lambda h, i: (0, h)),
                  pl.BlockSpec((8, DIFF_HD), lambda h, i: (0, 0)),
                  pl.BlockSpec((1, DIFF_HD), lambda h, i: (0, 0))],
        out_specs=pl.BlockSpec((DIFF_TQ, DIFF_HPS * DIFF_HD), lambda h, i: (i, h)),
        out_shape=jax.ShapeDtypeStruct((rows, GROUP_WIDTH), F32),
        scratch_shapes=[pltpu.VMEM((DIFF_HPS, 2 * DIFF_TQ, DIFF_HD), F32),
                        pltpu.VMEM((DIFF_HPS, 2 * DIFF_TQ, 2 * DIFF_HD), F32),
                        pltpu.VMEM((DIFF_HPS, 2, 2 * DIFF_TQ, DIFF_TK), F32)],
        compiler_params=pltpu.CompilerParams(
            dimension_semantics=("arbitrary", "arbitrary"), vmem_limit_bytes=VMEM_LIMIT_BYTES),
        name="diff_attention",
    )(q, k, v, lvec, subln.reshape(1, DIFF_HD))


GDN_TB = 256
GDN_HALO = 8
GDN_GATES = 2 * GDN_HEADS


def _softplus(x):
    return jnp.maximum(x, 0.0) + jnp.log(1.0 + jnp.exp(-jnp.abs(x)))


def _gdn_prep_kernel(*refs):
    (qp, qc, qn, kp, kc, kn, vp, vc, vn, ab_ref, w_ref, alog_ref, dtb_ref,
     qo_ref, ko_ref, vo_ref, go_ref, pad_ref) = refs
    i = pl.program_id(0)
    last = pl.num_programs(0) - 1
    prev_ok = (i >= 2).astype(F32)
    next_ok = jnp.logical_and(i >= 1, i < last).astype(F32)
    half = GDN_CONV // 2

    def conv_silu(prev_ref, cur_ref, next_ref, sec):
        pad_ref[0:GDN_HALO, :] = prev_ref[...] * prev_ok
        pad_ref[GDN_HALO:GDN_HALO + GDN_TB, :] = cur_ref[...]
        pad_ref[GDN_HALO + GDN_TB:, :] = next_ref[...] * next_ok
        acc = jnp.zeros((GDN_TB, GROUP_WIDTH), F32)
        for j in range(GDN_CONV):
            w = w_ref[j:j + 1, sec * GROUP_WIDTH:(sec + 1) * GROUP_WIDTH]
            acc = acc + pad_ref[GDN_HALO - half + j:GDN_HALO - half + j + GDN_TB, :] * w
        return acc * jax.nn.sigmoid(acc)

    def l2n(x):
        parts = []
        for h in range(GDN_HEADS):
            xh = x[:, h * GDN_HEAD_DIM:(h + 1) * GDN_HEAD_DIM]
            parts.append(xh * lax.rsqrt(jnp.sum(xh * xh, axis=1, keepdims=True) + EPS))
        return jnp.concatenate(parts, axis=1)

    qo_ref[...] = l2n(conv_silu(qp, qc, qn, 0)) * GDN_HEAD_DIM ** -0.5
    ko_ref[...] = l2n(conv_silu(kp, kc, kn, 1))
    vo_ref[...] = conv_silu(vp, vc, vn, 2)
    x = ab_ref[...]
    lane = lax.broadcasted_iota(jnp.int32, x.shape, 1)
    go_ref[...] = jnp.where(lane < GDN_GATES, -jnp.exp(alog_ref[...]) * _softplus(x + dtb_ref[...]),
                            jax.nn.sigmoid(x))


def _gdn_prep(p, conv_w, a_log, dt_bias):
    rows = p.shape[0]
    assert rows % GDN_TB == 0 and CTX_LEN == GDN_TB
    n_halo = rows // GDN_HALO
    per = GDN_TB // GDN_HALO
    c0 = P_GDN_QKV // GROUP_WIDTH
    specs = []
    for sec in range(3):
        specs += [
            pl.BlockSpec((GDN_HALO, GROUP_WIDTH), lambda i, sec=sec: (jnp.maximum(i * per - 1, 0), c0 + sec)),
            pl.BlockSpec((GDN_TB, GROUP_WIDTH), lambda i, sec=sec: (i, c0 + sec)),
            pl.BlockSpec((GDN_HALO, GROUP_WIDTH),
                         lambda i, sec=sec: (jnp.minimum((i + 1) * per, n_halo - 1), c0 + sec)),
        ]
    vec = pl.BlockSpec((1, 128), lambda i: (0, 0))
    specs += [pl.BlockSpec((GDN_TB, 128), lambda i: (i, P_GDN_AB // 128)),
              pl.BlockSpec((8, 3 * GROUP_WIDTH), lambda i: (0, 0)), vec, vec]
    blk = pl.BlockSpec((GDN_TB, GROUP_WIDTH), lambda i: (i, 0))
    out = jax.ShapeDtypeStruct((rows, GROUP_WIDTH), F32)
    pad8 = lambda t: jnp.pad(t.reshape(1, GDN_GATES), ((0, 0), (0, 128 - GDN_GATES)))
    return pl.pallas_call(
        _gdn_prep_kernel,
        grid=(rows // GDN_TB,),
        in_specs=specs,
        out_specs=[blk, blk, blk, pl.BlockSpec((GDN_TB, 128), lambda i: (i, 0))],
        out_shape=[out, out, out, jax.ShapeDtypeStruct((rows, 128), F32)],
        scratch_shapes=[pltpu.VMEM((GDN_TB + 2 * GDN_HALO, GROUP_WIDTH), F32)],
        compiler_params=pltpu.CompilerParams(
            dimension_semantics=("arbitrary",), vmem_limit_bytes=VMEM_LIMIT_BYTES),
        name="gdn_prep",
    )(*([p] * 10), jnp.pad(conv_w, ((0, 8 - GDN_CONV), (0, 0))), pad8(a_log), pad8(dt_bias))


def _mm3(a, b):
    ah = a.astype(BF16)
    al = (a - ah.astype(F32)).astype(BF16)
    bh = b.astype(BF16)
    bl = (b - bh.astype(F32)).astype(BF16)
    dot = lambda x, y: jnp.dot(x, y, preferred_element_type=F32)
    return dot(ah, bh) + dot(ah, bl) + dot(al, bh)


def _gdn_kernel(qf_ref, kf_ref, vf_ref, gf_ref, qb_ref, kb_ref, vb_ref, gb_ref, of_ref, ob_ref, s_ref):
    j = pl.program_id(0)
    tb, ck, hd = GDN_TB, GDN_CHUNK, GDN_HEAD_DIM
    n_ck = tb // ck

    @pl.when(j == 0)
    def _():
        s_ref[...] = jnp.zeros_like(s_ref)

    ri = lax.broadcasted_iota(jnp.int32, (tb, tb), 0)
    ci = lax.broadcasted_iota(jnp.int32, (tb, tb), 1)
    same = lambda n: (ri // n) == (ci // n)
    same_chunk = same(ck)
    eye = (ri == ci).astype(F32)
    dot = lambda x, y: jnp.dot(x, y, preferred_element_type=F32)
    tot_m = same_chunk.astype(BF16)

    def direction(reverse, g_ref):
        before = (ci > ri) if reverse else (ci < ri)
        strict = jnp.logical_and(same_chunk, before)
        incl = jnp.logical_and(same_chunk, jnp.logical_or(before, ri == ci))
        g = g_ref[...]
        g1 = g.astype(BF16)
        r1 = g - g1.astype(F32)
        g2 = r1.astype(BF16)
        g3 = (r1 - g2.astype(F32)).astype(BF16)
        cum_m = incl.astype(BF16)
        gcum = dot(cum_m, g1) + dot(cum_m, g2) + dot(cum_m, g3)
        gtot = dot(tot_m, g1) + dot(tot_m, g2) + dot(tot_m, g3)
        return dict(strict=strict, incl=incl, g=g, gcum=gcum, gtot=gtot, gcum_t=gcum.T,
                    order=list(range(n_ck - 1, -1, -1)) if reverse else list(range(n_ck)))

    dirs = [direction(False, gf_ref), direction(True, gb_ref)]
    refs = [(qf_ref, kf_ref, vf_ref, of_ref), (qb_ref, kb_ref, vb_ref, ob_ref)]
    chains = [(d, h) for d in range(2) for h in range(GDN_HEADS)]
    heads = range(len(chains))
    hs = [slice(h * hd, (h + 1) * hd) for _, h in chains]
    q_ref = [refs[d][0] for d, _ in chains]
    k_ref = [refs[d][1] for d, _ in chains]
    v_ref = [refs[d][2] for d, _ in chains]
    incl = [dirs[d]['incl'] for d, _ in chains]
    strict = [dirs[d]['strict'] for d, _ in chains]
    lanes = [d * GDN_HEADS + h for d, h in chains]
    gcol = [dirs[d]['gcum'][:, l:l + 1] for (d, _), l in zip(chains, lanes)]
    grow = [dirs[d]['gcum_t'][l:l + 1, :] for (d, _), l in zip(chains, lanes)]
    glast = [dirs[d]['gtot'][:, l:l + 1] for (d, _), l in zip(chains, lanes)]
    beta = [dirs[d]['g'][:, GDN_GATES + l:GDN_GATES + l + 1] for (d, _), l in zip(chains, lanes)]
    decay = [jnp.where(incl[h], jnp.exp(jnp.where(incl[h], gcol[h] - grow[h], 0.0)), 0.0) for h in heads]
    kb = [k_ref[h][:, hs[h]] * beta[h] for h in heads]
    k_b = [k_ref[h][:, hs[h]].astype(BF16) for h in heads]
    a_mat = [jnp.where(strict[h], _dot_nt(kb[h].astype(BF16), k_b[h]) * decay[h], 0.0) for h in heads]
    attn = [(_dot_nt(q_ref[h][:, hs[h]].astype(BF16), k_b[h]) * decay[h]).astype(BF16) for h in heads]

    x = [eye - jnp.where(same(2), a_mat[h], 0.0) for h in heads]
    for half_blk in (2, 4, 8, 16, 32):
        level = jnp.logical_and(same(2 * half_blk), jnp.logical_not(same(half_blk)))
        x_b = [x[h].astype(BF16) for h in heads]
        xl = [dot(x_b[h], jnp.where(level, a_mat[h], 0.0).astype(BF16)).astype(BF16) for h in heads]
        x = [x[h] - dot(xl[h], x_b[h]) for h in heads]
    resid = [eye - x[h] - _mm3(a_mat[h], x[h]) for h in heads]
    t_mat = [(x[h] + dot(x[h].astype(BF16), resid[h].astype(BF16))).astype(BF16) for h in heads]

    eg = [jnp.exp(gcol[h]) for h in heads]
    wu = [dot(t_mat[h], jnp.concatenate([kb[h] * eg[h], v_ref[h][:, hs[h]] * beta[h]], axis=1).astype(BF16))
          for h in heads]
    w_b = [wu[h][:, :hd].astype(BF16) for h in heads]
    qd_b = [(q_ref[h][:, hs[h]] * eg[h]).astype(BF16) for h in heads]
    k_carry = [(k_ref[h][:, hs[h]] * jnp.exp(glast[h] - gcol[h])).astype(BF16) for h in heads]

    s = [s_ref[h] for h in heads]
    v_new = [[None] * n_ck for _ in heads]
    o_inter = [[None] * n_ck for _ in heads]
    for step in range(n_ck):
        for h in heads:
            c = dirs[chains[h][0]]['order'][step]
            rows = slice(c * ck, (c + 1) * ck)
            s_b = s[h].astype(BF16)
            v_new[h][c] = wu[h][rows, hd:] - dot(w_b[h][rows], s_b)
            o_inter[h][c] = dot(qd_b[h][rows], s_b)
            s[h] = s[h] * jnp.exp(glast[h][c * ck:c * ck + 1, :]) + lax.dot_general(
                k_carry[h][rows], v_new[h][c].astype(BF16), (((0,), (0,)), ((), ())),
                preferred_element_type=F32)
    for h in heads:
        s_ref[h] = s[h]
        v_all = jnp.concatenate(v_new[h], axis=0).astype(BF16)
        refs[chains[h][0]][3][:, hs[h]] = jnp.concatenate(o_inter[h], axis=0) + dot(attn[h], v_all)


def _gdn_scan(q, k, v, gates):
    rows = q.shape[0]
    n_blocks = rows // GDN_TB
    bwd_of = lambda j: jnp.where(j == 0, 0, n_blocks - j)
    blk_f = pl.BlockSpec((GDN_TB, GROUP_WIDTH), lambda j: (j, 0))
    blk_b = pl.BlockSpec((GDN_TB, GROUP_WIDTH), lambda j: (bwd_of(j), 0))
    out = jax.ShapeDtypeStruct((rows, GROUP_WIDTH), F32)
    return pl.pallas_call(
        _gdn_kernel,
        grid=(n_blocks,),
        in_specs=[blk_f, blk_f, blk_f, pl.BlockSpec((GDN_TB, 128), lambda j: (j, 0)),
                  blk_b, blk_b, blk_b, pl.BlockSpec((GDN_TB, 128), lambda j: (bwd_of(j), 0))],
        out_specs=[blk_f, blk_b],
        out_shape=[out, out],
        scratch_shapes=[pltpu.VMEM((2 * GDN_HEADS, GDN_HEAD_DIM, GDN_HEAD_DIM), F32)],
        compiler_params=pltpu.CompilerParams(
            dimension_semantics=("arbitrary",), vmem_limit_bytes=VMEM_LIMIT_BYTES),
        name="gdn_scan",
    )(q, k, v, gates, q, k, v, gates)


def _gdn_finish_kernel(of_ref, ob_ref, z_ref, w_ref, y_ref):
    o = of_ref[...] + ob_ref[...]
    z = z_ref[...]
    parts = []
    for h in range(GDN_HEADS):
        oh = o[:, h * GDN_HEAD_DIM:(h + 1) * GDN_HEAD_DIM]
        parts.append(oh * lax.rsqrt(jnp.mean(oh * oh, axis=1, keepdims=True) + EPS) * w_ref[...])
    y_ref[...] = jnp.concatenate(parts, axis=1) * (z * jax.nn.sigmoid(z))


def _gdn_pallas(p, conv_w, a_log, dt_bias, norm_w):
    rows = p.shape[0]
    q, k, v, gates = _gdn_prep(p, conv_w, a_log, dt_bias)
    o_f, o_b = _gdn_scan(q, k, v, gates)
    blk = pl.BlockSpec((GDN_TB, GROUP_WIDTH), lambda i: (i, 0))
    return pl.pallas_call(
        _gdn_finish_kernel,
        grid=(rows // GDN_TB,),
        in_specs=[blk, blk, pl.BlockSpec((GDN_TB, GROUP_WIDTH), lambda i: (i, P_GDN_Z // GROUP_WIDTH)),
                  pl.BlockSpec((1, GDN_HEAD_DIM), lambda i: (0, 0))],
        out_specs=blk,
        out_shape=jax.ShapeDtypeStruct((rows, GROUP_WIDTH), F32),
        compiler_params=pltpu.CompilerParams(
            dimension_semantics=("arbitrary",), vmem_limit_bytes=VMEM_LIMIT_BYTES),
        name="gdn_finish",
    )(o_f, o_b, p, norm_w.reshape(1, GDN_HEAD_DIM))


def _reorder_w_in(w):
    front = w[..., 0:4096].astype(BF16)
    gdn_ab = w[..., 4096:4112].astype(BF16)
    na = w[..., 4112:5648].astype(BF16)
    pad = jnp.zeros(w.shape[:-1] + (P_WIDTH - P_GDN_AB - 16,), BF16)
    return jnp.concatenate([front, na, gdn_ab, pad], axis=-1)


def kernel(x, c, ctx, c_ctx, w_ada, b_ada, norm_ffn1, norm_mix, norm_ffn2, ffn1_w_in, ffn1_w_out,
           ffn2_w_in, ffn2_w_out, w_in, w_out, s5_lambda_re, s5_lambda_im, s5_log_step, s5_b_re,
           s5_b_im, s5_c_re, s5_c_im, s5_d, s5_w_glu, s5_b_glu, diff_q_norm, diff_k_norm,
           diff_lambda_q1, diff_lambda_k1, diff_lambda_q2, diff_lambda_k2, diff_subln, gdn_conv,
           gdn_a_log, gdn_dt_bias, gdn_norm, na_q_norm, na_k_norm, na_rpb):
    mod = _modulation(c, c_ctx, w_ada, b_ada)
    s = jnp.concatenate([ctx[0], x[0]], axis=0)
    ffn1_in, ffn1_out = ffn1_w_in.astype(BF16), ffn1_w_out.astype(BF16)
    ffn2_in, ffn2_out = ffn2_w_in.astype(BF16), ffn2_w_out.astype(BF16)
    w_in_bf, w_out_bf = _reorder_w_in(w_in), w_out.astype(BF16)
    for l in range(DEPTH):
        mod_l = mod[l]
        s = _ffn(s, mod_l, norm_ffn1[l], ffn1_in, ffn1_out, l, 0)
        p = _inproj(s, mod_l, norm_mix[l], w_in_bf, l)
        lam_init = 0.8 - 0.6 * math.exp(-0.3 * l)
        ya = _s5_pallas(p, s5_lambda_re[l], s5_lambda_im[l], s5_log_step[l], s5_b_re[l], s5_b_im[l],
                        s5_c_re[l], s5_c_im[l], s5_d[l], s5_w_glu[l], s5_b_glu[l])
        yb = _diff_pallas(p, diff_q_norm[l], diff_k_norm[l], diff_lambda_q1[l], diff_lambda_k1[l],
                          diff_lambda_q2[l], diff_lambda_k2[l], diff_subln[l], lam_init)
        yc = _gdn_pallas(p, gdn_conv[l], gdn_a_log[l], gdn_dt_bias[l], gdn_norm[l])
        yd = _na_pallas(p, na_q_norm[l], na_k_norm[l], na_rpb[l])
        s = _outproj(s, mod_l, [ya, yb, yc, yd], w_out_bf, l)
        s = _ffn(s, mod_l, norm_ffn2[l], ffn2_in, ffn2_out, l, 6)
    return s[None, CTX_LEN:]
```

```python
import functools
import math

import jax
import jax.numpy as jnp
from jax import lax
from jax.experimental import pallas as pl
from jax.experimental.pallas import tpu as pltpu

D_MODEL = 2048
SEQ = 8192
DEPTH = 4
GRID_W = 64
CTX_LEN = 256
ROWS = CTX_LEN + SEQ
GROUP_WIDTH = 512
D_FF = 5632
N_MOD = 9
EPS = 1e-6

S5_CH = 16
S5_GROUPS = GROUP_WIDTH // S5_CH
S5_STATE = 64
DIFF_HEADS = 4
DIFF_HEAD_DIM = 64
ROPE_BASE = 10000.0
Q_BLOCK = 128
GDN_HEADS = 4
GDN_HEAD_DIM = 128
GDN_CONV = 5
GDN_CHUNK = 64
NA_HEADS = 8
NA_HEAD_DIM = 64
WIN_H = 8
WIN_W = 16
NA_KEY_COLS = 2 * WIN_W

P_S5 = 0
P_DIFF = 512
P_GDN_QKV = 2048
P_GDN_Z = 3584
P_NA = 4096
P_GDN_AB = 5632
P_WIDTH = 5760

VMEM_LIMIT_BYTES = 56 * 1024 * 1024

F32 = jnp.float32
BF16 = jnp.bfloat16


MOD_TN = 1024


def _mod_kernel(ct_ref, w_ref, b_ref, o_ref):
    c = ct_ref[...]
    s = c * jax.nn.sigmoid(c)
    v0 = jnp.broadcast_to(s[:, 0:1], (D_MODEL, 128))
    v1 = jnp.broadcast_to(s[:, 1:2], (D_MODEL, 128))
    for j in range(MOD_TN // 128):
        w = w_ref[:, j * 128:(j + 1) * 128]
        b = b_ref[:, j * 128:(j + 1) * 128]
        o_ref[0:1, j * 128:(j + 1) * 128] = jnp.sum(w * v0, axis=0, keepdims=True) + b
        o_ref[1:2, j * 128:(j + 1) * 128] = jnp.sum(w * v1, axis=0, keepdims=True) + b


def _modulation(c, c_ctx, w_ada, b_ada):
    n = N_MOD * D_MODEL
    ct = jnp.stack([c.reshape(D_MODEL), c_ctx.reshape(D_MODEL)], axis=1)
    out = pl.pallas_call(
        _mod_kernel,
        grid=(DEPTH, n // MOD_TN),
        in_specs=[
            pl.BlockSpec((D_MODEL, 2), lambda l, j: (0, 0)),
            pl.BlockSpec((None, D_MODEL, MOD_TN), lambda l, j: (l, 0, j)),
            pl.BlockSpec((None, 1, MOD_TN), lambda l, j: (l, 0, j)),
        ],
        out_specs=pl.BlockSpec((None, 2, MOD_TN), lambda l, j: (l, 0, j)),
        out_shape=jax.ShapeDtypeStruct((DEPTH, 2, n), F32),
        compiler_params=pltpu.CompilerParams(
            dimension_semantics=("arbitrary", "arbitrary"), vmem_limit_bytes=VMEM_LIMIT_BYTES),
        name="adaln_mod",
    )(ct, w_ada, b_ada.reshape(DEPTH, 1, n))
    return out.reshape(DEPTH, 2 * N_MOD, D_MODEL)


def _is_ctx_rows(tile_rows):
    rows = pl.program_id(0) * tile_rows + lax.broadcasted_iota(jnp.int32, (tile_rows, 1), 0)
    return rows < CTX_LEN


def _mod_row(mod_ref, is_ctx, k):
    return jnp.where(is_ctx, mod_ref[N_MOD + k:N_MOD + k + 1, :], mod_ref[k:k + 1, :])


NORM_ROWS = 16


def _mod_chunk_row(mod_ref, first_row, k):
    off = jnp.where(first_row < CTX_LEN, N_MOD, 0)
    return mod_ref[pl.ds(off + k, 1), :]


def _fill_gain_shift(gamma_ref, mod_ref, gs_ref, base):
    for seg in range(2):
        gain = gamma_ref[...] * (1.0 + mod_ref[seg * N_MOD + base + 1:seg * N_MOD + base + 2, :])
        gs_ref[seg] = jnp.broadcast_to(gain, (NORM_ROWS, D_MODEL))
        gs_ref[2 + seg] = jnp.broadcast_to(mod_ref[seg * N_MOD + base:seg * N_MOD + base + 1, :],
                                           (NORM_ROWS, D_MODEL))


def _adaln_rows(x_ref, h_ref, gs_ref, tile_row0, lo, n_rows):
    for t in range(n_rows // NORM_ROWS):
        sl = slice(lo + t * NORM_ROWS, lo + (t + 1) * NORM_ROWS)
        seg = (tile_row0 + sl.start < CTX_LEN).astype(jnp.int32)
        x = x_ref[sl, :]
        ms = jnp.mean(x * x, axis=-1, keepdims=True)
        h = x * lax.rsqrt(ms + EPS) * gs_ref[seg] + gs_ref[2 + seg]
        h_ref[sl, :] = h.astype(BF16)


FFN_TM = 1408
FFN_RC = 352
FFN_TF = 512


def _ffn_kernel(x_ref, mod_ref, gamma_ref, wg_ref, wu_ref, wo_ref, o_ref, h_ref, gs_ref, *, base):
    f = pl.program_id(1)
    last = pl.num_programs(1) - 1
    row0 = pl.program_id(0) * FFN_TM
    n_chunks = FFN_TM // FFN_RC

    def down(r):
        rows = pl.ds(r * FFN_RC, FFN_RC)
        h = h_ref[rows, :]
        g = jnp.dot(h, wg_ref[...], preferred_element_type=F32)
        u = jnp.dot(h, wu_ref[...], preferred_element_type=F32)
        a = (g * jax.nn.sigmoid(g) * u).astype(BF16)
        return rows, jnp.dot(a, wo_ref[...], preferred_element_type=F32)

    @pl.when(f == 0)
    def _():
        _fill_gain_shift(gamma_ref, mod_ref, gs_ref, base)
        _adaln_rows(x_ref, h_ref, gs_ref, row0, 0, FFN_RC)
        for r in range(n_chunks):
            if r + 1 < n_chunks:
                _adaln_rows(x_ref, h_ref, gs_ref, row0, (r + 1) * FFN_RC, FFN_RC)
            rows, d = down(r)
            o_ref[rows, :] = d

    @pl.when(jnp.logical_and(f > 0, f < last))
    def _():
        for r in range(n_chunks):
            rows, d = down(r)
            o_ref[rows, :] += d

    @pl.when(f == last)
    def _():
        for r in range(n_chunks):
            _, d = down(r)
            for t in range(FFN_RC // NORM_ROWS):
                lo = r * FFN_RC + t * NORM_ROWS
                sl = slice(lo, lo + NORM_ROWS)
                gate = _mod_chunk_row(mod_ref, row0 + lo, base + 2)
                o_ref[sl, :] = x_ref[sl, :] + 0.5 * gate * (o_ref[sl, :] + d[t * NORM_ROWS:(t + 1) * NORM_ROWS])


def _ffn(s, mod_l, gamma, w_in_bf, w_out_bf, layer, base):
    nf = D_FF // FFN_TF
    return pl.pallas_call(
        functools.partial(_ffn_kernel, base=base),
        grid=(ROWS // FFN_TM, nf),
        in_specs=[
            pl.BlockSpec((FFN_TM, D_MODEL), lambda i, f: (i, 0)),
            pl.BlockSpec((2 * N_MOD, D_MODEL), lambda i, f: (0, 0)),
            pl.BlockSpec((1, D_MODEL), lambda i, f: (0, 0)),
            pl.BlockSpec((None, D_MODEL, FFN_TF), lambda i, f: (layer, 0, f)),
            pl.BlockSpec((None, D_MODEL, FFN_TF), lambda i, f: (layer, 0, nf + f)),
            pl.BlockSpec((None, FFN_TF, D_MODEL), lambda i, f: (layer, f, 0)),
        ],
        out_specs=pl.BlockSpec((FFN_TM, D_MODEL), lambda i, f: (i, 0), pipeline_mode=pl.Buffered(1)),
        out_shape=jax.ShapeDtypeStruct((ROWS, D_MODEL), F32),
        scratch_shapes=[pltpu.VMEM((FFN_TM, D_MODEL), BF16), pltpu.VMEM((4, NORM_ROWS, D_MODEL), F32)],
        compiler_params=pltpu.CompilerParams(
            dimension_semantics=("arbitrary", "arbitrary"), vmem_limit_bytes=VMEM_LIMIT_BYTES),
        name="ffn_swiglu",
    )(s, mod_l, gamma.reshape(1, D_MODEL), w_in_bf, w_in_bf, w_out_bf)


INP_TM = 1408
INP_RC = 352
INP_TN = 1152


def _inproj_kernel(x_ref, mod_ref, gamma_ref, w_ref, o_ref, h_ref, gs_ref):
    n = pl.program_id(1)
    row0 = pl.program_id(0) * INP_TM
    n_chunks = INP_TM // INP_RC

    def project(r):
        rows = pl.ds(r * INP_RC, INP_RC)
        o_ref[rows, :] = jnp.dot(h_ref[rows, :], w_ref[...], preferred_element_type=F32)

    @pl.when(n == 0)
    def _():
        _fill_gain_shift(gamma_ref, mod_ref, gs_ref, 3)
        _adaln_rows(x_ref, h_ref, gs_ref, row0, 0, INP_RC)
        for r in range(n_chunks):
            if r + 1 < n_chunks:
                _adaln_rows(x_ref, h_ref, gs_ref, row0, (r + 1) * INP_RC, INP_RC)
            project(r)

    @pl.when(n > 0)
    def _():
        for r in range(n_chunks):
            project(r)


def _inproj(s, mod_l, gamma, w_bf, layer):
    return pl.pallas_call(
        _inproj_kernel,
        grid=(ROWS // INP_TM, P_WIDTH // INP_TN),
        in_specs=[
            pl.BlockSpec((INP_TM, D_MODEL), lambda i, n: (i, 0)),
            pl.BlockSpec((2 * N_MOD, D_MODEL), lambda i, n: (0, 0)),
            pl.BlockSpec((1, D_MODEL), lambda i, n: (0, 0)),
            pl.BlockSpec((None, D_MODEL, INP_TN), lambda i, n: (layer, 0, n)),
        ],
        out_specs=pl.BlockSpec((INP_TM, INP_TN), lambda i, n: (i, n)),
        out_shape=jax.ShapeDtypeStruct((ROWS, P_WIDTH), F32),
        scratch_shapes=[pltpu.VMEM((INP_TM, D_MODEL), BF16), pltpu.VMEM((4, NORM_ROWS, D_MODEL), F32)],
        compiler_params=pltpu.CompilerParams(
            dimension_semantics=("arbitrary", "arbitrary"), vmem_limit_bytes=VMEM_LIMIT_BYTES),
        name="in_proj",
    )(s, mod_l, gamma.reshape(1, D_MODEL), w_bf)


OUT_TM = 384


def _outproj_kernel(x_ref, mod_ref, ya_ref, yb_ref, yc_ref, yd_ref, w_ref, o_ref):
    is_ctx = _is_ctx_rows(OUT_TM)
    acc = jnp.zeros((OUT_TM, D_MODEL), F32)
    for k, y_ref in enumerate((ya_ref, yb_ref, yc_ref, yd_ref)):
        acc += jnp.dot(y_ref[...].astype(BF16), w_ref[k * GROUP_WIDTH:(k + 1) * GROUP_WIDTH, :],
                       preferred_element_type=F32)
    o_ref[...] = x_ref[...] + _mod_row(mod_ref, is_ctx, 5) * acc


def _outproj(s, mod_l, ys, w_bf, layer):
    yspec = pl.BlockSpec((OUT_TM, GROUP_WIDTH), lambda i: (i, 0))
    return pl.pallas_call(
        _outproj_kernel,
        grid=(ROWS // OUT_TM,),
        in_specs=[
            pl.BlockSpec((OUT_TM, D_MODEL), lambda i: (i, 0)),
            pl.BlockSpec((2 * N_MOD, D_MODEL), lambda i: (0, 0)),
            yspec, yspec, yspec, yspec,
            pl.BlockSpec((None, D_MODEL, D_MODEL), lambda i: (layer, 0, 0)),
        ],
        out_specs=pl.BlockSpec((OUT_TM, D_MODEL), lambda i: (i, 0)),
        out_shape=jax.ShapeDtypeStruct((ROWS, D_MODEL), F32),
        compiler_params=pltpu.CompilerParams(
            dimension_semantics=("arbitrary",), vmem_limit_bytes=VMEM_LIMIT_BYTES),
        name="out_proj",
    )(s, mod_l, *ys, w_bf)


S5_TC = 256


def _s5_prepare(lam_re, lam_im, log_step, b_re, b_im, c_re, c_im):
    dt = jnp.exp(log_step)[..., None]
    mag = jnp.exp(lam_re * dt)
    ar = mag * jnp.cos(lam_im * dt)
    ai = mag * jnp.sin(lam_im * dt)
    den = lam_re * lam_re + lam_im * lam_im
    fr = ((ar - 1.0) * lam_re + ai * lam_im) / den
    fi = (ai * lam_re - (ar - 1.0) * lam_im) / den
    bbr = fr[..., None] * b_re - fi[..., None] * b_im
    bbi = fr[..., None] * b_im + fi[..., None] * b_re
    in_oct = jnp.eye(4, dtype=F32)[jnp.arange(S5_GROUPS) % 4]
    half = 4 * S5_STATE

    def w_in_half(bb):
        return jnp.einsum('dgpn,gj->dgnjp', bb, in_oct).reshape(2, GROUP_WIDTH, half)

    def w_out_half(cc):
        return jnp.einsum('dgcp,gj->djpgc', cc, in_oct).reshape(2, half, GROUP_WIDTH)

    w_in = jnp.concatenate([w_in_half(bbr), w_in_half(bbi)], axis=2)
    w_out = jnp.concatenate([w_out_half(c_re), w_out_half(-c_im)], axis=1)
    return (ar.reshape(2, 8, half), ai.reshape(2, 8, half), w_in.astype(BF16), w_out.astype(BF16))


def _s5_scan_kernel(uf_ref, ub_ref, win_ref, wout_ref, ar_ref, ai_ref, yf_ref, yb_ref, buf_ref, h_ref):
    j = pl.program_id(0)
    tc = S5_TC
    lane_blk = lambda b: slice(b * 128, (b + 1) * 128)

    @pl.when(j == 0)
    def _():
        buf_ref[...] = jnp.zeros_like(buf_ref)

    @pl.when(j <= 1)
    def _():
        h_ref[...] = jnp.zeros_like(h_ref)

    lo_half = lax.broadcasted_iota(jnp.int32, (tc, 128), 1) < 64
    u_refs = (uf_ref, ub_ref)
    y_refs = (yf_ref, yb_ref)
    for v in range(3):
        pl.when(lax.rem(j, 3) == v)(functools.partial(
            _s5_step, u_refs, y_refs, win_ref, wout_ref, ar_ref, ai_ref, buf_ref, h_ref, lo_half,
            v, (v + 2) % 3, (v + 1) % 3))


def _s5_step(u_refs, y_refs, win_ref, wout_ref, ar_ref, ai_ref, buf_ref, h_ref, lo_half,
             slot_in, slot_scan, slot_out):
    tc = S5_TC
    lane_blk = lambda b: slice(b * 128, (b + 1) * 128)

    def project_in(d, b, e):
        mask = lo_half if e == 0 else jnp.logical_not(lo_half)
        lhs = jnp.where(mask, u_refs[d][:, lane_blk(b)], 0.0).astype(BF16)
        res = jnp.dot(lhs, win_ref[d, lane_blk(b), :], preferred_element_type=F32)
        for s in range(4):
            buf_ref[slot_in, d, s, pl.ds(2 * b + e, tc, stride=8), :] = res[:, lane_blk(s)]

    def project_out(d, b):
        res = []
        for q in (2 * b, 2 * b + 1):
            states = jnp.concatenate([buf_ref[slot_out, d, s, pl.ds(q, tc, stride=8), :] for s in range(4)],
                                     axis=1)
            res.append(jnp.dot(states.astype(BF16), wout_ref[d, :, lane_blk(b)], preferred_element_type=F32))
        y_refs[d][:, lane_blk(b)] = jnp.where(lo_half, res[0], res[1])

    a_re = (ar_ref[0], ar_ref[1])
    a_im = (ai_ref[0], ai_ref[1])

    def scan_step(d, t, hr, hi):
        rows = slice(8 * t, 8 * t + 8)
        br = jnp.concatenate([buf_ref[slot_scan, d, 0, rows, :], buf_ref[slot_scan, d, 1, rows, :]], axis=1)
        bi = jnp.concatenate([buf_ref[slot_scan, d, 2, rows, :], buf_ref[slot_scan, d, 3, rows, :]], axis=1)
        nr = a_re[d] * hr - a_im[d] * hi + br
        ni = a_re[d] * hi + a_im[d] * hr + bi
        buf_ref[slot_scan, d, 0, rows, :] = nr[:, :128]
        buf_ref[slot_scan, d, 1, rows, :] = nr[:, 128:]
        buf_ref[slot_scan, d, 2, rows, :] = ni[:, :128]
        buf_ref[slot_scan, d, 3, rows, :] = ni[:, 128:]
        return nr, ni

    state = [(h_ref[0], h_ref[1]), (h_ref[2], h_ref[3])]
    n_phase = 16
    per_phase = tc // n_phase
    for ph in range(n_phase):
        d, b, e = ph // 8, (ph % 8) // 2, ph % 2
        project_in(d, b, e)
        if e == 1:
            project_out(d, b)
        for t in range(ph * per_phase, (ph + 1) * per_phase):
            state[0] = scan_step(0, t, *state[0])
            state[1] = scan_step(1, tc - 1 - t, *state[1])
    h_ref[0], h_ref[1] = state[0]
    h_ref[2], h_ref[3] = state[1]


def _s5_scan_call(p, a_r, a_i, w_in, w_out):
    rows = p.shape[0]
    n_chunks = rows // S5_TC
    fwd_in = lambda j: jnp.minimum(j, n_chunks - 1)
    fwd_out = lambda j: jnp.clip(j - 2, 0, n_chunks - 1)
    bwd = lambda c: jnp.where(c == 0, 0, n_chunks - c)
    full = lambda shape: pl.BlockSpec(shape, lambda j: (0,) * len(shape))
    y = jax.ShapeDtypeStruct((rows, GROUP_WIDTH), F32)
    return pl.pallas_call(
        _s5_scan_kernel,
        grid=(n_chunks + 2,),
        in_specs=[
            pl.BlockSpec((S5_TC, GROUP_WIDTH), lambda j: (fwd_in(j), 0)),
            pl.BlockSpec((S5_TC, GROUP_WIDTH), lambda j: (bwd(fwd_in(j)), 0)),
            full((2, GROUP_WIDTH, GROUP_WIDTH)), full((2, GROUP_WIDTH, GROUP_WIDTH)),
            full((2, 8, 256)), full((2, 8, 256)),
        ],
        out_specs=[pl.BlockSpec((S5_TC, GROUP_WIDTH), lambda j: (fwd_out(j), 0)),
                   pl.BlockSpec((S5_TC, GROUP_WIDTH), lambda j: (bwd(fwd_out(j)), 0))],
        out_shape=[y, y],
        scratch_shapes=[
            pltpu.VMEM((3, 2, 4, 8 * S5_TC, 128), F32),
            pltpu.VMEM((4, 8, 256), F32),
        ],
        compiler_params=pltpu.CompilerParams(
            dimension_semantics=("arbitrary",), vmem_limit_bytes=VMEM_LIMIT_BYTES),
        name="s5_scan",
    )(p, p, w_in, w_out, a_r, a_i)


S5_FIN_TM = 384


def _s5_finish_kernel(yf_ref, yb_ref, u_ref, d_ref, w_ref, b_ref, o_ref):
    y = yf_ref[...] + yb_ref[...] + d_ref[...] * u_ref[...]
    h = jnp.dot(jax.nn.gelu(y).astype(BF16), w_ref[...], preferred_element_type=F32) + b_ref[...]
    o_ref[...] = h[:, :GROUP_WIDTH] * jax.nn.sigmoid(h[:, GROUP_WIDTH:])


def _s5_finish(yf, yb, p, d_skip, w_glu_bf, b_glu):
    rows = p.shape[0]
    row_blk = pl.BlockSpec((S5_FIN_TM, GROUP_WIDTH), lambda i: (i, 0))
    return pl.pallas_call(
        _s5_finish_kernel,
        grid=(rows // S5_FIN_TM,),
        in_specs=[row_blk, row_blk, row_blk,
                  pl.BlockSpec((1, GROUP_WIDTH), lambda i: (0, 0)),
                  pl.BlockSpec((GROUP_WIDTH, 2 * GROUP_WIDTH), lambda i: (0, 0)),
                  pl.BlockSpec((1, 2 * GROUP_WIDTH), lambda i: (0, 0))],
        out_specs=row_blk,
        out_shape=jax.ShapeDtypeStruct((rows, GROUP_WIDTH), F32),
        compiler_params=pltpu.CompilerParams(
            dimension_semantics=("arbitrary",), vmem_limit_bytes=VMEM_LIMIT_BYTES),
        name="s5_finish",
    )(yf, yb, p, d_skip.reshape(1, GROUP_WIDTH), w_glu_bf, b_glu.reshape(1, 2 * GROUP_WIDTH))


def _s5_pallas(p, lam_re, lam_im, log_step, b_re, b_im, c_re, c_im, d_skip, w_glu, b_glu):
    a_r, a_i, w_in, w_out = _s5_prepare(lam_re, lam_im, log_step, b_re, b_im, c_re, c_im)
    yf, yb = _s5_scan_call(p, a_r, a_i, w_in, w_out)
    return _s5_finish(yf, yb, p, d_skip, w_glu.astype(BF16), b_glu)


PREP_TM = 256
PREP_RC = 128


def _segment_ones(width, seg):
    i = jnp.arange(width) // seg
    return (i[:, None] == i[None, :]).astype(BF16)


def _seg_sumsq(x, e):
    x2 = x * x
    hi = x2.astype(BF16)
    lo = (x2 - hi.astype(F32)).astype(BF16)
    return jnp.dot(hi, e, preferred_element_type=F32) + jnp.dot(lo, e, preferred_element_type=F32)


NA_QROWS = 4
NA_KROWS = NA_QROWS + WIN_H
NA_NQ = NA_QROWS * GRID_W
NA_NK = NA_KROWS * GRID_W
NA_PAIRS = 2
MASKED = -1e30


def _na_prep_kernel(q_ref, k_ref, v_ref, e_ref, qw_ref, kw_ref, qo_ref, ko_ref, vo_ref):
    def body(c, carry):
        sl = pl.ds(pl.multiple_of(c * PREP_RC, PREP_RC), PREP_RC)
        q = q_ref[sl, :]
        k = k_ref[sl, :]
        qn = q * lax.rsqrt(_seg_sumsq(q, e_ref[...]) * (1.0 / NA_HEAD_DIM) + EPS) * qw_ref[...]
        kn = k * lax.rsqrt(_seg_sumsq(k, e_ref[...]) * (1.0 / NA_HEAD_DIM) + EPS) * kw_ref[...]
        qo_ref[sl, :] = (qn * NA_HEAD_DIM ** -0.5).astype(BF16)
        ko_ref[sl, :] = kn.astype(BF16)
        vo_ref[sl, :] = v_ref[sl, :].astype(BF16)
        return carry

    lax.fori_loop(0, PREP_TM // PREP_RC, body, 0)


def _na_prep(p, q_norm, k_norm):
    rows = p.shape[0]
    assert rows % PREP_TM == 0
    c0 = P_NA // GROUP_WIDTH
    col = lambda j: pl.BlockSpec((PREP_TM, GROUP_WIDTH), lambda i: (i, c0 + j))
    vec = pl.BlockSpec((1, GROUP_WIDTH), lambda i: (0, 0))
    out = jax.ShapeDtypeStruct((rows, GROUP_WIDTH), BF16)
    blk = pl.BlockSpec((PREP_TM, GROUP_WIDTH), lambda i: (i, 0))
    return pl.pallas_call(
        _na_prep_kernel,
        grid=(rows // PREP_TM,),
        in_specs=[col(0), col(1), col(2),
                  pl.BlockSpec((GROUP_WIDTH, GROUP_WIDTH), lambda i: (0, 0)), vec, vec],
        out_specs=[blk, blk, blk],
        out_shape=[out, out, out],
        compiler_params=pltpu.CompilerParams(
            dimension_semantics=("arbitrary",), vmem_limit_bytes=VMEM_LIMIT_BYTES),
        name="na_prep",
    )(p, p, p, _segment_ones(GROUP_WIDTH, NA_HEAD_DIM),
      jnp.tile(q_norm, NA_HEADS).reshape(1, GROUP_WIDTH), jnp.tile(k_norm, NA_HEADS).reshape(1, GROUP_WIDTH))


def _na_bias_table(rpb, n_grid_rows):
    import numpy as np
    c = np.arange(GRID_W)
    col0 = np.clip(c - WIN_W // 2, 0, GRID_W - WIN_W)
    kc = np.arange(GRID_W)
    col_ok = (kc[None, :] >= col0[:, None]) & (kc[None, :] < col0[:, None] + WIN_W)
    dcol = kc[None, :] - c[:, None] + WIN_W - 1
    shift = ((dcol[None] == np.arange(2 * WIN_W - 1)[:, None, None]) & col_ok[None]).astype(np.float32)
    jr = np.arange(NA_QROWS)[:, None]
    kr = np.arange(NA_KROWS)[None, :]
    ri = np.stack([kr - jr + WIN_H - 1, kr - jr + WIN_H - 1 - WIN_H // 2, kr - jr + WIN_H - 1 - WIN_H])
    ok = np.stack([(kr < WIN_H) & (jr >= 0), (kr - jr >= 0) & (kr - jr < WIN_H), (kr >= NA_QROWS) & (jr >= 0)])
    valid = ok[:, :, None, :, None] & col_ok[None, None, :, None, :]
    neg = np.where(valid, 0.0, MASKED).astype(np.float32).reshape(3, 1, NA_NQ, NA_NK)
    hi = lax.Precision.HIGHEST
    band = jnp.einsum('hrj,jck->hrck', rpb, jnp.asarray(shift), precision=hi)
    zero = jnp.zeros_like(band[:, :1])
    pairs = jnp.concatenate([jnp.concatenate([zero, band], axis=1), jnp.concatenate([band, zero], axis=1)],
                            axis=-1)
    sel = (ri[:, :, 0::2, None] + 1 == np.arange(2 * WIN_H)).astype(np.float32)
    tab = jnp.einsum('vjmr,hrcy->vhjcmy', jnp.asarray(sel), pairs, precision=hi)
    return tab.reshape(3, NA_HEADS, NA_NQ, NA_NK) + jnp.asarray(neg)


def _softmax_pv(scores, values):
    heads = range(len(scores))
    m = [functools.reduce(jnp.maximum, [jnp.max(s, axis=1, keepdims=True) for s in scores[h]]) for h in heads]
    ps = [[jnp.exp(s - m[h]) for s in scores[h]] for h in heads]
    denom = [functools.reduce(lambda a, b: a + b, [jnp.sum(p, axis=1, keepdims=True) for p in ps[h]])
             for h in heads]
    acc = [functools.reduce(lambda a, b: a + b,
                            [jnp.dot(p.astype(BF16), v, preferred_element_type=F32)
                             for p, v in zip(ps[h], values[h])]) for h in heads]
    return [acc[h] / denom[h] for h in heads]


def _dot_nt(a, b):
    return lax.dot_general(a, b, (((1,), (1,)), ((), ())), preferred_element_type=F32)


def _na_kernel(q_ref, k_ref, v_ref, tab_ref, o_ref, *, n_blocks):
    i = pl.program_id(1)
    lo_half = lax.broadcasted_iota(jnp.int32, (NA_NQ, 128), 1) < NA_HEAD_DIM
    lanes = [slice(pr * 128, (pr + 1) * 128) for pr in range(NA_PAIRS)]

    def heads_of(pr):
        q = q_ref[:, lanes[pr]]
        zero = jnp.zeros_like(q)
        return (jnp.where(lo_half, q, zero), jnp.where(lo_half, zero, q))

    @pl.when(i == 0)
    def _():
        for pr in range(NA_PAIRS):
            kc, vc = k_ref[0:CTX_LEN, lanes[pr]], v_ref[0:CTX_LEN, lanes[pr]]
            outs = _softmax_pv([[_dot_nt(qh, kc)] for qh in heads_of(pr)], [[vc], [vc]])
            o_ref[:, lanes[pr]] = jnp.where(lo_half, outs[0], outs[1])

    @pl.when(i > 0)
    def _():
        ib = i - 1
        kr0 = jnp.clip(NA_QROWS * ib - WIN_H // 2, 0, NA_QROWS * n_blocks - NA_KROWS)
        win = pl.ds(pl.multiple_of(CTX_LEN + kr0 * GRID_W, GRID_W), NA_NK)
        variant = jnp.where(ib == 0, 0, jnp.where(ib == n_blocks - 1, 2, 1))
        for pr in range(NA_PAIRS):
            kc, vc = k_ref[0:CTX_LEN, lanes[pr]], v_ref[0:CTX_LEN, lanes[pr]]
            kwin, vwin = k_ref[win, lanes[pr]], v_ref[win, lanes[pr]]
            outs = []
            for e, qh in enumerate(heads_of(pr)):
                s_loc = _dot_nt(qh, kwin) + tab_ref[variant, pr, e]
                outs += _softmax_pv([[s_loc, _dot_nt(qh, kc)]], [[vwin, vc]])
            o_ref[:, lanes[pr]] = jnp.where(lo_half, outs[0], outs[1])


def _na_pallas(p, q_norm, k_norm, rpb):
    rows = p.shape[0]
    n_grid_rows = (rows - CTX_LEN) // GRID_W
    n_blocks = n_grid_rows // NA_QROWS
    q, k, v = _na_prep(p, q_norm, k_norm)
    width = 128 * NA_PAIRS
    tab = _na_bias_table(rpb, n_grid_rows).reshape(3, NA_HEADS // 2, 2, NA_NQ, NA_NK)
    kv = pl.BlockSpec((rows, width), lambda h, i: (0, h))
    return pl.pallas_call(
        functools.partial(_na_kernel, n_blocks=n_blocks),
        grid=(NA_HEADS // (2 * NA_PAIRS), n_blocks + 1),
        in_specs=[pl.BlockSpec((NA_NQ, width), lambda h, i: (i, h)), kv, kv,
                  pl.BlockSpec((3, NA_PAIRS, 2, NA_NQ, NA_NK), lambda h, i: (0, h, 0, 0, 0))],
        out_specs=pl.BlockSpec((NA_NQ, width), lambda h, i: (i, h)),
        out_shape=jax.ShapeDtypeStruct((rows, GROUP_WIDTH), F32),
        compiler_params=pltpu.CompilerParams(
            dimension_semantics=("arbitrary", "arbitrary"), vmem_limit_bytes=VMEM_LIMIT_BYTES),
        name="na_attention",
    )(q, k, v, tab)


def _outproj_fused_kernel(x_ref, mod_ref, sf_ref, sb_ref, u_ref, d_ref, wg_ref, bg_ref, yb_ref,
                          gf_ref, gb_ref, z_ref, gw_ref, yd_ref, w_ref, o_ref):
    is_ctx = _is_ctx_rows(OUT_TM)
    y = sf_ref[...] + sb_ref[...] + d_ref[...] * u_ref[...]
    h = jnp.dot(jax.nn.gelu(y).astype(BF16), wg_ref[...], preferred_element_type=F32) + bg_ref[...]
    ya = h[:, :GROUP_WIDTH] * jax.nn.sigmoid(h[:, GROUP_WIDTH:])
    o = gf_ref[...] + gb_ref[...]
    z = z_ref[...]
    parts = []
    for hd in range(GDN_HEADS):
        oh = o[:, hd * GDN_HEAD_DIM:(hd + 1) * GDN_HEAD_DIM]
        parts.append(oh * lax.rsqrt(jnp.mean(oh * oh, axis=1, keepdims=True) + EPS) * gw_ref[...])
    yc = jnp.concatenate(parts, axis=1) * (z * jax.nn.sigmoid(z))
    acc = jnp.zeros((OUT_TM, D_MODEL), F32)
    for k, yk in enumerate((ya, yb_ref[...], yc, yd_ref[...])):
        acc += jnp.dot(yk.astype(BF16), w_ref[k * GROUP_WIDTH:(k + 1) * GROUP_WIDTH, :],
                       preferred_element_type=F32)
    o_ref[...] = x_ref[...] + _mod_row(mod_ref, is_ctx, 5) * acc


def _outproj_fused(s, mod_l, p, s5_f, s5_b, d_skip, w_glu_bf, b_glu, yb, gdn_f, gdn_b, gdn_norm, yd, w_bf, layer):
    blk = pl.BlockSpec((OUT_TM, GROUP_WIDTH), lambda i: (i, 0))
    pcol = lambda c: pl.BlockSpec((OUT_TM, GROUP_WIDTH), lambda i: (i, c))
    const = lambda shape: pl.BlockSpec(shape, lambda i: (0,) * len(shape))
    return pl.pallas_call(
        _outproj_fused_kernel,
        grid=(ROWS // OUT_TM,),
        in_specs=[
            pl.BlockSpec((OUT_TM, D_MODEL), lambda i: (i, 0)),
            const((2 * N_MOD, D_MODEL)),
            blk, blk, pcol(P_S5 // GROUP_WIDTH), const((1, GROUP_WIDTH)),
            const((GROUP_WIDTH, 2 * GROUP_WIDTH)), const((1, 2 * GROUP_WIDTH)),
            blk,
            blk, blk, pcol(P_GDN_Z // GROUP_WIDTH), const((1, GDN_HEAD_DIM)),
            blk,
            pl.BlockSpec((None, D_MODEL, D_MODEL), lambda i: (layer, 0, 0)),
        ],
        out_specs=pl.BlockSpec((OUT_TM, D_MODEL), lambda i: (i, 0)),
        out_shape=jax.ShapeDtypeStruct((ROWS, D_MODEL), F32),
        compiler_params=pltpu.CompilerParams(
            dimension_semantics=("arbitrary",), vmem_limit_bytes=VMEM_LIMIT_BYTES),
        name="out_proj_fused",
    )(s, mod_l, s5_f, s5_b, p, d_skip.reshape(1, GROUP_WIDTH), w_glu_bf, b_glu.reshape(1, 2 * GROUP_WIDTH),
      yb, gdn_f, gdn_b, p, gdn_norm.reshape(1, GDN_HEAD_DIM), yd, w_bf)


DIFF_TQ = 256
DIFF_TK = 1408
DIFF_HD = 2 * DIFF_HEAD_DIM
DIFF_HPS = 2


def _rope_tables_rows(rows):
    pos = jnp.arange(rows - CTX_LEN)
    row = (pos // GRID_W).astype(F32)
    col = (pos % GRID_W).astype(F32)
    n_freq = DIFF_HEAD_DIM // 4
    inv = ROPE_BASE ** (-jnp.arange(n_freq, dtype=F32) / n_freq)
    ang = jnp.concatenate([row[:, None] * inv, row[:, None] * inv, col[:, None] * inv, col[:, None] * inv], -1)
    sign = jnp.tile(jnp.repeat(jnp.array([-1.0, 1.0, -1.0, 1.0], F32), n_freq), 2)
    cos = jnp.concatenate([jnp.ones((CTX_LEN, DIFF_HD), F32), jnp.tile(jnp.cos(ang), (1, 2))], axis=0)
    sin = jnp.concatenate([jnp.zeros((CTX_LEN, DIFF_HD), F32), jnp.tile(jnp.sin(ang), (1, 2)) * sign], axis=0)
    return cos, sin


def _diff_prep_kernel(q_ref, k_ref, v_ref, e_ref, qw_ref, kw_ref, cos_ref, sin_ref, qo_ref, ko_ref, vo_ref):
    quarter = lax.broadcasted_iota(jnp.int32, (PREP_RC, GROUP_WIDTH), 1) // (DIFF_HEAD_DIM // 4)
    first_of_pair = quarter % 2 == 0

    def rope(x, cos, sin):
        partner = jnp.where(first_of_pair, pltpu.roll(x, GROUP_WIDTH - DIFF_HEAD_DIM // 4, 1),
                            pltpu.roll(x, DIFF_HEAD_DIM // 4, 1))
        return x * cos + partner * sin

    def body(c, carry):
        sl = pl.ds(pl.multiple_of(c * PREP_RC, PREP_RC), PREP_RC)
        cos = jnp.concatenate([cos_ref[sl, :]] * DIFF_HEADS, axis=1)
        sin = jnp.concatenate([sin_ref[sl, :]] * DIFF_HEADS, axis=1)
        q = q_ref[sl, :]
        k = k_ref[sl, :]
        qn = q * lax.rsqrt(_seg_sumsq(q, e_ref[...]) * (1.0 / DIFF_HEAD_DIM) + EPS) * qw_ref[...]
        kn = k * lax.rsqrt(_seg_sumsq(k, e_ref[...]) * (1.0 / DIFF_HEAD_DIM) + EPS) * kw_ref[...]
        qo_ref[sl, :] = (rope(qn, cos, sin) * DIFF_HEAD_DIM ** -0.5).astype(BF16)
        ko_ref[sl, :] = rope(kn, cos, sin).astype(BF16)
        v = v_ref[sl, :].astype(BF16)
        ones = jnp.ones((PREP_RC, DIFF_HD), BF16)
        for h in range(DIFF_HEADS):
            vo_ref[sl, 2 * h * DIFF_HD:(2 * h + 1) * DIFF_HD] = v[:, h * DIFF_HD:(h + 1) * DIFF_HD]
            vo_ref[sl, (2 * h + 1) * DIFF_HD:(2 * h + 2) * DIFF_HD] = ones
        return carry

    lax.fori_loop(0, PREP_TM // PREP_RC, body, 0)


def _diff_prep(p, q_norm, k_norm):
    rows = p.shape[0]
    assert rows % PREP_TM == 0
    c0 = P_DIFF // GROUP_WIDTH
    col = lambda j: pl.BlockSpec((PREP_TM, GROUP_WIDTH), lambda i: (i, c0 + j))
    vec = pl.BlockSpec((1, GROUP_WIDTH), lambda i: (0, 0))
    tab = pl.BlockSpec((PREP_TM, DIFF_HD), lambda i: (i, 0))
    blk = pl.BlockSpec((PREP_TM, GROUP_WIDTH), lambda i: (i, 0))
    cos, sin = _rope_tables_rows(rows)
    return pl.pallas_call(
        _diff_prep_kernel,
        grid=(rows // PREP_TM,),
        in_specs=[col(0), col(1), col(2),
                  pl.BlockSpec((GROUP_WIDTH, GROUP_WIDTH), lambda i: (0, 0)), vec, vec, tab, tab],
        out_specs=[blk, blk, pl.BlockSpec((PREP_TM, 2 * GROUP_WIDTH), lambda i: (i, 0))],
        out_shape=[jax.ShapeDtypeStruct((rows, GROUP_WIDTH), BF16), jax.ShapeDtypeStruct((rows, GROUP_WIDTH), BF16),
                   jax.ShapeDtypeStruct((rows, 2 * GROUP_WIDTH), BF16)],
        compiler_params=pltpu.CompilerParams(
            dimension_semantics=("arbitrary",), vmem_limit_bytes=VMEM_LIMIT_BYTES),
        name="diff_prep",
    )(p, p, p, _segment_ones(GROUP_WIDTH, DIFF_HEAD_DIM),
      jnp.tile(q_norm, 2 * DIFF_HEADS).reshape(1, GROUP_WIDTH),
      jnp.tile(k_norm, 2 * DIFF_HEADS).reshape(1, GROUP_WIDTH), cos, sin)


def _diff_kernel(q_ref, k_ref, v_ref, lv_ref, sw_ref, o_ref, m_ref, acc_ref, s_ref, *, n_kchunks):
    i = pl.program_id(1)
    heads = range(DIFF_HPS)
    lo_half = lax.broadcasted_iota(jnp.int32, (DIFF_TQ, DIFF_HD), 1) < DIFF_HEAD_DIM
    q2 = []
    for h in heads:
        q = q_ref[:, h * DIFF_HD:(h + 1) * DIFF_HD]
        zero = jnp.zeros_like(q)
        q2.append(jnp.concatenate([jnp.where(lo_half, q, zero), jnp.where(lo_half, zero, q)], axis=0))

    m_ref[...] = jnp.full(m_ref.shape, MASKED, F32)
    acc_ref[...] = jnp.zeros_like(acc_ref)

    def keys(c):
        return pl.ds(pl.multiple_of(c * DIFF_TK, DIFF_TK), DIFF_TK)

    def k_of(h, rows):
        return k_ref[rows, h * DIFF_HD:(h + 1) * DIFF_HD]

    def v_of(h, rows):
        return v_ref[rows, 2 * h * DIFF_HD:2 * (h + 1) * DIFF_HD]

    def scores(c, slot):
        for h in heads:
            s_ref[h, slot] = _dot_nt(q2[h], k_of(h, keys(c)))

    def accumulate(s, rows):
        m_old = [m_ref[h] for h in heads]
        m_new = [jnp.maximum(m_old[h], jnp.max(s[h], axis=1, keepdims=True)) for h in heads]
        p = [jnp.exp(s[h] - m_new[h][:, 0:1]).astype(BF16) for h in heads]
        pv = [jnp.dot(p[h], v_of(h, rows), preferred_element_type=F32) for h in heads]
        for h in heads:
            alpha = jnp.exp(m_old[h] - m_new[h])
            acc_ref[h] = jnp.concatenate([alpha, alpha], axis=1) * acc_ref[h] + pv[h]
            m_ref[h] = m_new[h]

    @pl.when(i == 0)
    def _():
        ctx = slice(0, CTX_LEN)
        accumulate([_dot_nt(q2[h], k_of(h, ctx)) for h in heads], ctx)

    @pl.when(i > 0)
    def _():
        scores(0, 0)

        def pair(t, carry):
            c = 2 * t
            scores(c + 1, 1)
            accumulate([s_ref[h, 0] for h in heads], keys(c))
            scores(c + 2, 0)
            accumulate([s_ref[h, 1] for h in heads], keys(c + 1))
            return carry

        lax.fori_loop(0, (n_kchunks - 1) // 2, pair, 0)
        if n_kchunks % 2 == 0:
            scores(n_kchunks - 1, 1)
            accumulate([s_ref[h, 0] for h in heads], keys(n_kchunks - 2))
            accumulate([s_ref[h, 1] for h in heads], keys(n_kchunks - 1))
        else:
            accumulate([s_ref[h, 0] for h in heads], keys(n_kchunks - 1))

    lam_init = lv_ref[4:5, 0:1]
    lam = (jnp.exp(jnp.sum(lv_ref[0:1, :] * lv_ref[1:2, :], axis=1, keepdims=True))
           - jnp.exp(jnp.sum(lv_ref[2:3, :] * lv_ref[3:4, :], axis=1, keepdims=True)) + lam_init)
    for h in heads:
        a1 = acc_ref[h, 0:DIFF_TQ, :]
        a2 = acc_ref[h, DIFF_TQ:, :]
        o = a1[:, :DIFF_HD] / a1[:, DIFF_HD:] - lam * (a2[:, :DIFF_HD] / a2[:, DIFF_HD:])
        y = o * lax.rsqrt(jnp.mean(o * o, axis=1, keepdims=True) + EPS) * sw_ref[...]
        o_ref[:, h * DIFF_HD:(h + 1) * DIFF_HD] = y * (1.0 - lam_init)


def _diff_pallas(p, q_norm, k_norm, lq1, lk1, lq2, lk2, subln, lam_init):
    rows = p.shape[0]
    assert rows % DIFF_TK == 0 and rows % DIFF_TQ == 0 and CTX_LEN == DIFF_TQ
    q, k, v = _diff_prep(p, q_norm, k_norm)
    pad = lambda t: jnp.pad(t, (0, DIFF_HD - DIFF_HEAD_DIM))
    lvec = jnp.stack([pad(lq1), pad(lk1), pad(lq2), pad(lk2), jnp.full((DIFF_HD,), lam_init, F32),
                      jnp.zeros((DIFF_HD,), F32), jnp.zeros((DIFF_HD,), F32), jnp.zeros((DIFF_HD,), F32)])
    return pl.pallas_call(
        functools.partial(_diff_kernel, n_kchunks=rows // DIFF_TK),
        grid=(DIFF_HEADS // DIFF_HPS, rows // DIFF_TQ),
        in_specs=[pl.BlockSpec((DIFF_TQ, DIFF_HPS * DIFF_HD), lambda h, i: (i, h)),
                  pl.BlockSpec((rows, DIFF_HPS * DIFF_HD), lambda h, i: (0, h)),
                  pl.BlockSpec((rows, 2 * DIFF_HPS * DIFF_HD), lambda h, i: (0, h)),
                  pl.BlockSpec((8, DIFF_HD), lambda h, i: (0, 0)),
                  pl.BlockSpec((1, DIFF_HD), lambda h, i: (0, 0))],
        out_specs=pl.BlockSpec((DIFF_TQ, DIFF_HPS * DIFF_HD), lambda h, i: (i, h)),
        out_shape=jax.ShapeDtypeStruct((rows, GROUP_WIDTH), F32),
        scratch_shapes=[pltpu.VMEM((DIFF_HPS, 2 * DIFF_TQ, DIFF_HD), F32),
                        pltpu.VMEM((DIFF_HPS, 2 * DIFF_TQ, 2 * DIFF_HD), F32),
                        pltpu.VMEM((DIFF_HPS, 2, 2 * DIFF_TQ, DIFF_TK), F32)],
        compiler_params=pltpu.CompilerParams(
            dimension_semantics=("arbitrary", "arbitrary"), vmem_limit_bytes=VMEM_LIMIT_BYTES),
        name="diff_attention",
    )(q, k, v, lvec, subln.reshape(1, DIFF_HD))


GDN_TB = 256
GDN_HALO = 8
GDN_GATES = 2 * GDN_HEADS


def _softplus(x):
    return jnp.maximum(x, 0.0) + jnp.log(1.0 + jnp.exp(-jnp.abs(x)))


def _gdn_prep_kernel(*refs):
    (qp, qc, qn, kp, kc, kn, vp, vc, vn, ab_ref, w_ref, alog_ref, dtb_ref,
     qo_ref, ko_ref, vo_ref, go_ref, pad_ref) = refs
    i = pl.program_id(0)
    last = pl.num_programs(0) - 1
    prev_ok = (i >= 2).astype(F32)
    next_ok = jnp.logical_and(i >= 1, i < last).astype(F32)
    half = GDN_CONV // 2

    def conv_silu(prev_ref, cur_ref, next_ref, sec):
        pad_ref[0:GDN_HALO, :] = prev_ref[...] * prev_ok
        pad_ref[GDN_HALO:GDN_HALO + GDN_TB, :] = cur_ref[...]
        pad_ref[GDN_HALO + GDN_TB:, :] = next_ref[...] * next_ok
        acc = jnp.zeros((GDN_TB, GROUP_WIDTH), F32)
        for j in range(GDN_CONV):
            w = w_ref[j:j + 1, sec * GROUP_WIDTH:(sec + 1) * GROUP_WIDTH]
            acc = acc + pad_ref[GDN_HALO - half + j:GDN_HALO - half + j + GDN_TB, :] * w
        return acc * jax.nn.sigmoid(acc)

    def l2n(x):
        parts = []
        for h in range(GDN_HEADS):
            xh = x[:, h * GDN_HEAD_DIM:(h + 1) * GDN_HEAD_DIM]
            parts.append(xh * lax.rsqrt(jnp.sum(xh * xh, axis=1, keepdims=True) + EPS))
        return jnp.concatenate(parts, axis=1)

    qo_ref[...] = l2n(conv_silu(qp, qc, qn, 0)) * GDN_HEAD_DIM ** -0.5
    ko_ref[...] = l2n(conv_silu(kp, kc, kn, 1))
    vo_ref[...] = conv_silu(vp, vc, vn, 2)
    x = ab_ref[...]
    lane = lax.broadcasted_iota(jnp.int32, x.shape, 1)
    go_ref[...] = jnp.where(lane < GDN_GATES, -jnp.exp(alog_ref[...]) * _softplus(x + dtb_ref[...]),
                            jax.nn.sigmoid(x))


def _gdn_prep(p, conv_w, a_log, dt_bias):
    rows = p.shape[0]
    assert rows % GDN_TB == 0 and CTX_LEN == GDN_TB
    n_halo = rows // GDN_HALO
    per = GDN_TB // GDN_HALO
    c0 = P_GDN_QKV // GROUP_WIDTH
    specs = []
    for sec in range(3):
        specs += [
            pl.BlockSpec((GDN_HALO, GROUP_WIDTH), lambda i, sec=sec: (jnp.maximum(i * per - 1, 0), c0 + sec)),
            pl.BlockSpec((GDN_TB, GROUP_WIDTH), lambda i, sec=sec: (i, c0 + sec)),
            pl.BlockSpec((GDN_HALO, GROUP_WIDTH),
                         lambda i, sec=sec: (jnp.minimum((i + 1) * per, n_halo - 1), c0 + sec)),
        ]
    vec = pl.BlockSpec((1, 128), lambda i: (0, 0))
    specs += [pl.BlockSpec((GDN_TB, 128), lambda i: (i, P_GDN_AB // 128)),
              pl.BlockSpec((8, 3 * GROUP_WIDTH), lambda i: (0, 0)), vec, vec]
    blk = pl.BlockSpec((GDN_TB, GROUP_WIDTH), lambda i: (i, 0))
    out = jax.ShapeDtypeStruct((rows, GROUP_WIDTH), F32)
    pad8 = lambda t: jnp.pad(t.reshape(1, GDN_GATES), ((0, 0), (0, 128 - GDN_GATES)))
    return pl.pallas_call(
        _gdn_prep_kernel,
        grid=(rows // GDN_TB,),
        in_specs=specs,
        out_specs=[blk, blk, blk, pl.BlockSpec((GDN_TB, 128), lambda i: (i, 0))],
        out_shape=[out, out, out, jax.ShapeDtypeStruct((rows, 128), F32)],
        scratch_shapes=[pltpu.VMEM((GDN_TB + 2 * GDN_HALO, GROUP_WIDTH), F32)],
        compiler_params=pltpu.CompilerParams(
            dimension_semantics=("arbitrary",), vmem_limit_bytes=VMEM_LIMIT_BYTES),
        name="gdn_prep",
    )(*([p] * 10), jnp.pad(conv_w, ((0, 8 - GDN_CONV), (0, 0))), pad8(a_log), pad8(dt_bias))


def _mm3(a, b):
    ah = a.astype(BF16)
    al = (a - ah.astype(F32)).astype(BF16)
    bh = b.astype(BF16)
    bl = (b - bh.astype(F32)).astype(BF16)
    dot = lambda x, y: jnp.dot(x, y, preferred_element_type=F32)
    return dot(ah, bh) + dot(ah, bl) + dot(al, bh)


def _gdn_kernel(qf_ref, kf_ref, vf_ref, gf_ref, qb_ref, kb_ref, vb_ref, gb_ref, of_ref, ob_ref, s_ref):
    j = pl.program_id(0)
    tb, ck, hd = GDN_TB, GDN_CHUNK, GDN_HEAD_DIM
    n_ck = tb // ck

    @pl.when(j == 0)
    def _():
        s_ref[...] = jnp.zeros_like(s_ref)

    ri = lax.broadcasted_iota(jnp.int32, (tb, tb), 0)
    ci = lax.broadcasted_iota(jnp.int32, (tb, tb), 1)
    same = lambda n: (ri // n) == (ci // n)
    same_chunk = same(ck)
    eye = (ri == ci).astype(F32)
    dot = lambda x, y: jnp.dot(x, y, preferred_element_type=F32)
    tot_m = same_chunk.astype(BF16)

    def direction(reverse, g_ref):
        before = (ci > ri) if reverse else (ci < ri)
        strict = jnp.logical_and(same_chunk, before)
        incl = jnp.logical_and(same_chunk, jnp.logical_or(before, ri == ci))
        g = g_ref[...]
        g1 = g.astype(BF16)
        r1 = g - g1.astype(F32)
        g2 = r1.astype(BF16)
        g3 = (r1 - g2.astype(F32)).astype(BF16)
        cum_m = incl.astype(BF16)
        gcum = dot(cum_m, g1) + dot(cum_m, g2) + dot(cum_m, g3)
        gtot = dot(tot_m, g1) + dot(tot_m, g2) + dot(tot_m, g3)
        return dict(strict=strict, incl=incl, g=g, gcum=gcum, gtot=gtot, gcum_t=gcum.T,
                    order=list(range(n_ck - 1, -1, -1)) if reverse else list(range(n_ck)))

    dirs = [direction(False, gf_ref), direction(True, gb_ref)]
    refs = [(qf_ref, kf_ref, vf_ref, of_ref), (qb_ref, kb_ref, vb_ref, ob_ref)]
    chains = [(d, h) for d in range(2) for h in range(GDN_HEADS)]
    heads = range(len(chains))
    hs = [slice(h * hd, (h + 1) * hd) for _, h in chains]
    q_ref = [refs[d][0] for d, _ in chains]
    k_ref = [refs[d][1] for d, _ in chains]
    v_ref = [refs[d][2] for d, _ in chains]
    incl = [dirs[d]['incl'] for d, _ in chains]
    strict = [dirs[d]['strict'] for d, _ in chains]
    lanes = [d * GDN_HEADS + h for d, h in chains]
    gcol = [dirs[d]['gcum'][:, l:l + 1] for (d, _), l in zip(chains, lanes)]
    grow = [dirs[d]['gcum_t'][l:l + 1, :] for (d, _), l in zip(chains, lanes)]
    glast = [dirs[d]['gtot'][:, l:l + 1] for (d, _), l in zip(chains, lanes)]
    beta = [dirs[d]['g'][:, GDN_GATES + l:GDN_GATES + l + 1] for (d, _), l in zip(chains, lanes)]
    decay = [jnp.where(incl[h], jnp.exp(jnp.where(incl[h], gcol[h] - grow[h], 0.0)), 0.0) for h in heads]
    kb = [k_ref[h][:, hs[h]] * beta[h] for h in heads]
    k_b = [k_ref[h][:, hs[h]].astype(BF16) for h in heads]
    a_mat = [jnp.where(strict[h], _dot_nt(kb[h].astype(BF16), k_b[h]) * decay[h], 0.0) for h in heads]
    attn = [(_dot_nt(q_ref[h][:, hs[h]].astype(BF16), k_b[h]) * decay[h]).astype(BF16) for h in heads]

    x = [eye - jnp.where(same(2), a_mat[h], 0.0) for h in heads]
    for half_blk in (2, 4, 8, 16, 32):
        level = jnp.logical_and(same(2 * half_blk), jnp.logical_not(same(half_blk)))
        x_b = [x[h].astype(BF16) for h in heads]
        xl = [dot(x_b[h], jnp.where(level, a_mat[h], 0.0).astype(BF16)).astype(BF16) for h in heads]
        x = [x[h] - dot(xl[h], x_b[h]) for h in heads]
    resid = [eye - x[h] - _mm3(a_mat[h], x[h]) for h in heads]
    t_mat = [(x[h] + dot(x[h].astype(BF16), resid[h].astype(BF16))).astype(BF16) for h in heads]

    eg = [jnp.exp(gcol[h]) for h in heads]
    wu = [dot(t_mat[h], jnp.concatenate([kb[h] * eg[h], v_ref[h][:, hs[h]] * beta[h]], axis=1).astype(BF16))
          for h in heads]
    w_b = [wu[h][:, :hd].astype(BF16) for h in heads]
    qd_b = [(q_ref[h][:, hs[h]] * eg[h]).astype(BF16) for h in heads]
    k_carry = [(k_ref[h][:, hs[h]] * jnp.exp(glast[h] - gcol[h])).astype(BF16) for h in heads]

    s = [s_ref[h] for h in heads]
    v_new = [[None] * n_ck for _ in heads]
    o_inter = [[None] * n_ck for _ in heads]
    for step in range(n_ck):
        for h in heads:
            c = dirs[chains[h][0]]['order'][step]
            rows = slice(c * ck, (c + 1) * ck)
            s_b = s[h].astype(BF16)
            v_new[h][c] = wu[h][rows, hd:] - dot(w_b[h][rows], s_b)
            o_inter[h][c] = dot(qd_b[h][rows], s_b)
            s[h] = s[h] * jnp.exp(glast[h][c * ck:c * ck + 1, :]) + lax.dot_general(
                k_carry[h][rows], v_new[h][c].astype(BF16), (((0,), (0,)), ((), ())),
                preferred_element_type=F32)
    for h in heads:
        s_ref[h] = s[h]
        v_all = jnp.concatenate(v_new[h], axis=0).astype(BF16)
        refs[chains[h][0]][3][:, hs[h]] = jnp.concatenate(o_inter[h], axis=0) + dot(attn[h], v_all)


def _gdn_scan(q, k, v, gates):
    rows = q.shape[0]
    n_blocks = rows // GDN_TB
    bwd_of = lambda j: jnp.where(j == 0, 0, n_blocks - j)
    blk_f = pl.BlockSpec((GDN_TB, GROUP_WIDTH), lambda j: (j, 0))
    blk_b = pl.BlockSpec((GDN_TB, GROUP_WIDTH), lambda j: (bwd_of(j), 0))
    out = jax.ShapeDtypeStruct((rows, GROUP_WIDTH), F32)
    return pl.pallas_call(
        _gdn_kernel,
        grid=(n_blocks,),
        in_specs=[blk_f, blk_f, blk_f, pl.BlockSpec((GDN_TB, 128), lambda j: (j, 0)),
                  blk_b, blk_b, blk_b, pl.BlockSpec((GDN_TB, 128), lambda j: (bwd_of(j), 0))],
        out_specs=[blk_f, blk_b],
        out_shape=[out, out],
        scratch_shapes=[pltpu.VMEM((2 * GDN_HEADS, GDN_HEAD_DIM, GDN_HEAD_DIM), F32)],
        compiler_params=pltpu.CompilerParams(
            dimension_semantics=("arbitrary",), vmem_limit_bytes=VMEM_LIMIT_BYTES),
        name="gdn_scan",
    )(q, k, v, gates, q, k, v, gates)


def _gdn_finish_kernel(of_ref, ob_ref, z_ref, w_ref, y_ref):
    o = of_ref[...] + ob_ref[...]
    z = z_ref[...]
    parts = []
    for h in range(GDN_HEADS):
        oh = o[:, h * GDN_HEAD_DIM:(h + 1) * GDN_HEAD_DIM]
        parts.append(oh * lax.rsqrt(jnp.mean(oh * oh, axis=1, keepdims=True) + EPS) * w_ref[...])
    y_ref[...] = jnp.concatenate(parts, axis=1) * (z * jax.nn.sigmoid(z))


def _gdn_pallas(p, conv_w, a_log, dt_bias, norm_w):
    rows = p.shape[0]
    q, k, v, gates = _gdn_prep(p, conv_w, a_log, dt_bias)
    o_f, o_b = _gdn_scan(q, k, v, gates)
    blk = pl.BlockSpec((GDN_TB, GROUP_WIDTH), lambda i: (i, 0))
    return pl.pallas_call(
        _gdn_finish_kernel,
        grid=(rows // GDN_TB,),
        in_specs=[blk, blk, pl.BlockSpec((GDN_TB, GROUP_WIDTH), lambda i: (i, P_GDN_Z // GROUP_WIDTH)),
                  pl.BlockSpec((1, GDN_HEAD_DIM), lambda i: (0, 0))],
        out_specs=blk,
        out_shape=jax.ShapeDtypeStruct((rows, GROUP_WIDTH), F32),
        compiler_params=pltpu.CompilerParams(
            dimension_semantics=("arbitrary",), vmem_limit_bytes=VMEM_LIMIT_BYTES),
        name="gdn_finish",
    )(o_f, o_b, p, norm_w.reshape(1, GDN_HEAD_DIM))


def _reorder_w_in(w):
    front = w[..., 0:4096].astype(BF16)
    gdn_ab = w[..., 4096:4112].astype(BF16)
    na = w[..., 4112:5648].astype(BF16)
    pad = jnp.zeros(w.shape[:-1] + (P_WIDTH - P_GDN_AB - 16,), BF16)
    return jnp.concatenate([front, na, gdn_ab, pad], axis=-1)


def kernel(x, c, ctx, c_ctx, w_ada, b_ada, norm_ffn1, norm_mix, norm_ffn2, ffn1_w_in, ffn1_w_out,
           ffn2_w_in, ffn2_w_out, w_in, w_out, s5_lambda_re, s5_lambda_im, s5_log_step, s5_b_re,
           s5_b_im, s5_c_re, s5_c_im, s5_d, s5_w_glu, s5_b_glu, diff_q_norm, diff_k_norm,
           diff_lambda_q1, diff_lambda_k1, diff_lambda_q2, diff_lambda_k2, diff_subln, gdn_conv,
           gdn_a_log, gdn_dt_bias, gdn_norm, na_q_norm, na_k_norm, na_rpb):
    mod = _modulation(c, c_ctx, w_ada, b_ada)
    s = jnp.concatenate([ctx[0], x[0]], axis=0)
    ffn1_in, ffn1_out = ffn1_w_in.astype(BF16), ffn1_w_out.astype(BF16)
    ffn2_in, ffn2_out = ffn2_w_in.astype(BF16), ffn2_w_out.astype(BF16)
    w_in_bf, w_out_bf = _reorder_w_in(w_in), w_out.astype(BF16)
    s5_w_glu_bf = s5_w_glu.astype(BF16)
    for l in range(DEPTH):
        mod_l = mod[l]
        s = _ffn(s, mod_l, norm_ffn1[l], ffn1_in, ffn1_out, l, 0)
        p = _inproj(s, mod_l, norm_mix[l], w_in_bf, l)
        lam_init = 0.8 - 0.6 * math.exp(-0.3 * l)
        s5_f, s5_b = _s5_scan_call(p, *_s5_prepare(s5_lambda_re[l], s5_lambda_im[l], s5_log_step[l], s5_b_re[l],
                                                   s5_b_im[l], s5_c_re[l], s5_c_im[l]))
        yb = _diff_pallas(p, diff_q_norm[l], diff_k_norm[l], diff_lambda_q1[l], diff_lambda_k1[l],
                          diff_lambda_q2[l], diff_lambda_k2[l], diff_subln[l], lam_init)
        gdn_f, gdn_b = _gdn_scan(*_gdn_prep(p, gdn_conv[l], gdn_a_log[l], gdn_dt_bias[l]))
        yd = _na_pallas(p, na_q_norm[l], na_k_norm[l], na_rpb[l])
        s = _outproj_fused(s, mod_l, p, s5_f, s5_b, s5_d[l], s5_w_glu_bf[l], s5_b_glu[l], yb, gdn_f, gdn_b,
                           gdn_norm[l], yd, w_out_bf, l)
        s = _ffn(s, mod_l, norm_ffn2[l], ffn2_in, ffn2_out, l, 6)
    return s[None, CTX_LEN:]
```
